```python
import math
import jax, jax.numpy as jnp
from jax import lax
import numpy as np

D_MODEL = 1024
BATCH = 32
SEQ = 2048
DEPTH = 2

CTX_LEN = 256
GRID_W = 64
EPS = 1e-6
ROPE_BASE = 10000.0
Q_BLOCK = 128

DIFF_HEADS = 4
DIFF_DH = 64
DIFF_SCALE = DIFF_DH ** -0.5
MLA_HEADS = 4
MLA_Q_LORA = 256
MLA_KV_LORA = 128
MLA_NOPE = 128
MLA_ROPE = 64
MLA_V = 128
MLA_SCALE = (MLA_NOPE + MLA_ROPE) ** -0.5
EVEN_IN_SIZES = (DIFF_HEADS * 2 * DIFF_DH, DIFF_HEADS * 2 * DIFF_DH, DIFF_HEADS * 2 * DIFF_DH,
                 MLA_Q_LORA, MLA_KV_LORA, MLA_ROPE)
EVEN_IN_WIDTH = sum(EVEN_IN_SIZES)
EVEN_OUT_WIDTH = DIFF_HEADS * 2 * DIFF_DH + MLA_HEADS * MLA_V
NA_HEADS = 16
NA_DH = D_MODEL // NA_HEADS
NA_SCALE = NA_DH ** -0.5
WIN_ROWS = 8
WIN_COLS = 16
Q_COLS = 16
N_EXPERTS = 32
TOP_K = 4
D_FF = 1024
SWIGLU_ALPHA = 1.702
SWIGLU_LIMIT = 7.0
EXPERT_BLOCK = 256

N_EVEN = (DEPTH + 1) // 2
N_ODD = DEPTH // 2

kernel_name = 'hybrid_diffattn_mla_natten_moe_dit'


def rmsnorm(x, g):
    xf = x.astype(jnp.float32)
    y = xf * lax.rsqrt(jnp.mean(xf * xf, axis=-1, keepdims=True) + EPS) * g.astype(jnp.float32)
    return y.astype(x.dtype)


def modulate(h, shift, scale):
    return h * (1.0 + scale) + shift


def rope_1d(x, pos):
    d = x.shape[-1]
    half = d // 2
    inv = ROPE_BASE ** (-jnp.arange(half, dtype=jnp.float32) * 2.0 / d)
    ang = pos.astype(jnp.float32)[:, None] * inv[None, :]
    cos = jnp.cos(ang)[None, :, None, :]
    sin = jnp.sin(ang)[None, :, None, :]
    xf = x.astype(jnp.float32)
    x1, x2 = xf[..., :half], xf[..., half:]
    return jnp.concatenate([x1 * cos - x2 * sin, x2 * cos + x1 * sin], axis=-1).astype(x.dtype)


def axial_rope(x, row, col):
    half = x.shape[-1] // 2
    return jnp.concatenate([rope_1d(x[..., :half], row), rope_1d(x[..., half:], col)], axis=-1)


def _probs(q, k, scale):
    s = jnp.einsum('bqhd,bkhd->bhqk', q, k).astype(jnp.float32) * scale
    return jax.nn.softmax(s, axis=-1)


def softmax_attention(q, k, v, scale):
    p = _probs(q, k, scale).astype(v.dtype)
    return jnp.einsum('bhqk,bkhe->bqhe', p, v)


def diff_attention(q1, q2, k1, k2, v, lam, scale):
    a = _probs(q1, k1, scale) - lam * _probs(q2, k2, scale)
    return jnp.einsum('bhqk,bkhe->bqhe', a.astype(v.dtype), v)


def sweep_query_blocks(fn, *qs):
    B, S = qs[0].shape[:2]
    nb = S // Q_BLOCK
    to_blocks = lambda t: jnp.moveaxis(t.reshape(B, nb, Q_BLOCK, *t.shape[2:]), 1, 0)
    out = lax.map(lambda blk: fn(*blk), tuple(to_blocks(t) for t in qs))
    from_blocks = lambda t: jnp.moveaxis(t, 0, 1).reshape(B, S, *t.shape[3:])
    return tuple(from_blocks(t) for t in out)


def even_project(h, pos, w_in, q_norm_g, w_qb, kv_norm_g, w_kvb):
    Bn, L, _ = h.shape
    cuts = [int(s) for s in np.cumsum(EVEN_IN_SIZES)[:-1]]
    dq, dk, dv, cq, ckv, kpe = jnp.split(h @ w_in, cuts, axis=-1)
    q12 = dq.reshape(Bn, L, DIFF_HEADS, 2, DIFF_DH)
    k12 = dk.reshape(Bn, L, DIFF_HEADS, 2, DIFF_DH)
    v_d = dv.reshape(Bn, L, DIFF_HEADS, 2 * DIFF_DH)
    q_m = (rmsnorm(cq, q_norm_g) @ w_qb).reshape(Bn, L, MLA_HEADS, MLA_NOPE + MLA_ROPE)
    kv = (rmsnorm(ckv, kv_norm_g) @ w_kvb).reshape(Bn, L, MLA_HEADS, MLA_NOPE + MLA_V)
    rot = [q12[..., 0, :], q12[..., 1, :], k12[..., 0, :], k12[..., 1, :],
           q_m[..., MLA_NOPE:], kpe[:, :, None, :]]
    if pos is not None:
        rot = [axial_rope(t, pos[0], pos[1]) for t in rot]
    q1, q2, k1, k2, q_pe, k_pe = rot
    q_m = jnp.concatenate([q_m[..., :MLA_NOPE], q_pe], axis=-1)
    k_m = jnp.concatenate([kv[..., :MLA_NOPE], jnp.broadcast_to(k_pe, (Bn, L, MLA_HEADS, MLA_ROPE))], axis=-1)
    return q1, q2, k1, k2, v_d, q_m, k_m, kv[..., MLA_NOPE:]


def even_mixer(hl, hc, pos, w_in, lam_vec, subln_g, q_norm_g, w_qb, kv_norm_g, w_kvb, w_out,
               lam_init, need_ctx):
    lat = even_project(hl, pos, w_in, q_norm_g, w_qb, kv_norm_g, w_kvb)
    cx = even_project(hc, None, w_in, q_norm_g, w_qb, kv_norm_g, w_kvb)
    lv = lam_vec.astype(jnp.float32)
    lam = jnp.exp(jnp.sum(lv[0] * lv[1])) - jnp.exp(jnp.sum(lv[2] * lv[3])) + lam_init
    k1, k2, v_d, k_m, v_m = [jnp.concatenate([lat[i], cx[i]], axis=1) for i in (2, 3, 4, 6, 7)]

    def mix(q1, q2, q_m, k1, k2, v_d, k_m, v_m):
        return (diff_attention(q1, q2, k1, k2, v_d, lam, DIFF_SCALE),
                softmax_attention(q_m, k_m, v_m, MLA_SCALE))

    def merge(o_d, o_m):
        Bn, L = o_d.shape[:2]
        o_d = rmsnorm(o_d, subln_g) * (1.0 - lam_init)
        return jnp.concatenate([o_d.reshape(Bn, L, -1), o_m.reshape(Bn, L, -1)], axis=-1) @ w_out

    yl = merge(*sweep_query_blocks(lambda a, b, m: mix(a, b, m, k1, k2, v_d, k_m, v_m),
                                   lat[0], lat[1], lat[5]))
    yc = merge(*mix(cx[0], cx[1], cx[5], cx[2], cx[3], cx[4], cx[6], cx[7])) if need_ctx else None
    return yl, yc


def neighbourhood_attention(q, k, v, k_ctx, v_ctx, rpb):
    B, S, H, dh = q.shape
    rows = S // GRID_W
    kr = min(WIN_ROWS, rows)
    n_cb = GRID_W // Q_COLS
    kb = Q_COLS + WIN_COLS
    r = np.arange(rows)
    row_start = np.clip(r - WIN_ROWS // 2, 0, rows - kr)
    row_off = row_start[:, None] + np.arange(kr)[None, :] - r[:, None] + WIN_ROWS - 1
    cq = np.arange(GRID_W).reshape(n_cb, Q_COLS)
    cb_start = np.clip(np.arange(n_cb) * Q_COLS - WIN_COLS // 2, 0, GRID_W - kb)
    key_cols = cb_start[:, None] + np.arange(kb)[None, :]
    q_start = np.clip(cq - WIN_COLS // 2, 0, GRID_W - WIN_COLS)[:, :, None]
    kcol = key_cols[:, None, :]
    col_valid = (kcol >= q_start) & (kcol < q_start + WIN_COLS)
    col_off = np.clip(kcol - cq[:, :, None] + WIN_COLS - 1, 0, 2 * WIN_COLS - 2)
    bias = rpb.astype(jnp.float32)[:, row_off][..., col_off]
    bias = jnp.where(col_valid[None, None, None], bias, -jnp.inf)
    bias = bias.transpose(1, 0, 3, 4, 2, 5).reshape(rows, H, n_cb, Q_COLS, kr * kb)

    kg = k.reshape(B, rows, GRID_W, H, dh)
    vg = v.reshape(B, rows, GRID_W, H, dh)
    qg = jnp.moveaxis(q.reshape(B, rows, n_cb, Q_COLS, H, dh), 1, 0)
    n_loc = kr * kb

    def one_row(args):
        start, q_r, bias_r = args

        def gather(t):
            strip = lax.dynamic_slice_in_dim(t, start, kr, axis=1)
            blk = strip[:, :, key_cols]
            return blk.transpose(0, 2, 1, 3, 4, 5).reshape(B, n_cb, n_loc, H, dh)

        k_r, v_r = gather(kg), gather(vg)
        s_loc = jnp.einsum('bnqhd,bnkhd->bhnqk', q_r, k_r).astype(jnp.float32) * NA_SCALE + bias_r
        s_ctx = jnp.einsum('bnqhd,bchd->bhnqc', q_r, k_ctx).astype(jnp.float32) * NA_SCALE
        p = jax.nn.softmax(jnp.concatenate([s_loc, s_ctx], axis=-1), axis=-1).astype(v.dtype)
        return (jnp.einsum('bhnqk,bnkhd->bnqhd', p[..., :n_loc], v_r)
                + jnp.einsum('bhnqc,bchd->bnqhd', p[..., n_loc:], v_ctx))

    out = lax.map(one_row, (jnp.asarray(row_start, jnp.int32), qg, bias))
    return jnp.moveaxis(out, 0, 1).reshape(B, S, H, dh)


def odd_mixer(hl, hc, w_qkv, rpb, w_out, need_ctx):
    B, S, D = hl.shape
    C = hc.shape[1]
    q, k, v = [t.reshape(B, S, NA_HEADS, NA_DH) for t in jnp.split(hl @ w_qkv, 3, axis=-1)]
    if need_ctx:
        qc, kc, vc = [t.reshape(B, C, NA_HEADS, NA_DH) for t in jnp.split(hc @ w_qkv, 3, axis=-1)]
    else:
        kc, vc = [t.reshape(B, C, NA_HEADS, NA_DH) for t in jnp.split(hc @ w_qkv[:, D:], 2, axis=-1)]
    yl = neighbourhood_attention(q, k, v, kc, vc, rpb).reshape(B, S, D) @ w_out
    yc = softmax_attention(qc, kc, vc, NA_SCALE).reshape(B, C, D) @ w_out if need_ctx else None
    return yl, yc


def moe(h, router_w, router_b, w1, b1, w2, b2):
    T, D = h.shape
    TK = T * TOP_K
    logits = (h @ router_w + router_b).astype(jnp.float32)
    top_v, top_e = lax.top_k(logits, TOP_K)
    gate_w = jax.nn.softmax(top_v, axis=-1).reshape(TK)
    e_flat = top_e.reshape(TK)
    order = jnp.argsort(e_flat)
    e_sorted = e_flat[order]
    counts = jnp.bincount(e_flat, length=N_EXPERTS)
    padded = (counts + EXPERT_BLOCK - 1) // EXPERT_BLOCK * EXPERT_BLOCK
    start = jnp.cumsum(counts) - counts
    pad_end = jnp.cumsum(padded)
    dest = pad_end[e_sorted] - padded[e_sorted] + jnp.arange(TK) - start[e_sorted]
    n_blocks = -(-TK // EXPERT_BLOCK) + N_EXPERTS
    M = n_blocks * EXPERT_BLOCK
    row_tok = jnp.zeros((M,), jnp.int32).at[dest].set((order // TOP_K).astype(jnp.int32))
    row_w = jnp.zeros((M,), jnp.float32).at[dest].set(gate_w[order])
    blk_e = jnp.minimum(jnp.searchsorted(pad_end, jnp.arange(n_blocks) * EXPERT_BLOCK, side='right'),
                        N_EXPERTS - 1)

    def expert_block(args):
        tok, e = args
        gu = (h[tok] @ w1[e] + b1[e]).astype(jnp.float32)
        gate = jnp.minimum(gu[:, :D_FF], SWIGLU_LIMIT)
        lin = jnp.clip(gu[:, D_FF:], -SWIGLU_LIMIT, SWIGLU_LIMIT)
        act = gate * jax.nn.sigmoid(SWIGLU_ALPHA * gate) * (lin + 1.0)
        return act.astype(h.dtype) @ w2[e] + b2[e]

    out = lax.map(expert_block, (row_tok.reshape(n_blocks, EXPERT_BLOCK), blk_e))
    out = out.reshape(M, D) * row_w[:, None].astype(h.dtype)
    return jax.ops.segment_sum(out, row_tok, num_segments=T)


def setup_inputs(seed: int = 0) -> dict:
    key = jax.random.key(seed)
    ks = iter(jax.random.split(key, 32))
    D = D_MODEL
    nrm = lambda shape, s: jax.random.normal(next(ks), shape, jnp.float32) * s
    gain = lambda shape: 1.0 + nrm(shape, 0.05)
    return {
        'x': nrm((BATCH, SEQ, D), 1.0),
        'c': nrm((BATCH, D), 1.0),
        'ctx': nrm((BATCH, CTX_LEN, D), 1.0),
        'c_ctx': nrm((D,), 1.0),
        'ada_w': nrm((DEPTH, D, 6 * D), 0.5 * D ** -0.5),
        'ada_b': nrm((DEPTH, 6 * D), 0.01),
        'mix_pre_g': gain((DEPTH, D)),
        'mix_post_g': gain((DEPTH, D)),
        'ffn_pre_g': gain((DEPTH, D)),
        'ffn_post_g': gain((DEPTH, D)),
        'even_w_in': nrm((N_EVEN, D, EVEN_IN_WIDTH), D ** -0.5),
        'diff_lambda': nrm((N_EVEN, 4, DIFF_DH), 0.1),
        'diff_subln_g': gain((N_EVEN, 2 * DIFF_DH)),
        'mla_q_norm_g': gain((N_EVEN, MLA_Q_LORA)),
        'mla_w_qb': nrm((N_EVEN, MLA_Q_LORA, MLA_HEADS * (MLA_NOPE + MLA_ROPE)), MLA_Q_LORA ** -0.5),
        'mla_kv_norm_g': gain((N_EVEN, MLA_KV_LORA)),
        'mla_w_kvb': nrm((N_EVEN, MLA_KV_LORA, MLA_HEADS * (MLA_NOPE + MLA_V)), MLA_KV_LORA ** -0.5),
        'even_w_out': nrm((N_EVEN, EVEN_OUT_WIDTH, D), EVEN_OUT_WIDTH ** -0.5),
        'na_w_qkv': nrm((N_ODD, D, 3 * D), D ** -0.5),
        'na_rpb': nrm((N_ODD, NA_HEADS, 2 * WIN_ROWS - 1, 2 * WIN_COLS - 1), 0.1),
        'na_w_out': nrm((N_ODD, D, D), D ** -0.5),
        'router_w': nrm((DEPTH, D, N_EXPERTS), D ** -0.5),
        'router_b': nrm((DEPTH, N_EXPERTS), 0.01),
        'moe_w1': nrm((DEPTH, N_EXPERTS, D, 2 * D_FF), D ** -0.5),
        'moe_b1': nrm((DEPTH, N_EXPERTS, 2 * D_FF), 0.01),
        'moe_w2': nrm((DEPTH, N_EXPERTS, D_FF, D), D_FF ** -0.5),
        'moe_b2': nrm((DEPTH, N_EXPERTS, D), 0.01),
    }


def reference(x, c, ctx, c_ctx, ada_w, ada_b, mix_pre_g, mix_post_g, ffn_pre_g, ffn_post_g,
              even_w_in, diff_lambda, diff_subln_g, mla_q_norm_g, mla_w_qb, mla_kv_norm_g, mla_w_kvb,
              even_w_out, na_w_qkv, na_rpb, na_w_out, router_w, router_b, moe_w1, moe_b1, moe_w2, moe_b2):
    B, S, D = x.shape
    C = ctx.shape[1]
    t = jnp.arange(S)
    pos = (t // GRID_W, t % GRID_W)
    xl, xc = x, ctx
    for l in range(DEPTH):
        last = l == DEPTH - 1
        i = l // 2
        mod_l = jnp.split((jax.nn.silu(c) @ ada_w[l] + ada_b[l])[:, None, :], 6, axis=-1)
        mod_c = jnp.split(jax.nn.silu(c_ctx) @ ada_w[l] + ada_b[l], 6, axis=-1)
        hl = modulate(rmsnorm(xl, mix_pre_g[l]), mod_l[0], mod_l[1])
        hc = modulate(rmsnorm(xc, mix_pre_g[l]), mod_c[0], mod_c[1])
        if l % 2 == 0:
            lam_init = 0.8 - 0.6 * math.exp(-0.3 * l)
            yl, yc = even_mixer(hl, hc, pos, even_w_in[i], diff_lambda[i], diff_subln_g[i],
                                mla_q_norm_g[i], mla_w_qb[i], mla_kv_norm_g[i], mla_w_kvb[i],
                                even_w_out[i], lam_init, not last)
        else:
            yl, yc = odd_mixer(hl, hc, na_w_qkv[i], na_rpb[i], na_w_out[i], not last)
        xl = xl + mod_l[2] * rmsnorm(yl, mix_post_g[l])
        hl = modulate(rmsnorm(xl, ffn_pre_g[l]), mod_l[3], mod_l[4])
        if last:
            yl = moe(hl.reshape(B * S, D), router_w[l], router_b[l], moe_w1[l], moe_b1[l],
                     moe_w2[l], moe_b2[l]).reshape(B, S, D)
        else:
            xc = xc + mod_c[2] * rmsnorm(yc, mix_post_g[l])
            hc = modulate(rmsnorm(xc, ffn_pre_g[l]), mod_c[3], mod_c[4])
            y = moe(jnp.concatenate([hl.reshape(B * S, D), hc.reshape(B * C, D)], axis=0),
                    router_w[l], router_b[l], moe_w1[l], moe_b1[l], moe_w2[l], moe_b2[l])
            yl = y[:B * S].reshape(B, S, D)
            xc = xc + mod_c[5] * rmsnorm(y[B * S:].reshape(B, C, D), ffn_post_g[l])
        xl = xl + mod_l[5] * rmsnorm(yl, ffn_post_g[l])
    return xl
```

```python
import functools
import math

import numpy as np
import jax
import jax.numpy as jnp
from jax import lax
from jax.experimental import pallas as pl
from jax.experimental.pallas import tpu as pltpu

F32 = jnp.float32
BF16 = jnp.bfloat16

D = 1024
SEQ = 2048
CTX = 256
GRID_W = 64
GRID_H = SEQ // GRID_W
TILE = 256
LAT_TILES = SEQ // TILE
SMP_TILES = LAT_TILES + 1
SMP_ROWS = SEQ + CTX
EPS = 1e-6
ROPE_BASE = 10000.0

DIFF_HEADS = 4
DIFF_DH = 64
DIFF_SCALE = DIFF_DH ** -0.5
MLA_HEADS = 4
MLA_Q_LORA = 256
MLA_KV_LORA = 128
MLA_NOPE = 128
MLA_ROPE = 64
MLA_V = 128
MLA_SCALE = (MLA_NOPE + MLA_ROPE) ** -0.5
MLA_QK_PAD = 256
NA_HEADS = 16
NA_DH = 64
NA_SCALE = NA_DH ** -0.5
WIN_ROWS = 8
WIN_COLS = 16
N_EXPERTS = 32
TOP_K = 4
D_FF = 1024
SWIGLU_ALPHA = 1.702
SWIGLU_LIMIT = 7.0
EXPERT_BLOCK = 256
MASK_VALUE = -1e30
META_LANES = 128
VMEM_LIMIT = 60 * 1024 * 1024

_NT = (((1,), (1,)), ((), ()))


def _dot(a, b):
    return jnp.dot(a, b, preferred_element_type=F32)


def _dot_nt(a, b):
    return lax.dot_general(a, b, _NT, preferred_element_type=F32)


def _dot_f32(a, b):
    return jnp.dot(a, b, preferred_element_type=F32, precision=lax.Precision.HIGHEST)


def _rms(x, g):
    return x * lax.rsqrt(jnp.mean(x * x, axis=-1, keepdims=True) + EPS) * g


def _mod(mod_ref, k):
    return mod_ref[0, :, k * D:(k + 1) * D]


def _params(sem):
    return pltpu.CompilerParams(dimension_semantics=sem, vmem_limit_bytes=VMEM_LIMIT)


def _ada_kernel(c_ref, w_ref, b_ref, o_ref):
    c = c_ref[...]
    s = c * jax.nn.sigmoid(c)
    o_ref[0] = _dot_f32(s, w_ref[0]) + b_ref[0]


def _ada(cc, ada_w, ada_b):
    depth = ada_w.shape[0]
    n = cc.shape[0]
    nt = 6 * D // D
    return pl.pallas_call(
        _ada_kernel,
        grid=(depth, nt),
        in_specs=[pl.BlockSpec((n, D), lambda l, j: (0, 0)),
                  pl.BlockSpec((1, D, D), lambda l, j: (l, 0, j)),
                  pl.BlockSpec((1, 1, D), lambda l, j: (l, 0, j))],
        out_specs=pl.BlockSpec((1, n, D), lambda l, j: (l, 0, j)),
        out_shape=jax.ShapeDtypeStruct((depth, n, 6 * D), F32),
        compiler_params=_params(("arbitrary", "arbitrary")),
        name="ada",
    )(cc, ada_w, ada_b.reshape(depth, 1, 6 * D))


def _x_lat_map(i):
    return ((i // SMP_TILES) * LAT_TILES + jnp.minimum(i % SMP_TILES, LAT_TILES - 1), 0)


def _x_ctx_map(i):
    return (i // SMP_TILES, 0)


def _mod_map_all(nb):
    return lambda i: (jnp.where(i % SMP_TILES == LAT_TILES, nb, i // SMP_TILES), 0, 0)


def _lat_of_all(i):
    return (i // LAT_TILES) * SMP_TILES + i % LAT_TILES


def _rope(x, cos, sa, sb):
    return x * cos + pltpu.roll(x, 112, 1) * sa + pltpu.roll(x, 16, 1) * sb


def _proj0_kernel(x_ref, c_ref, mod_ref, g_ref, wa_ref, qng_ref, wqb_ref, kvng_ref, wkvb_ref,
                  cos_ref, sa_ref, sb_ref, qd_ref, kd_ref, vd_ref, qm_ref, km_ref, vm_ref):
    is_ctx = pl.program_id(0) % SMP_TILES == LAT_TILES
    x = jnp.where(is_ctx, c_ref[...], x_ref[...])
    h = (_rms(x, g_ref[...]) * (1.0 + _mod(mod_ref, 1)) + _mod(mod_ref, 0)).astype(BF16)
    big = _dot(h, wa_ref[...])
    cos, sa, sb = cos_ref[...], sa_ref[...], sb_ref[...]
    nd = DIFF_HEADS * 2 * DIFF_DH
    for j in range(nd // 128):
        sl = slice(j * 128, (j + 1) * 128)
        qd_ref[:, sl] = _rope(big[:, j * 128:(j + 1) * 128], cos, sa, sb).astype(BF16)
        kd_ref[:, sl] = _rope(big[:, nd + j * 128:nd + (j + 1) * 128], cos, sa, sb).astype(BF16)
    vd_ref[...] = big[:, 2 * nd:3 * nd].astype(BF16)
    o = 3 * nd
    cq = big[:, o:o + MLA_Q_LORA]
    ckv = big[:, o + MLA_Q_LORA:o + MLA_Q_LORA + MLA_KV_LORA]
    kpe = _rope(big[:, o + MLA_Q_LORA + MLA_KV_LORA:], cos, sa, sb).astype(BF16)
    qm = _dot(_rms(cq, qng_ref[...]).astype(BF16), wqb_ref[...])
    kv = _dot(_rms(ckv, kvng_ref[...]).astype(BF16), wkvb_ref[...])
    for hd in range(MLA_HEADS):
        b0 = hd * MLA_QK_PAD
        qm_ref[:, b0:b0 + 128] = qm[:, b0:b0 + 128].astype(BF16)
        qm_ref[:, b0 + 128:b0 + 256] = _rope(qm[:, b0 + 128:b0 + 256], cos, sa, sb).astype(BF16)
        km_ref[:, b0:b0 + 128] = kv[:, hd * 256:hd * 256 + 128].astype(BF16)
        km_ref[:, b0 + 128:b0 + 256] = kpe
        vm_ref[:, hd * 128:(hd + 1) * 128] = kv[:, hd * 256 + 128:(hd + 1) * 256].astype(BF16)


def _rope_tables():
    t = np.arange(SEQ)
    half = 16
    inv = (ROPE_BASE ** (-np.arange(half, dtype=np.float32) * 2.0 / (2 * half))).astype(np.float32)
    ang_r = (t // GRID_W).astype(np.float32)[:, None] * inv[None, :]
    ang_c = (t % GRID_W).astype(np.float32)[:, None] * inv[None, :]
    cr, sr, cc, sc = np.cos(ang_r), np.sin(ang_r), np.cos(ang_c), np.sin(ang_c)
    z = np.zeros_like(cr)
    cos = np.concatenate([cr, cr, cc, cc], axis=1)
    sa = np.concatenate([-sr, z, -sc, z], axis=1)
    sb = np.concatenate([z, sr, z, sc], axis=1)

    def full(tab, fill):
        tab = np.tile(tab, (1, 2))
        ident = np.full((CTX, 128), fill, np.float32)
        return jnp.asarray(np.concatenate([tab, ident], axis=0).astype(np.float32))

    return full(cos, 1.0), full(sa, 0.0), full(sb, 0.0)


def _proj0(x2, c2, mods, g, wa, qng, wqb, kvng, wkvb, nb):
    n_tiles = nb * SMP_TILES
    rows = nb * SMP_ROWS
    cos, sa, sb = _rope_tables()
    tile_map = lambda i: (i, 0)
    const = lambda i: (0, 0)
    tab_spec = pl.BlockSpec((TILE, 128), lambda i: (i % SMP_TILES, 0))
    widths = (512, 512, 512, 1024, 1024, 512)
    return pl.pallas_call(
        _proj0_kernel,
        grid=(n_tiles,),
        in_specs=[pl.BlockSpec((TILE, D), _x_lat_map),
                  pl.BlockSpec((TILE, D), _x_ctx_map),
                  pl.BlockSpec((1, 1, 6 * D), _mod_map_all(nb)),
                  pl.BlockSpec((1, D), const),
                  pl.BlockSpec(wa.shape, const),
                  pl.BlockSpec((1, MLA_Q_LORA), const),
                  pl.BlockSpec(wqb.shape, const),
                  pl.BlockSpec((1, MLA_KV_LORA), const),
                  pl.BlockSpec(wkvb.shape, const),
                  tab_spec, tab_spec, tab_spec],
        out_specs=[pl.BlockSpec((TILE, w), tile_map) for w in widths],
        out_shape=[jax.ShapeDtypeStruct((rows, w), BF16) for w in widths],
        compiler_params=_params(("arbitrary",)),
        name="proj_even",
    )(x2, c2, mods, g, wa, qng, wqb, kvng, wkvb, cos, sa, sb)


def _softmax_parts(s):
    m = jnp.max(s, axis=-1, keepdims=True)
    p = jnp.exp(s - m)
    return p, jnp.sum(p, axis=-1, keepdims=True)


def _attn0_kernel(lam_init, qd_ref, kd_ref, vd_ref, qm_ref, km_ref, vm_ref, lam_ref, sg_ref, o_ref):
    lv = lam_ref[...]
    lam = (jnp.exp(jnp.sum(lv[0:1] * lv[1:2], axis=1, keepdims=True))
           - jnp.exp(jnp.sum(lv[2:3] * lv[3:4], axis=1, keepdims=True)) + lam_init)
    lane = lax.broadcasted_iota(jnp.int32, (TILE, 128), 1)

    def heads(k_lo, nk):
        for hd in range(DIFF_HEADS):
            sl = slice(hd * 128, (hd + 1) * 128)
            q = qd_ref[:, sl]
            zero = jnp.zeros_like(q)
            qq = jnp.concatenate([jnp.where(lane < DIFF_DH, q, zero),
                                  jnp.where(lane >= DIFF_DH, q, zero)], axis=0)
            s = _dot_nt(qq, kd_ref[k_lo:k_lo + nk, sl]) * DIFF_SCALE
            p, l = _softmax_parts(s)
            pv = _dot(p.astype(BF16), vd_ref[k_lo:k_lo + nk, sl])
            o = pv[:TILE] / l[:TILE] - lam * (pv[TILE:] / l[TILE:])
            o_ref[:, sl] = (_rms(o, sg_ref[...]) * (1.0 - lam_init)).astype(BF16)
        for hd in range(MLA_HEADS):
            ql = slice(hd * MLA_QK_PAD, (hd + 1) * MLA_QK_PAD)
            s = _dot_nt(qm_ref[:, ql], km_ref[k_lo:k_lo + nk, ql]) * MLA_SCALE
            p, l = _softmax_parts(s)
            pv = _dot(p.astype(BF16), vm_ref[k_lo:k_lo + nk, hd * 128:(hd + 1) * 128])
            o_ref[:, 512 + hd * 128:512 + (hd + 1) * 128] = (pv / l).astype(BF16)

    is_ctx = pl.program_id(1) == LAT_TILES

    @pl.when(jnp.logical_not(is_ctx))
    def _():
        heads(0, SMP_ROWS)

    @pl.when(is_ctx)
    def _():
        heads(SEQ, CTX)


def _attn0(qd, kd, vd, qm, km, vm, lam_vec, subln_g, lam_init, nb):
    q_map = lambda b, p: (b * SMP_TILES + p, 0)
    k_map = lambda b, p: (b, 0)
    const = lambda b, p: (0, 0)
    return pl.pallas_call(
        functools.partial(_attn0_kernel, lam_init),
        grid=(nb, SMP_TILES),
        in_specs=[pl.BlockSpec((TILE, 512), q_map),
                  pl.BlockSpec((SMP_ROWS, 512), k_map),
                  pl.BlockSpec((SMP_ROWS, 512), k_map),
                  pl.BlockSpec((TILE, 1024), q_map),
                  pl.BlockSpec((SMP_ROWS, 1024), k_map),
                  pl.BlockSpec((SMP_ROWS, 512), k_map),
                  pl.BlockSpec(lam_vec.shape, const),
                  pl.BlockSpec((1, 128), const)],
        out_specs=pl.BlockSpec((TILE, D), q_map),
        out_shape=jax.ShapeDtypeStruct((nb * SMP_ROWS, D), BF16),
        compiler_params=_params(("arbitrary", "arbitrary")),
        name="attn_even",
    )(qd, kd, vd, qm, km, vm, lam_vec, subln_g)


def _oproj_kernel(dual, *refs):
    if dual:
        (o_ref, xl_ref, xc_ref, mod_ref, wo_ref, pg_ref, fg_ref, rw_ref, rb_ref,
         xn_ref, hf_ref, mi_ref, mw_ref, cnt_ref, run_ref) = refs
        is_ctx = pl.program_id(0) % SMP_TILES == LAT_TILES
        x = jnp.where(is_ctx, xc_ref[...], xl_ref[...])
    else:
        (o_ref, x_ref, mod_ref, wo_ref, pg_ref, fg_ref, rw_ref, rb_ref,
         xn_ref, hf_ref, mi_ref, mw_ref, cnt_ref, run_ref) = refs
        x = x_ref[...]

    @pl.when(pl.program_id(0) == 0)
    def _():
        run_ref[...] = jnp.zeros_like(run_ref)

    y = _dot(o_ref[...], wo_ref[...])
    xn = x + _mod(mod_ref, 2) * _rms(y, pg_ref[...])
    xn_ref[...] = xn
    hf = _rms(xn, fg_ref[...]) * (1.0 + _mod(mod_ref, 4)) + _mod(mod_ref, 3)
    hf_ref[...] = hf

    logits = _dot_f32(hf, rw_ref[...]) + rb_ref[...]
    eid = lax.broadcasted_iota(jnp.int32, (TILE, N_EXPERTS), 1).astype(F32)
    sels, vals, idxs = [], [], []
    for _ in range(TOP_K):
        m = jnp.max(logits, axis=-1, keepdims=True)
        idx = jnp.min(jnp.where(logits == m, eid, float(N_EXPERTS)), axis=-1, keepdims=True)
        sel = eid == idx
        sels.append(sel)
        vals.append(m)
        idxs.append(idx)
        logits = jnp.where(sel, -jnp.inf, logits)
    ex = [jnp.exp(v - vals[0]) for v in vals]
    den = ex[0] + ex[1] + ex[2] + ex[3]

    onehot = (sels[0] | sels[1] | sels[2] | sels[3]).astype(F32)
    r_i = lax.broadcasted_iota(jnp.int32, (TILE, TILE), 0)
    c_i = lax.broadcasted_iota(jnp.int32, (TILE, TILE), 1)
    tri = (c_i < r_i).astype(BF16)
    before = _dot(tri, onehot.astype(BF16)) + run_ref[...]
    lane = lax.broadcasted_iota(jnp.int32, (TILE, META_LANES), 1)
    mi = jnp.zeros((TILE, META_LANES), jnp.int32)
    mw = jnp.zeros((TILE, META_LANES), F32)
    for k in range(TOP_K):
        rank = jnp.sum(jnp.where(sels[k], before, 0.0), axis=-1, keepdims=True).astype(jnp.int32)
        mi = jnp.where(lane == k, idxs[k].astype(jnp.int32), mi)
        mi = jnp.where(lane == TOP_K + k, rank, mi)
        mw = jnp.where(lane == k, ex[k] / den, mw)
    mi_ref[...] = mi
    mw_ref[...] = mw
    run_ref[...] += jnp.sum(onehot, axis=0, keepdims=True)
    cnt_ref[...] = jnp.broadcast_to(run_ref[...], cnt_ref.shape)


def _oproj(o, x_args, mods, wo, pg, fg, rw, rb, nb, all_tokens):
    const = lambda i: (0, 0)
    tile_map = lambda i: (i, 0)
    if all_tokens:
        n_tiles = nb * SMP_TILES
        x_specs = [pl.BlockSpec((TILE, D), _x_lat_map), pl.BlockSpec((TILE, D), _x_ctx_map)]
        mod_map = _mod_map_all(nb)
    else:
        n_tiles = nb * LAT_TILES
        x_specs = [pl.BlockSpec((TILE, D), lambda i: (_lat_of_all(i), 0))]
        mod_map = lambda i: (i // LAT_TILES, 0, 0)
    rows = n_tiles * TILE
    return pl.pallas_call(
        functools.partial(_oproj_kernel, all_tokens),
        grid=(n_tiles,),
        in_specs=[pl.BlockSpec((TILE, D), tile_map)] + x_specs + [
            pl.BlockSpec((1, 1, 6 * D), mod_map),
            pl.BlockSpec((D, D), const),
            pl.BlockSpec((1, D), const),
            pl.BlockSpec((1, D), const),
            pl.BlockSpec((D, N_EXPERTS), const),
            pl.BlockSpec((1, N_EXPERTS), const)],
        out_specs=[pl.BlockSpec((TILE, D), tile_map),
                   pl.BlockSpec((TILE, D), tile_map),
                   pl.BlockSpec((TILE, META_LANES), tile_map),
                   pl.BlockSpec((TILE, META_LANES), tile_map),
                   pl.BlockSpec((8, N_EXPERTS), const)],
        out_shape=[jax.ShapeDtypeStruct((rows, D), F32),
                   jax.ShapeDtypeStruct((rows, D), F32),
                   jax.ShapeDtypeStruct((rows, META_LANES), jnp.int32),
                   jax.ShapeDtypeStruct((rows, META_LANES), F32),
                   jax.ShapeDtypeStruct((8, N_EXPERTS), F32)],
        scratch_shapes=[pltpu.VMEM((1, N_EXPERTS), F32)],
        compiler_params=_params(("arbitrary",)),
        name="oproj_router",
    )(o, *x_args, mods, wo, pg, fg, rw, rb)


def _row_copy(src, s, dst, d, sem):
    return pltpu.make_async_copy(src.at[pl.ds(s, 1)], dst.at[pl.ds(d, 1)], sem)


def _dispatch_kernel(dest_ref, h_ref, xs_ref, sem):
    n = TILE * TOP_K

    def start(j, c):
        _row_copy(h_ref, j // TOP_K, xs_ref, dest_ref[0, 0, j], sem).start()
        return c

    def wait(j, c):
        _row_copy(h_ref, j // TOP_K, xs_ref, dest_ref[0, 0, j], sem).wait()
        return c

    lax.fori_loop(0, n, start, 0)
    lax.fori_loop(0, n, wait, 0)


def _dispatch(dest, hf, m_rows):
    n_tiles = hf.shape[0] // TILE
    return pl.pallas_call(
        _dispatch_kernel,
        grid=(n_tiles,),
        in_specs=[pl.BlockSpec((1, 1, TILE * TOP_K), lambda i: (i, 0, 0), memory_space=pltpu.SMEM),
                  pl.BlockSpec((TILE, D), lambda i: (i, 0))],
        out_specs=pl.BlockSpec(memory_space=pl.ANY),
        out_shape=jax.ShapeDtypeStruct((m_rows, D), F32),
        scratch_shapes=[pltpu.SemaphoreType.DMA],
        compiler_params=_params(("arbitrary",)),
        name="moe_dispatch",
    )(dest, hf)


def _expert_kernel(be_ref, nv_ref, nu_ref, xs_ref, w1_ref, b1_ref, w2_ref, b2_ref, ys_ref):
    i = pl.program_id(0)

    @pl.when(i < nu_ref[0])
    def _():
        rows = lax.broadcasted_iota(jnp.int32, (EXPERT_BLOCK, 1), 0)
        x = jnp.where(rows < nv_ref[i], xs_ref[...], 0.0).astype(BF16)
        gu = _dot(x, w1_ref[0]) + b1_ref[0]
        gate = jnp.minimum(gu[:, :D_FF], SWIGLU_LIMIT)
        lin = jnp.clip(gu[:, D_FF:], -SWIGLU_LIMIT, SWIGLU_LIMIT)
        act = gate * jax.nn.sigmoid(SWIGLU_ALPHA * gate) * (lin + 1.0)
        ys_ref[...] = _dot(act.astype(BF16), w2_ref[0]) + b2_ref[0]


def _experts(blk_e, blk_valid, n_used, xs, w1, b1, w2, b2):
    n_blocks = xs.shape[0] // EXPERT_BLOCK
    row_map = lambda i, be, nv, nu: (jnp.minimum(i, nu[0] - 1), 0)
    e_map = lambda i, be, nv, nu: (be[i], 0, 0)
    grid_spec = pltpu.PrefetchScalarGridSpec(
        num_scalar_prefetch=3,
        grid=(n_blocks,),
        in_specs=[pl.BlockSpec((EXPERT_BLOCK, D), row_map),
                  pl.BlockSpec((1, D, 2 * D_FF), e_map),
                  pl.BlockSpec((1, 1, 2 * D_FF), e_map),
                  pl.BlockSpec((1, D_FF, D), e_map),
                  pl.BlockSpec((1, 1, D), e_map)],
        out_specs=pl.BlockSpec((EXPERT_BLOCK, D), row_map))
    return pl.pallas_call(
        _expert_kernel,
        grid_spec=grid_spec,
        out_shape=jax.ShapeDtypeStruct(xs.shape, F32),
        compiler_params=_params(("arbitrary",)),
        name="moe_experts",
    )(blk_e, blk_valid, n_used, xs, w1, b1.reshape(N_EXPERTS, 1, 2 * D_FF), w2, b2.reshape(N_EXPERTS, 1, D))


def _combine_kernel(dual, *refs):
    if dual:
        dest_ref, mw_ref, x_ref, mod_ref, g_ref, ys_ref, o_ref, buf, sem = refs
    else:
        dest_ref, mw_ref, x_ref, mod_ref, g_ref, ys_ref, o_ref, buf, sem = refs
    n = TILE * TOP_K

    def copy(j):
        return _row_copy(ys_ref, dest_ref[0, 0, j], buf.at[j % TOP_K], j // TOP_K, sem)

    def start(j, c):
        copy(j).start()
        return c

    def wait(j, c):
        copy(j).wait()
        return c

    lax.fori_loop(0, n, start, 0)
    lax.fori_loop(0, n, wait, 0)
    w = mw_ref[...]
    y = buf[0] * w[:, 0:1]
    for k in range(1, TOP_K):
        y = y + buf[k] * w[:, k:k + 1]
    o_ref[...] = x_ref[...] + _mod(mod_ref, 5) * _rms(y, g_ref[...])


def _combine(dest, mw, xn, mods, g, ys, nb, all_tokens):
    n_tiles = xn.shape[0] // TILE
    tile_map = lambda i: (i, 0)
    mod_map = _mod_map_all(nb) if all_tokens else (lambda i: (i // LAT_TILES, 0, 0))
    return pl.pallas_call(
        functools.partial(_combine_kernel, all_tokens),
        grid=(n_tiles,),
        in_specs=[pl.BlockSpec((1, 1, TILE * TOP_K), lambda i: (i, 0, 0), memory_space=pltpu.SMEM),
                  pl.BlockSpec((TILE, META_LANES), tile_map),
                  pl.BlockSpec((TILE, D), tile_map),
                  pl.BlockSpec((1, 1, 6 * D), mod_map),
                  pl.BlockSpec((1, D), lambda i: (0, 0)),
                  pl.BlockSpec(memory_space=pl.ANY)],
        out_specs=pl.BlockSpec((TILE, D), tile_map),
        out_shape=jax.ShapeDtypeStruct(xn.shape, F32),
        scratch_shapes=[pltpu.VMEM((TOP_K, TILE, D), F32), pltpu.SemaphoreType.DMA],
        compiler_params=_params(("arbitrary",)),
        name="moe_combine",
    )(dest, mw, xn, mods, g, ys)


def _moe(hf, mi, mw, cnt, xn, mods, g, w1, b1, w2, b2, nb, all_tokens):
    t = hf.shape[0]
    n_blocks = -(-(t * TOP_K) // EXPERT_BLOCK) + N_EXPERTS
    counts = cnt[0].astype(jnp.int32)
    padded = (counts + EXPERT_BLOCK - 1) // EXPERT_BLOCK * EXPERT_BLOCK
    pad_end = jnp.cumsum(padded)
    pad_start = pad_end - padded
    dest = pad_start[mi[:, :TOP_K]] + mi[:, TOP_K:2 * TOP_K]
    dest = dest.reshape(t // TILE, 1, TILE * TOP_K)
    blk_row = jnp.arange(n_blocks, dtype=jnp.int32) * EXPERT_BLOCK
    blk_e = jnp.minimum(jnp.searchsorted(pad_end, blk_row, side='right'), N_EXPERTS - 1).astype(jnp.int32)
    blk_valid = jnp.clip(counts[blk_e] - (blk_row - pad_start[blk_e]), 0, EXPERT_BLOCK).astype(jnp.int32)
    n_used = (pad_end[-1:] // EXPERT_BLOCK).astype(jnp.int32)
    xs = _dispatch(dest, hf, n_blocks * EXPERT_BLOCK)
    ys = _experts(blk_e, blk_valid, n_used, xs, w1, b1, w2, b2)
    return _combine(dest, mw, xn, mods, g, ys, nb, all_tokens)


def _proj1_kernel(x_ref, mod_ref, g_ref, w_ref, q_ref, k_ref, v_ref):
    h = (_rms(x_ref[...], g_ref[...]) * (1.0 + _mod(mod_ref, 1)) + _mod(mod_ref, 0)).astype(BF16)
    qkv = _dot(h, w_ref[...])
    q_ref[...] = (qkv[:, :D] * NA_SCALE).astype(BF16)
    k_ref[...] = qkv[:, D:2 * D].astype(BF16)
    v_ref[...] = qkv[:, 2 * D:].astype(BF16)


def _proj1(x1, mods, g, w, nb):
    n_tiles = nb * SMP_TILES
    tile_map = lambda i: (i, 0)
    return pl.pallas_call(
        _proj1_kernel,
        grid=(n_tiles,),
        in_specs=[pl.BlockSpec((TILE, D), tile_map),
                  pl.BlockSpec((1, 1, 6 * D), _mod_map_all(nb)),
                  pl.BlockSpec((1, D), lambda i: (0, 0)),
                  pl.BlockSpec((D, 3 * D), lambda i: (0, 0))],
        out_specs=[pl.BlockSpec((TILE, D), tile_map)] * 3,
        out_shape=[jax.ShapeDtypeStruct((n_tiles * TILE, D), BF16)] * 3,
        compiler_params=_params(("arbitrary",)),
        name="proj_odd",
    )(x1, mods, g, w)


def _na_row_start(r):
    return jnp.clip(r - WIN_ROWS // 2, 0, GRID_H - WIN_ROWS)


def _na_kernel(q_ref, k_ref, v_ref, bias_ref, o_ref):
    r = pl.program_id(1)
    k0 = pl.multiple_of(_na_row_start(r) * GRID_W, GRID_W)
    n_loc = WIN_ROWS * GRID_W
    lane = lax.broadcasted_iota(jnp.int32, (GRID_W, 128), 1)
    for j in range(NA_HEADS // 2):
        sl = slice(j * 128, (j + 1) * 128)
        q = q_ref[:, sl]
        zero = jnp.zeros_like(q)
        qq = jnp.concatenate([jnp.where(lane < NA_DH, q, zero),
                              jnp.where(lane >= NA_DH, q, zero)], axis=0)
        s_loc = _dot_nt(qq, k_ref[pl.ds(k0, n_loc), sl]) + bias_ref[0, j]
        s_ctx = _dot_nt(qq, k_ref[SEQ:, sl])
        m = jnp.maximum(jnp.max(s_loc, axis=-1, keepdims=True), jnp.max(s_ctx, axis=-1, keepdims=True))
        p_loc = jnp.exp(s_loc - m)
        p_ctx = jnp.exp(s_ctx - m)
        l = jnp.sum(p_loc, axis=-1, keepdims=True) + jnp.sum(p_ctx, axis=-1, keepdims=True)
        pv = (_dot(p_loc.astype(BF16), v_ref[pl.ds(k0, n_loc), sl])
              + _dot(p_ctx.astype(BF16), v_ref[SEQ:, sl])) / l
        o_ref[:, sl] = jnp.where(lane < NA_DH, pv[:GRID_W], pv[GRID_W:]).astype(BF16)


def _na_bias(rpb):
    mid = WIN_ROWS // 2
    pat_rows = list(range(mid)) + [mid] + list(range(GRID_H - mid + 1, GRID_H))
    r = np.array(pat_rows)
    rs = np.clip(r - mid, 0, GRID_H - WIN_ROWS)
    row_off = rs[:, None] + np.arange(WIN_ROWS)[None, :] - r[:, None] + WIN_ROWS - 1
    c = np.arange(GRID_W)
    q_start = np.clip(c - WIN_COLS // 2, 0, GRID_W - WIN_COLS)[:, None]
    kc = np.arange(GRID_W)[None, :]
    valid = (kc >= q_start) & (kc < q_start + WIN_COLS)
    col_off = np.clip(kc - c[:, None] + WIN_COLS - 1, 0, 2 * WIN_COLS - 2)
    b = rpb.astype(F32)[:, row_off[:, :, None, None], col_off[None, None, :, :]]
    b = jnp.where(valid[None, None, None], b, MASK_VALUE)
    b = b.transpose(1, 0, 3, 2, 4).reshape(len(pat_rows), NA_HEADS // 2, 2 * GRID_W, WIN_ROWS * GRID_W)
    return b


def _na_pattern(r):
    mid = WIN_ROWS // 2
    return jnp.where(r < mid, r, jnp.where(r <= GRID_H - mid, mid, r - (GRID_H - 2 * mid)))


def _na(q, k, v, bias, nb):
    blocks_per_smp = SMP_ROWS // GRID_W
    return pl.pallas_call(
        _na_kernel,
        grid=(nb, GRID_H),
        in_specs=[pl.BlockSpec((GRID_W, D), lambda b, r: (b * blocks_per_smp + r, 0)),
                  pl.BlockSpec((SMP_ROWS, D), lambda b, r: (b, 0)),
                  pl.BlockSpec((SMP_ROWS, D), lambda b, r: (b, 0)),
                  pl.BlockSpec((1,) + bias.shape[1:], lambda b, r: (_na_pattern(r), 0, 0, 0))],
        out_specs=pl.BlockSpec((GRID_W, D), lambda b, r: (b * GRID_H + r, 0)),
        out_shape=jax.ShapeDtypeStruct((nb * SEQ, D), BF16),
        compiler_params=_params(("arbitrary", "arbitrary")),
        name="na_attn",
    )(q, k, v, bias)


def kernel(x, c, ctx, c_ctx, ada_w, ada_b, mix_pre_g, mix_post_g, ffn_pre_g, ffn_post_g, even_w_in,
           diff_lambda, diff_subln_g, mla_q_norm_g, mla_w_qb, mla_kv_norm_g, mla_w_kvb, even_w_out,
           na_w_qkv, na_rpb, na_w_out, router_w, router_b, moe_w1, moe_b1, moe_w2, moe_b2):
    nb = x.shape[0]
    assert x.shape[1:] == (SEQ, D) and ctx.shape[1:] == (CTX, D)
    x2 = x.reshape(nb * SEQ, D)
    c2 = ctx.reshape(nb * CTX, D)
    row = lambda a: a.reshape(1, -1)

    mods = _ada(jnp.concatenate([c, c_ctx[None, :]], axis=0), ada_w, ada_b)
    mods0 = mods[0].reshape(nb + 1, 1, 6 * D)
    mods1 = mods[1].reshape(nb + 1, 1, 6 * D)

    lam_init = 0.8 - 0.6 * math.exp(-0.3 * 0)
    w_in = even_w_in[0]
    wa = jnp.pad(w_in, ((0, 0), (0, 2048 - w_in.shape[1]))).astype(BF16)
    wqb = mla_w_qb[0].reshape(MLA_Q_LORA, MLA_HEADS, MLA_NOPE + MLA_ROPE)
    wqb = jnp.pad(wqb, ((0, 0), (0, 0), (0, MLA_QK_PAD - MLA_NOPE - MLA_ROPE)))
    wqb = wqb.reshape(MLA_Q_LORA, MLA_HEADS * MLA_QK_PAD).astype(BF16)
    qd, kd, vd, qm, km, vm = _proj0(x2, c2, mods0, row(mix_pre_g[0]), wa, row(mla_q_norm_g[0]), wqb,
                                    row(mla_kv_norm_g[0]), mla_w_kvb[0].astype(BF16), nb)
    o0 = _attn0(qd, kd, vd, qm, km, vm, diff_lambda[0], row(diff_subln_g[0]), lam_init, nb)
    xn, hf, mi, mw, cnt = _oproj(o0, (x2, c2), mods0, even_w_out[0].astype(BF16), row(mix_post_g[0]),
                                 row(ffn_pre_g[0]), router_w[0], row(router_b[0]), nb, True)
    x1 = _moe(hf, mi, mw, cnt, xn, mods0, row(ffn_post_g[0]), moe_w1[0].astype(BF16), moe_b1[0],
              moe_w2[0].astype(BF16), moe_b2[0], nb, True)

    q, k, v = _proj1(x1, mods1, row(mix_pre_g[1]), na_w_qkv[0].astype(BF16), nb)
    o1 = _na(q, k, v, _na_bias(na_rpb[0]), nb)
    xn, hf, mi, mw, cnt = _oproj(o1, (x1,), mods1, na_w_out[0].astype(BF16), row(mix_post_g[1]),
                                 row(ffn_pre_g[1]), router_w[1], row(router_b[1]), nb, False)
    out = _moe(hf, mi, mw, cnt, xn, mods1, row(ffn_post_g[1]), moe_w1[1].astype(BF16), moe_b1[1],
               moe_w2[1].astype(BF16), moe_b2[1], nb, False)
    return out.reshape(nb, SEQ, D)
```

```python
import functools
import math

import numpy as np
import jax
import jax.numpy as jnp
from jax import lax
from jax.experimental import pallas as pl
from jax.experimental.pallas import tpu as pltpu

F32 = jnp.float32
BF16 = jnp.bfloat16

D = 1024
SEQ = 2048
CTX = 256
GRID_W = 64
GRID_H = SEQ // GRID_W
TILE = 256
LAT_TILES = SEQ // TILE
SMP_TILES = LAT_TILES + 1
SMP_ROWS = SEQ + CTX
EPS = 1e-6
ROPE_BASE = 10000.0

DIFF_HEADS = 4
DIFF_DH = 64
DIFF_SCALE = DIFF_DH ** -0.5
MLA_HEADS = 4
MLA_Q_LORA = 256
MLA_KV_LORA = 128
MLA_NOPE = 128
MLA_ROPE = 64
MLA_V = 128
MLA_SCALE = (MLA_NOPE + MLA_ROPE) ** -0.5
MLA_QK_PAD = 256
NA_HEADS = 16
NA_DH = 64
NA_SCALE = NA_DH ** -0.5
WIN_ROWS = 8
WIN_COLS = 16
N_EXPERTS = 32
TOP_K = 4
D_FF = 1024
SWIGLU_ALPHA = 1.702
SWIGLU_LIMIT = 7.0
EXPERT_BLOCK = 256
SEG_ALIGN = 8
SORT_ROWS = -(-(TILE * TOP_K + N_EXPERTS * (SEG_ALIGN - 1)) // 256) * 256
MASK_VALUE = -1e30
META_LANES = 128
VMEM_LIMIT = 60 * 1024 * 1024

_NT = (((1,), (1,)), ((), ()))


def _dot(a, b):
    return jnp.dot(a, b, preferred_element_type=F32)


def _dot_nt(a, b):
    return lax.dot_general(a, b, _NT, preferred_element_type=F32)


def _dot_f32(a, b):
    return jnp.dot(a, b, preferred_element_type=F32, precision=lax.Precision.HIGHEST)


def _rms(x, g):
    return x * lax.rsqrt(jnp.mean(x * x, axis=-1, keepdims=True) + EPS) * g


def _mod(mod_ref, k):
    return mod_ref[0, :, k * D:(k + 1) * D]


def _params(sem):
    return pltpu.CompilerParams(dimension_semantics=sem, vmem_limit_bytes=VMEM_LIMIT)


def _ada_kernel(c_ref, w_ref, b_ref, o_ref):
    c = c_ref[...]
    s = c * jax.nn.sigmoid(c)
    o_ref[0] = _dot_f32(s, w_ref[0]) + b_ref[0]


def _ada(cc, ada_w, ada_b):
    depth = ada_w.shape[0]
    n = cc.shape[0]
    nt = 6 * D // D
    return pl.pallas_call(
        _ada_kernel,
        grid=(depth, nt),
        in_specs=[pl.BlockSpec((n, D), lambda l, j: (0, 0)),
                  pl.BlockSpec((1, D, D), lambda l, j: (l, 0, j)),
                  pl.BlockSpec((1, 1, D), lambda l, j: (l, 0, j))],
        out_specs=pl.BlockSpec((1, n, D), lambda l, j: (l, 0, j)),
        out_shape=jax.ShapeDtypeStruct((depth, n, 6 * D), F32),
        compiler_params=_params(("arbitrary", "arbitrary")),
        name="ada",
    )(cc, ada_w, ada_b.reshape(depth, 1, 6 * D))


def _x_lat_map(i):
    return ((i // SMP_TILES) * LAT_TILES + jnp.minimum(i % SMP_TILES, LAT_TILES - 1), 0)


def _x_ctx_map(i):
    return (i // SMP_TILES, 0)


def _mod_map_all(nb):
    return lambda i: (jnp.where(i % SMP_TILES == LAT_TILES, nb, i // SMP_TILES), 0, 0)


def _lat_of_all(i):
    return (i // LAT_TILES) * SMP_TILES + i % LAT_TILES


def _rope(x, cos, sa, sb):
    return x * cos + pltpu.roll(x, 112, 1) * sa + pltpu.roll(x, 16, 1) * sb


def _proj0_kernel(x_ref, c_ref, mod_ref, g_ref, wa_ref, qng_ref, wqb_ref, kvng_ref, wkvb_ref,
                  cos_ref, sa_ref, sb_ref, qd_ref, kd_ref, vd_ref, qm_ref, km_ref, vm_ref):
    is_ctx = pl.program_id(0) % SMP_TILES == LAT_TILES
    x = jnp.where(is_ctx, c_ref[...], x_ref[...])
    h = (_rms(x, g_ref[...]) * (1.0 + _mod(mod_ref, 1)) + _mod(mod_ref, 0)).astype(BF16)
    big = _dot(h, wa_ref[...])
    cos, sa, sb = cos_ref[...], sa_ref[...], sb_ref[...]
    nd = DIFF_HEADS * 2 * DIFF_DH
    for j in range(nd // 128):
        sl = slice(j * 128, (j + 1) * 128)
        qd_ref[:, sl] = _rope(big[:, j * 128:(j + 1) * 128], cos, sa, sb).astype(BF16)
        kd_ref[:, sl] = _rope(big[:, nd + j * 128:nd + (j + 1) * 128], cos, sa, sb).astype(BF16)
    vd_ref[...] = big[:, 2 * nd:3 * nd].astype(BF16)
    o = 3 * nd
    cq = big[:, o:o + MLA_Q_LORA]
    ckv = big[:, o + MLA_Q_LORA:o + MLA_Q_LORA + MLA_KV_LORA]
    kpe = _rope(big[:, o + MLA_Q_LORA + MLA_KV_LORA:], cos, sa, sb).astype(BF16)
    qm = _dot(_rms(cq, qng_ref[...]).astype(BF16), wqb_ref[...])
    kv = _dot(_rms(ckv, kvng_ref[...]).astype(BF16), wkvb_ref[...])
    for hd in range(MLA_HEADS):
        b0 = hd * MLA_QK_PAD
        qm_ref[:, b0:b0 + 128] = qm[:, b0:b0 + 128].astype(BF16)
        qm_ref[:, b0 + 128:b0 + 256] = _rope(qm[:, b0 + 128:b0 + 256], cos, sa, sb).astype(BF16)
        km_ref[:, b0:b0 + 128] = kv[:, hd * 256:hd * 256 + 128].astype(BF16)
        km_ref[:, b0 + 128:b0 + 256] = kpe
        vm_ref[:, hd * 128:(hd + 1) * 128] = kv[:, hd * 256 + 128:(hd + 1) * 256].astype(BF16)


def _rope_tables():
    t = np.arange(SEQ)
    half = 16
    inv = (ROPE_BASE ** (-np.arange(half, dtype=np.float32) * 2.0 / (2 * half))).astype(np.float32)
    ang_r = (t // GRID_W).astype(np.float32)[:, None] * inv[None, :]
    ang_c = (t % GRID_W).astype(np.float32)[:, None] * inv[None, :]
    cr, sr, cc, sc = np.cos(ang_r), np.sin(ang_r), np.cos(ang_c), np.sin(ang_c)
    z = np.zeros_like(cr)
    cos = np.concatenate([cr, cr, cc, cc], axis=1)
    sa = np.concatenate([-sr, z, -sc, z], axis=1)
    sb = np.concatenate([z, sr, z, sc], axis=1)

    def full(tab, fill):
        tab = np.tile(tab, (1, 2))
        ident = np.full((CTX, 128), fill, np.float32)
        return jnp.asarray(np.concatenate([tab, ident], axis=0).astype(np.float32))

    return full(cos, 1.0), full(sa, 0.0), full(sb, 0.0)


def _proj0(x2, c2, mods, g, wa, qng, wqb, kvng, wkvb, nb):
    n_tiles = nb * SMP_TILES
    rows = nb * SMP_ROWS
    cos, sa, sb = _rope_tables()
    tile_map = lambda i: (i, 0)
    const = lambda i: (0, 0)
    tab_spec = pl.BlockSpec((TILE, 128), lambda i: (i % SMP_TILES, 0))
    widths = (512, 512, 512, 1024, 1024, 512)
    return pl.pallas_call(
        _proj0_kernel,
        grid=(n_tiles,),
        in_specs=[pl.BlockSpec((TILE, D), _x_lat_map),
                  pl.BlockSpec((TILE, D), _x_ctx_map),
                  pl.BlockSpec((1, 1, 6 * D), _mod_map_all(nb)),
                  pl.BlockSpec((1, D), const),
                  pl.BlockSpec(wa.shape, const),
                  pl.BlockSpec((1, MLA_Q_LORA), const),
                  pl.BlockSpec(wqb.shape, const),
                  pl.BlockSpec((1, MLA_KV_LORA), const),
                  pl.BlockSpec(wkvb.shape, const),
                  tab_spec, tab_spec, tab_spec],
        out_specs=[pl.BlockSpec((TILE, w), tile_map) for w in widths],
        out_shape=[jax.ShapeDtypeStruct((rows, w), BF16) for w in widths],
        compiler_params=_params(("arbitrary",)),
        name="proj_even",
    )(x2, c2, mods, g, wa, qng, wqb, kvng, wkvb, cos, sa, sb)


def _softmax_parts(s):
    m = jnp.max(s, axis=-1, keepdims=True)
    p = jnp.exp(s - m)
    return p, jnp.sum(p, axis=-1, keepdims=True)


def _attn0_kernel(lam_init, qd_ref, kd_ref, vd_ref, qm_ref, km_ref, vm_ref, lam_ref, sg_ref, o_ref):
    lv = lam_ref[...]
    lam = (jnp.exp(jnp.sum(lv[0:1] * lv[1:2], axis=1, keepdims=True))
           - jnp.exp(jnp.sum(lv[2:3] * lv[3:4], axis=1, keepdims=True)) + lam_init)
    lane = lax.broadcasted_iota(jnp.int32, (TILE, 128), 1)

    def heads(k_lo, nk):
        for hd in range(DIFF_HEADS):
            sl = slice(hd * 128, (hd + 1) * 128)
            q = qd_ref[:, sl]
            zero = jnp.zeros_like(q)
            qq = jnp.concatenate([jnp.where(lane < DIFF_DH, q, zero),
                                  jnp.where(lane >= DIFF_DH, q, zero)], axis=0)
            s = _dot_nt(qq, kd_ref[k_lo:k_lo + nk, sl]) * DIFF_SCALE
            p, l = _softmax_parts(s)
            pv = _dot(p.astype(BF16), vd_ref[k_lo:k_lo + nk, sl])
            o = pv[:TILE] / l[:TILE] - lam * (pv[TILE:] / l[TILE:])
            o_ref[:, sl] = (_rms(o, sg_ref[...]) * (1.0 - lam_init)).astype(BF16)
        for hd in range(MLA_HEADS):
            ql = slice(hd * MLA_QK_PAD, (hd + 1) * MLA_QK_PAD)
            s = _dot_nt(qm_ref[:, ql], km_ref[k_lo:k_lo + nk, ql]) * MLA_SCALE
            p, l = _softmax_parts(s)
            pv = _dot(p.astype(BF16), vm_ref[k_lo:k_lo + nk, hd * 128:(hd + 1) * 128])
            o_ref[:, 512 + hd * 128:512 + (hd + 1) * 128] = (pv / l).astype(BF16)

    is_ctx = pl.program_id(1) == LAT_TILES

    @pl.when(jnp.logical_not(is_ctx))
    def _():
        heads(0, SMP_ROWS)

    @pl.when(is_ctx)
    def _():
        heads(SEQ, CTX)


def _attn0(qd, kd, vd, qm, km, vm, lam_vec, subln_g, lam_init, nb):
    q_map = lambda b, p: (b * SMP_TILES + p, 0)
    k_map = lambda b, p: (b, 0)
    const = lambda b, p: (0, 0)
    return pl.pallas_call(
        functools.partial(_attn0_kernel, lam_init),
        grid=(nb, SMP_TILES),
        in_specs=[pl.BlockSpec((TILE, 512), q_map),
                  pl.BlockSpec((SMP_ROWS, 512), k_map),
                  pl.BlockSpec((SMP_ROWS, 512), k_map),
                  pl.BlockSpec((TILE, 1024), q_map),
                  pl.BlockSpec((SMP_ROWS, 1024), k_map),
                  pl.BlockSpec((SMP_ROWS, 512), k_map),
                  pl.BlockSpec(lam_vec.shape, const),
                  pl.BlockSpec((1, 128), const)],
        out_specs=pl.BlockSpec((TILE, D), q_map),
        out_shape=jax.ShapeDtypeStruct((nb * SMP_ROWS, D), BF16),
        compiler_params=_params(("arbitrary", "arbitrary")),
        name="attn_even",
    )(qd, kd, vd, qm, km, vm, lam_vec, subln_g)


def _oproj_kernel(dual, *refs):
    if dual:
        (o_ref, xl_ref, xc_ref, mod_ref, wo_ref, pg_ref, fg_ref, rw_ref, rb_ref,
         xn_ref, hf_ref, mi_ref, mw_ref, seg_ref) = refs
        is_ctx = pl.program_id(0) % SMP_TILES == LAT_TILES
        x = jnp.where(is_ctx, xc_ref[...], xl_ref[...])
    else:
        (o_ref, x_ref, mod_ref, wo_ref, pg_ref, fg_ref, rw_ref, rb_ref,
         xn_ref, hf_ref, mi_ref, mw_ref, seg_ref) = refs
        x = x_ref[...]

    y = _dot(o_ref[...], wo_ref[...])
    xn = x + _mod(mod_ref, 2) * _rms(y, pg_ref[...])
    xn_ref[...] = xn
    hf = _rms(xn, fg_ref[...]) * (1.0 + _mod(mod_ref, 4)) + _mod(mod_ref, 3)
    hf_ref[...] = hf.astype(BF16)

    logits = _dot_f32(hf, rw_ref[...]) + rb_ref[...]
    eid = lax.broadcasted_iota(jnp.int32, (TILE, N_EXPERTS), 1).astype(F32)
    sels, vals = [], []
    for _ in range(TOP_K):
        m = jnp.max(logits, axis=-1, keepdims=True)
        idx = jnp.min(jnp.where(logits == m, eid, float(N_EXPERTS)), axis=-1, keepdims=True)
        sel = eid == idx
        sels.append(sel)
        vals.append(m)
        logits = jnp.where(sel, -jnp.inf, logits)
    ex = [jnp.exp(v - vals[0]) for v in vals]
    den = ex[0] + ex[1] + ex[2] + ex[3]

    onehot = (sels[0] | sels[1] | sels[2] | sels[3]).astype(F32)
    r_i = lax.broadcasted_iota(jnp.int32, (TILE, TILE), 0)
    c_i = lax.broadcasted_iota(jnp.int32, (TILE, TILE), 1)
    before = _dot((c_i < r_i).astype(BF16), onehot.astype(BF16))
    cnt = jnp.sum(onehot, axis=0, keepdims=True)
    seg_len = jnp.floor((cnt + (SEG_ALIGN - 1)) * (1.0 / SEG_ALIGN)) * SEG_ALIGN
    e_r = lax.broadcasted_iota(jnp.int32, (N_EXPERTS, N_EXPERTS), 0)
    e_c = lax.broadcasted_iota(jnp.int32, (N_EXPERTS, N_EXPERTS), 1)
    seg_off = _dot_f32(jnp.broadcast_to(seg_len, (8, N_EXPERTS)), (e_r < e_c).astype(F32))[0:1]
    pos = before + seg_off
    lane = lax.broadcasted_iota(jnp.int32, (TILE, META_LANES), 1)
    mi = jnp.zeros((TILE, META_LANES), jnp.int32)
    mw = jnp.zeros((TILE, META_LANES), F32)
    for k in range(TOP_K):
        row = jnp.sum(jnp.where(sels[k], pos, 0.0), axis=-1, keepdims=True).astype(jnp.int32)
        mi = jnp.where(lane == k, row, mi)
        mw = jnp.where(lane == k, ex[k] / den, mw)
    mi_ref[...] = mi
    mw_ref[...] = mw
    seg_ref[0] = seg_len.astype(jnp.int32)


def _oproj(o, x_args, mods, wo, pg, fg, rw, rb, nb, all_tokens):
    const = lambda i: (0, 0)
    tile_map = lambda i: (i, 0)
    if all_tokens:
        n_tiles = nb * SMP_TILES
        x_specs = [pl.BlockSpec((TILE, D), _x_lat_map), pl.BlockSpec((TILE, D), _x_ctx_map)]
        mod_map = _mod_map_all(nb)
    else:
        n_tiles = nb * LAT_TILES
        x_specs = [pl.BlockSpec((TILE, D), lambda i: (_lat_of_all(i), 0))]
        mod_map = lambda i: (i // LAT_TILES, 0, 0)
    rows = n_tiles * TILE
    return pl.pallas_call(
        functools.partial(_oproj_kernel, all_tokens),
        grid=(n_tiles,),
        in_specs=[pl.BlockSpec((TILE, D), tile_map)] + x_specs + [
            pl.BlockSpec((1, 1, 6 * D), mod_map),
            pl.BlockSpec((D, D), const),
            pl.BlockSpec((1, D), const),
            pl.BlockSpec((1, D), const),
            pl.BlockSpec((D, N_EXPERTS), const),
            pl.BlockSpec((1, N_EXPERTS), const)],
        out_specs=[pl.BlockSpec((TILE, D), tile_map),
                   pl.BlockSpec((TILE, D), tile_map),
                   pl.BlockSpec((TILE, META_LANES), tile_map),
                   pl.BlockSpec((TILE, META_LANES), tile_map),
                   pl.BlockSpec((1, 1, N_EXPERTS), lambda i: (i, 0, 0))],
        out_shape=[jax.ShapeDtypeStruct((rows, D), F32),
                   jax.ShapeDtypeStruct((rows, D), BF16),
                   jax.ShapeDtypeStruct((rows, META_LANES), jnp.int32),
                   jax.ShapeDtypeStruct((rows, META_LANES), F32),
                   jax.ShapeDtypeStruct((n_tiles, 1, N_EXPERTS), jnp.int32)],
        compiler_params=_params(("arbitrary",)),
        name="oproj_router",
    )(o, *x_args, mods, wo, pg, fg, rw, rb)


def _for_pieces(n, largest, fn):
    piece = largest
    while piece >= SEG_ALIGN:
        done = (n // (2 * piece)) * (2 * piece)
        pl.when((n & piece) != 0)(functools.partial(fn, done, piece))
        piece //= 2


def _segment_copies(seg_ref, local_ref, hbm_ref, sem, to_hbm, action):
    def per_expert(e, carry):
        def piece_copy(done, rows):
            loc = local_ref.at[pl.ds(pl.multiple_of(seg_ref[0, 1, e] + done, SEG_ALIGN), rows)]
            glb = hbm_ref.at[pl.ds(pl.multiple_of(seg_ref[0, 2, e] + done, SEG_ALIGN), rows)]
            action(pltpu.make_async_copy(loc, glb, sem) if to_hbm else pltpu.make_async_copy(glb, loc, sem))

        _for_pieces(seg_ref[0, 0, e], TILE, piece_copy)
        return carry

    lax.fori_loop(0, N_EXPERTS, per_expert, 0)


def _zero_fill_copies(tail_ref, free_ref, zbuf, xs_ref, sem, action):
    def per_expert(e, carry):
        def piece_copy(done, rows):
            dst = xs_ref.at[pl.ds(pl.multiple_of(tail_ref[0, e] + done, SEG_ALIGN), rows)]
            action(pltpu.make_async_copy(zbuf.at[pl.ds(0, rows)], dst, sem))

        _for_pieces(tail_ref[1, e], EXPERT_BLOCK // 2, piece_copy)
        return carry

    def per_block(j, carry):
        first = pl.multiple_of(free_ref[0] + j * EXPERT_BLOCK, EXPERT_BLOCK)
        action(pltpu.make_async_copy(zbuf, xs_ref.at[pl.ds(first, EXPERT_BLOCK)], sem))
        return carry

    lax.fori_loop(0, N_EXPERTS, per_expert, 0)
    lax.fori_loop(0, free_ref[1], per_block, 0)


def _dispatch_kernel(seg_ref, tail_ref, free_ref, lp_ref, h_ref, xs_ref, sbuf, zbuf, sem):
    row = lax.broadcasted_iota(jnp.int32, (SORT_ROWS, TILE), 0)
    lp = lp_ref[0]
    hit = row == lp[0:1]
    for k in range(1, TOP_K):
        hit = hit | (row == lp[k:k + 1])
    sbuf[...] = _dot(jnp.where(hit, 1.0, 0.0).astype(BF16), h_ref[...])
    _segment_copies(seg_ref, sbuf, xs_ref, sem, True, lambda cp: cp.start())
    _segment_copies(seg_ref, sbuf, xs_ref, sem, True, lambda cp: cp.wait())

    @pl.when(pl.program_id(0) == pl.num_programs(0) - 1)
    def _():
        zbuf[...] = jnp.zeros_like(zbuf)
        _zero_fill_copies(tail_ref, free_ref, zbuf, xs_ref, sem, lambda cp: cp.start())
        _zero_fill_copies(tail_ref, free_ref, zbuf, xs_ref, sem, lambda cp: cp.wait())


def _dispatch(segs, tails, free, lpos_t, hf, m_rows):
    n_tiles = hf.shape[0] // TILE
    return pl.pallas_call(
        _dispatch_kernel,
        grid=(n_tiles,),
        in_specs=[pl.BlockSpec((1, 3, N_EXPERTS), lambda i: (i, 0, 0), memory_space=pltpu.SMEM),
                  pl.BlockSpec(memory_space=pltpu.SMEM),
                  pl.BlockSpec(memory_space=pltpu.SMEM),
                  pl.BlockSpec((1, TOP_K, TILE), lambda i: (i, 0, 0)),
                  pl.BlockSpec((TILE, D), lambda i: (i, 0))],
        out_specs=pl.BlockSpec(memory_space=pl.ANY),
        out_shape=jax.ShapeDtypeStruct((m_rows, D), F32),
        scratch_shapes=[pltpu.VMEM((SORT_ROWS, D), F32), pltpu.VMEM((EXPERT_BLOCK, D), F32),
                        pltpu.SemaphoreType.DMA],
        compiler_params=_params(("arbitrary",)),
        name="moe_dispatch",
    )(segs, tails, free, lpos_t, hf)


def _expert_kernel(be_ref, nu_ref, xs_ref, w1_ref, b1_ref, w2_ref, b2_ref, ys_ref):
    i = pl.program_id(0)

    @pl.when(i < nu_ref[0])
    def _():
        gu = _dot(xs_ref[...].astype(BF16), w1_ref[0]) + b1_ref[0]
        gate = jnp.minimum(gu[:, :D_FF], SWIGLU_LIMIT)
        lin = jnp.clip(gu[:, D_FF:], -SWIGLU_LIMIT, SWIGLU_LIMIT)
        act = gate * jax.nn.sigmoid(SWIGLU_ALPHA * gate) * (lin + 1.0)
        ys_ref[...] = _dot(act.astype(BF16), w2_ref[0]) + b2_ref[0]

    @pl.when(i >= nu_ref[0])
    def _():
        ys_ref[...] = jnp.zeros_like(ys_ref)


def _experts(blk_e, n_used, xs, w1, b1, w2, b2):
    n_blocks = xs.shape[0] // EXPERT_BLOCK
    in_row_map = lambda i, be, nu: (jnp.minimum(i, nu[0] - 1), 0)
    e_map = lambda i, be, nu: (be[i], 0, 0)
    grid_spec = pltpu.PrefetchScalarGridSpec(
        num_scalar_prefetch=2,
        grid=(n_blocks,),
        in_specs=[pl.BlockSpec((EXPERT_BLOCK, D), in_row_map),
                  pl.BlockSpec((1, D, 2 * D_FF), e_map),
                  pl.BlockSpec((1, 1, 2 * D_FF), e_map),
                  pl.BlockSpec((1, D_FF, D), e_map),
                  pl.BlockSpec((1, 1, D), e_map)],
        out_specs=pl.BlockSpec((EXPERT_BLOCK, D), lambda i, be, nu: (i, 0)))
    return pl.pallas_call(
        _expert_kernel,
        grid_spec=grid_spec,
        out_shape=jax.ShapeDtypeStruct(xs.shape, F32),
        compiler_params=_params(("arbitrary",)),
        name="moe_experts",
    )(blk_e, n_used, xs, w1, b1.reshape(N_EXPERTS, 1, 2 * D_FF), w2, b2.reshape(N_EXPERTS, 1, D))


def _combine_kernel(seg_ref, mi_ref, mw_ref, x_ref, mod_ref, g_ref, ys_ref, o_ref, ybuf, sem):
    @pl.when(pl.program_id(0) == 0)
    def _():
        ybuf[...] = jnp.zeros_like(ybuf)

    _segment_copies(seg_ref, ybuf, ys_ref, sem, False, lambda cp: cp.start())
    col = lax.broadcasted_iota(jnp.int32, (TILE, SORT_ROWS), 1)
    mi = mi_ref[...]
    mw = mw_ref[...]
    wm = jnp.zeros((TILE, SORT_ROWS), F32)
    for k in range(TOP_K):
        wm = jnp.where(col == mi[:, k:k + 1], mw[:, k:k + 1], wm)
    _segment_copies(seg_ref, ybuf, ys_ref, sem, False, lambda cp: cp.wait())
    y = _dot(wm.astype(BF16), ybuf[...].astype(BF16))
    o_ref[...] = x_ref[...] + _mod(mod_ref, 5) * _rms(y, g_ref[...])


def _combine(segs, mi, mw, xn, mods, g, ys, nb, all_tokens):
    n_tiles = xn.shape[0] // TILE
    tile_map = lambda i: (i, 0)
    mod_map = _mod_map_all(nb) if all_tokens else (lambda i: (i // LAT_TILES, 0, 0))
    return pl.pallas_call(
        _combine_kernel,
        grid=(n_tiles,),
        in_specs=[pl.BlockSpec((1, 3, N_EXPERTS), lambda i: (i, 0, 0), memory_space=pltpu.SMEM),
                  pl.BlockSpec((TILE, META_LANES), tile_map),
                  pl.BlockSpec((TILE, META_LANES), tile_map),
                  pl.BlockSpec((TILE, D), tile_map),
                  pl.BlockSpec((1, 1, 6 * D), mod_map),
                  pl.BlockSpec((1, D), lambda i: (0, 0)),
                  pl.BlockSpec(memory_space=pl.ANY)],
        out_specs=pl.BlockSpec((TILE, D), tile_map),
        out_shape=jax.ShapeDtypeStruct(xn.shape, F32),
        scratch_shapes=[pltpu.VMEM((SORT_ROWS, D), F32), pltpu.SemaphoreType.DMA],
        compiler_params=_params(("arbitrary",)),
        name="moe_combine",
    )(segs, mi, mw, xn, mods, g, ys)


def _moe(hf, mi, mw, seg, xn, mods, g, w1, b1, w2, b2, nb, all_tokens):
    t = hf.shape[0]
    n_tiles = t // TILE
    rows_max = t * TOP_K + n_tiles * N_EXPERTS * (SEG_ALIGN - 1)
    n_blocks = -(-rows_max // EXPERT_BLOCK) + N_EXPERTS
    seg_len = seg.reshape(n_tiles, N_EXPERTS)
    counts = jnp.sum(seg_len, axis=0)
    padded = (counts + EXPERT_BLOCK - 1) // EXPERT_BLOCK * EXPERT_BLOCK
    pad_end = jnp.cumsum(padded)
    pad_start = pad_end - padded
    seg_first = pad_start[None, :] + jnp.cumsum(seg_len, axis=0) - seg_len
    seg_local = jnp.cumsum(seg_len, axis=1) - seg_len
    segs = jnp.stack([seg_len, seg_local, seg_first], axis=1).astype(jnp.int32)
    lpos_t = mi[:, :TOP_K].reshape(n_tiles, TILE, TOP_K).transpose(0, 2, 1)
    blk_row = jnp.arange(n_blocks, dtype=jnp.int32) * EXPERT_BLOCK
    blk_e = jnp.minimum(jnp.sum(pad_end[None, :] <= blk_row[:, None], axis=1), N_EXPERTS - 1).astype(jnp.int32)
    n_used = (pad_end[-1:] // EXPERT_BLOCK).astype(jnp.int32)
    tails = jnp.stack([pad_start + counts, padded - counts]).astype(jnp.int32)
    free = jnp.concatenate([pad_end[-1:], n_blocks - n_used]).astype(jnp.int32)
    xs = _dispatch(segs, tails, free, lpos_t, hf, n_blocks * EXPERT_BLOCK)
    ys = _experts(blk_e, n_used, xs, w1, b1, w2, b2)
    return _combine(segs, mi, mw, xn, mods, g, ys, nb, all_tokens)


def _proj1_kernel(x_ref, mod_ref, g_ref, w_ref, q_ref, k_ref, v_ref):
    h = (_rms(x_ref[...], g_ref[...]) * (1.0 + _mod(mod_ref, 1)) + _mod(mod_ref, 0)).astype(BF16)
    qkv = _dot(h, w_ref[...])
    q_ref[...] = (qkv[:, :D] * NA_SCALE).astype(BF16)
    k_ref[...] = qkv[:, D:2 * D].astype(BF16)
    v_ref[...] = qkv[:, 2 * D:].astype(BF16)


def _proj1(x1, mods, g, w, nb):
    n_tiles = nb * SMP_TILES
    tile_map = lambda i: (i, 0)
    return pl.pallas_call(
        _proj1_kernel,
        grid=(n_tiles,),
        in_specs=[pl.BlockSpec((TILE, D), tile_map),
                  pl.BlockSpec((1, 1, 6 * D), _mod_map_all(nb)),
                  pl.BlockSpec((1, D), lambda i: (0, 0)),
                  pl.BlockSpec((D, 3 * D), lambda i: (0, 0))],
        out_specs=[pl.BlockSpec((TILE, D), tile_map)] * 3,
        out_shape=[jax.ShapeDtypeStruct((n_tiles * TILE, D), BF16)] * 3,
        compiler_params=_params(("arbitrary",)),
        name="proj_odd",
    )(x1, mods, g, w)


def _na_row_start(r):
    return jnp.clip(r - WIN_ROWS // 2, 0, GRID_H - WIN_ROWS)


def _na_kernel(q_ref, k_ref, v_ref, bias_ref, o_ref):
    r = pl.program_id(1)
    k0 = pl.multiple_of(_na_row_start(r) * GRID_W, GRID_W)
    n_loc = WIN_ROWS * GRID_W
    lane = lax.broadcasted_iota(jnp.int32, (GRID_W, 128), 1)
    for j in range(NA_HEADS // 2):
        sl = slice(j * 128, (j + 1) * 128)
        q = q_ref[:, sl]
        zero = jnp.zeros_like(q)
        qq = jnp.concatenate([jnp.where(lane < NA_DH, q, zero),
                              jnp.where(lane >= NA_DH, q, zero)], axis=0)
        s_loc = _dot_nt(qq, k_ref[pl.ds(k0, n_loc), sl]) + bias_ref[0, j]
        s_ctx = _dot_nt(qq, k_ref[SEQ:, sl])
        m = jnp.maximum(jnp.max(s_loc, axis=-1, keepdims=True), jnp.max(s_ctx, axis=-1, keepdims=True))
        p_loc = jnp.exp(s_loc - m)
        p_ctx = jnp.exp(s_ctx - m)
        l = jnp.sum(p_loc, axis=-1, keepdims=True) + jnp.sum(p_ctx, axis=-1, keepdims=True)
        pv = (_dot(p_loc.astype(BF16), v_ref[pl.ds(k0, n_loc), sl])
              + _dot(p_ctx.astype(BF16), v_ref[SEQ:, sl])) / l
        o_ref[:, sl] = jnp.where(lane < NA_DH, pv[:GRID_W], pv[GRID_W:]).astype(BF16)


def _na_bias(rpb):
    mid = WIN_ROWS // 2
    pat_rows = list(range(mid)) + [mid] + list(range(GRID_H - mid + 1, GRID_H))
    r = np.array(pat_rows)
    rs = np.clip(r - mid, 0, GRID_H - WIN_ROWS)
    row_off = rs[:, None] + np.arange(WIN_ROWS)[None, :] - r[:, None] + WIN_ROWS - 1
    c = np.arange(GRID_W)
    q_start = np.clip(c - WIN_COLS // 2, 0, GRID_W - WIN_COLS)[:, None]
    kc = np.arange(GRID_W)[None, :]
    valid = (kc >= q_start) & (kc < q_start + WIN_COLS)
    col_off = np.clip(kc - c[:, None] + WIN_COLS - 1, 0, 2 * WIN_COLS - 2)
    b = rpb.astype(F32)[:, row_off[:, :, None, None], col_off[None, None, :, :]]
    b = jnp.where(valid[None, None, None], b, MASK_VALUE)
    b = b.transpose(1, 0, 3, 2, 4).reshape(len(pat_rows), NA_HEADS // 2, 2 * GRID_W, WIN_ROWS * GRID_W)
    return b


def _na_pattern(r):
    mid = WIN_ROWS // 2
    return jnp.where(r < mid, r, jnp.where(r <= GRID_H - mid, mid, r - (GRID_H - 2 * mid)))


def _na(q, k, v, bias, nb):
    blocks_per_smp = SMP_ROWS // GRID_W
    return pl.pallas_call(
        _na_kernel,
        grid=(nb, GRID_H),
        in_specs=[pl.BlockSpec((GRID_W, D), lambda b, r: (b * blocks_per_smp + r, 0)),
                  pl.BlockSpec((SMP_ROWS, D), lambda b, r: (b, 0)),
                  pl.BlockSpec((SMP_ROWS, D), lambda b, r: (b, 0)),
                  pl.BlockSpec((1,) + bias.shape[1:], lambda b, r: (_na_pattern(r), 0, 0, 0))],
        out_specs=pl.BlockSpec((GRID_W, D), lambda b, r: (b * GRID_H + r, 0)),
        out_shape=jax.ShapeDtypeStruct((nb * SEQ, D), BF16),
        compiler_params=_params(("arbitrary", "arbitrary")),
        name="na_attn",
    )(q, k, v, bias)


def kernel(x, c, ctx, c_ctx, ada_w, ada_b, mix_pre_g, mix_post_g, ffn_pre_g, ffn_post_g, even_w_in,
           diff_lambda, diff_subln_g, mla_q_norm_g, mla_w_qb, mla_kv_norm_g, mla_w_kvb, even_w_out,
           na_w_qkv, na_rpb, na_w_out, router_w, router_b, moe_w1, moe_b1, moe_w2, moe_b2):
    nb = x.shape[0]
    assert x.shape[1:] == (SEQ, D) and ctx.shape[1:] == (CTX, D)
    x2 = x.reshape(nb * SEQ, D)
    c2 = ctx.reshape(nb * CTX, D)
    row = lambda a: a.reshape(1, -1)

    mods = _ada(jnp.concatenate([c, c_ctx[None, :]], axis=0), ada_w, ada_b)
    mods0 = mods[0].reshape(nb + 1, 1, 6 * D)
    mods1 = mods[1].reshape(nb + 1, 1, 6 * D)

    lam_init = 0.8 - 0.6 * math.exp(-0.3 * 0)
    w_in = even_w_in[0]
    wa = jnp.pad(w_in, ((0, 0), (0, 2048 - w_in.shape[1]))).astype(BF16)
    wqb = mla_w_qb[0].reshape(MLA_Q_LORA, MLA_HEADS, MLA_NOPE + MLA_ROPE)
    wqb = jnp.pad(wqb, ((0, 0), (0, 0), (0, MLA_QK_PAD - MLA_NOPE - MLA_ROPE)))
    wqb = wqb.reshape(MLA_Q_LORA, MLA_HEADS * MLA_QK_PAD).astype(BF16)
    qd, kd, vd, qm, km, vm = _proj0(x2, c2, mods0, row(mix_pre_g[0]), wa, row(mla_q_norm_g[0]), wqb,
                                    row(mla_kv_norm_g[0]), mla_w_kvb[0].astype(BF16), nb)
    o0 = _attn0(qd, kd, vd, qm, km, vm, diff_lambda[0], row(diff_subln_g[0]), lam_init, nb)
    xn, hf, mi, mw, seg = _oproj(o0, (x2, c2), mods0, even_w_out[0].astype(BF16), row(mix_post_g[0]),
                                 row(ffn_pre_g[0]), router_w[0], row(router_b[0]), nb, True)
    x1 = _moe(hf, mi, mw, seg, xn, mods0, row(ffn_post_g[0]), moe_w1[0].astype(BF16), moe_b1[0],
              moe_w2[0].astype(BF16), moe_b2[0], nb, True)

    q, k, v = _proj1(x1, mods1, row(mix_pre_g[1]), na_w_qkv[0].astype(BF16), nb)
    o1 = _na(q, k, v, _na_bias(na_rpb[0]), nb)
    xn, hf, mi, mw, seg = _oproj(o1, (x1,), mods1, na_w_out[0].astype(BF16), row(mix_post_g[1]),
                                 row(ffn_pre_g[1]), router_w[1], row(router_b[1]), nb, False)
    out = _moe(hf, mi, mw, seg, xn, mods1, row(ffn_post_g[1]), moe_w1[1].astype(BF16), moe_b1[1],
               moe_w2[1].astype(BF16), moe_b2[1], nb, False)
    return out.reshape(nb, SEQ, D)
```

```python
import functools
import math

import numpy as np
import jax
import jax.numpy as jnp
from jax import lax
from jax.experimental import pallas as pl
from jax.experimental.pallas import tpu as pltpu

F32 = jnp.float32
BF16 = jnp.bfloat16

D = 1024
SEQ = 2048
CTX = 256
GRID_W = 64
GRID_H = SEQ // GRID_W
TILE = 256
LAT_TILES = SEQ // TILE
SMP_TILES = LAT_TILES + 1
SMP_ROWS = SEQ + CTX
EPS = 1e-6
ROPE_BASE = 10000.0

DIFF_HEADS = 4
DIFF_DH = 64
DIFF_SCALE = DIFF_DH ** -0.5
MLA_HEADS = 4
MLA_Q_LORA = 256
MLA_KV_LORA = 128
MLA_NOPE = 128
MLA_ROPE = 64
MLA_V = 128
MLA_SCALE = (MLA_NOPE + MLA_ROPE) ** -0.5
MLA_QK_PAD = 256
NA_HEADS = 16
NA_DH = 64
NA_SCALE = NA_DH ** -0.5
WIN_ROWS = 8
WIN_COLS = 16
N_EXPERTS = 32
TOP_K = 4
D_FF = 1024
SWIGLU_ALPHA = 1.702
SWIGLU_LIMIT = 7.0
EXPERT_BLOCK = 512
SEG_ALIGN = 8
SORT_ROWS = -(-(TILE * TOP_K + N_EXPERTS * (SEG_ALIGN - 1)) // 256) * 256
MASK_VALUE = -1e30
LOG2_E = math.log2(math.e)
META_LANES = 128
VMEM_LIMIT = 60 * 1024 * 1024

_NT = (((1,), (1,)), ((), ()))


def _dot(a, b):
    return jnp.dot(a, b, preferred_element_type=F32)


def _dot_nt(a, b):
    return lax.dot_general(a, b, _NT, preferred_element_type=F32)


def _dot_f32(a, b):
    return jnp.dot(a, b, preferred_element_type=F32, precision=lax.Precision.HIGHEST)


def _rms(x, g):
    return x * lax.rsqrt(jnp.mean(x * x, axis=-1, keepdims=True) + EPS) * g


def _mod(mod_ref, k):
    return mod_ref[0, :, k * D:(k + 1) * D]


def _params(sem):
    return pltpu.CompilerParams(dimension_semantics=sem, vmem_limit_bytes=VMEM_LIMIT)


def _ada_kernel(c_ref, w_ref, b_ref, o_ref):
    c = c_ref[...]
    s = c * jax.nn.sigmoid(c)
    o_ref[0] = _dot_f32(s, w_ref[0]) + b_ref[0]


def _ada(cc, ada_w, ada_b):
    depth = ada_w.shape[0]
    n = cc.shape[0]
    nt = 6 * D // D
    return pl.pallas_call(
        _ada_kernel,
        grid=(depth, nt),
        in_specs=[pl.BlockSpec((n, D), lambda l, j: (0, 0)),
                  pl.BlockSpec((1, D, D), lambda l, j: (l, 0, j)),
                  pl.BlockSpec((1, 1, D), lambda l, j: (l, 0, j))],
        out_specs=pl.BlockSpec((1, n, D), lambda l, j: (l, 0, j)),
        out_shape=jax.ShapeDtypeStruct((depth, n, 6 * D), F32),
        compiler_params=_params(("arbitrary", "arbitrary")),
        name="ada",
    )(cc, ada_w, ada_b.reshape(depth, 1, 6 * D))


def _x_lat_map(i):
    return ((i // SMP_TILES) * LAT_TILES + jnp.minimum(i % SMP_TILES, LAT_TILES - 1), 0)


def _x_ctx_map(i):
    return (i // SMP_TILES, 0)


def _mod_map_all(nb):
    return lambda i: (jnp.where(i % SMP_TILES == LAT_TILES, nb, i // SMP_TILES), 0, 0)


def _lat_of_all(i):
    return (i // LAT_TILES) * SMP_TILES + i % LAT_TILES


def _rope(x, cos, sa, sb):
    return x * cos + pltpu.roll(x, 112, 1) * sa + pltpu.roll(x, 16, 1) * sb


def _proj0_kernel(x_ref, c_ref, mod_ref, g_ref, wa_ref, qng_ref, wqb_ref, kvng_ref, wkvb_ref,
                  cos_ref, sa_ref, sb_ref, qd_ref, kd_ref, vd_ref, qm_ref, km_ref, vm_ref):
    is_ctx = pl.program_id(0) % SMP_TILES == LAT_TILES
    x = jnp.where(is_ctx, c_ref[...], x_ref[...])
    h = (_rms(x, g_ref[...]) * (1.0 + _mod(mod_ref, 1)) + _mod(mod_ref, 0)).astype(BF16)
    big = _dot(h, wa_ref[...])
    cos, sa, sb = cos_ref[...], sa_ref[...], sb_ref[...]
    nd = DIFF_HEADS * 2 * DIFF_DH
    for j in range(nd // 128):
        sl = slice(j * 128, (j + 1) * 128)
        qd_ref[:, sl] = _rope(big[:, j * 128:(j + 1) * 128], cos, sa, sb).astype(BF16)
        kd_ref[:, sl] = _rope(big[:, nd + j * 128:nd + (j + 1) * 128], cos, sa, sb).astype(BF16)
    vd_ref[...] = big[:, 2 * nd:3 * nd].astype(BF16)
    o = 3 * nd
    cq = big[:, o:o + MLA_Q_LORA]
    ckv = big[:, o + MLA_Q_LORA:o + MLA_Q_LORA + MLA_KV_LORA]
    kpe = _rope(big[:, o + MLA_Q_LORA + MLA_KV_LORA:], cos, sa, sb).astype(BF16)
    qm = _dot(_rms(cq, qng_ref[...]).astype(BF16), wqb_ref[...])
    kv = _dot(_rms(ckv, kvng_ref[...]).astype(BF16), wkvb_ref[...])
    for hd in range(MLA_HEADS):
        b0 = hd * MLA_QK_PAD
        qm_ref[:, b0:b0 + 128] = qm[:, b0:b0 + 128].astype(BF16)
        qm_ref[:, b0 + 128:b0 + 256] = _rope(qm[:, b0 + 128:b0 + 256], cos, sa, sb).astype(BF16)
        km_ref[:, b0:b0 + 128] = kv[:, hd * 256:hd * 256 + 128].astype(BF16)
        km_ref[:, b0 + 128:b0 + 256] = kpe
        vm_ref[:, hd * 128:(hd + 1) * 128] = kv[:, hd * 256 + 128:(hd + 1) * 256].astype(BF16)


def _rope_tables():
    t = np.arange(SEQ)
    half = 16
    inv = (ROPE_BASE ** (-np.arange(half, dtype=np.float32) * 2.0 / (2 * half))).astype(np.float32)
    ang_r = (t // GRID_W).astype(np.float32)[:, None] * inv[None, :]
    ang_c = (t % GRID_W).astype(np.float32)[:, None] * inv[None, :]
    cr, sr, cc, sc = np.cos(ang_r), np.sin(ang_r), np.cos(ang_c), np.sin(ang_c)
    z = np.zeros_like(cr)
    cos = np.concatenate([cr, cr, cc, cc], axis=1)
    sa = np.concatenate([-sr, z, -sc, z], axis=1)
    sb = np.concatenate([z, sr, z, sc], axis=1)

    def full(tab, fill):
        tab = np.tile(tab, (1, 2))
        ident = np.full((CTX, 128), fill, np.float32)
        return jnp.asarray(np.concatenate([tab, ident], axis=0).astype(np.float32))

    return full(cos, 1.0), full(sa, 0.0), full(sb, 0.0)


def _proj0(x2, c2, mods, g, wa, qng, wqb, kvng, wkvb, nb):
    n_tiles = nb * SMP_TILES
    rows = nb * SMP_ROWS
    cos, sa, sb = _rope_tables()
    tile_map = lambda i: (i, 0)
    const = lambda i: (0, 0)
    tab_spec = pl.BlockSpec((TILE, 128), lambda i: (i % SMP_TILES, 0))
    widths = (512, 512, 512, 1024, 1024, 512)
    return pl.pallas_call(
        _proj0_kernel,
        grid=(n_tiles,),
        in_specs=[pl.BlockSpec((TILE, D), _x_lat_map),
                  pl.BlockSpec((TILE, D), _x_ctx_map),
                  pl.BlockSpec((1, 1, 6 * D), _mod_map_all(nb)),
                  pl.BlockSpec((1, D), const),
                  pl.BlockSpec(wa.shape, const),
                  pl.BlockSpec((1, MLA_Q_LORA), const),
                  pl.BlockSpec(wqb.shape, const),
                  pl.BlockSpec((1, MLA_KV_LORA), const),
                  pl.BlockSpec(wkvb.shape, const),
                  tab_spec, tab_spec, tab_spec],
        out_specs=[pl.BlockSpec((TILE, w), tile_map) for w in widths],
        out_shape=[jax.ShapeDtypeStruct((rows, w), BF16) for w in widths],
        compiler_params=_params(("arbitrary",)),
        name="proj_even",
    )(x2, c2, mods, g, wa, qng, wqb, kvng, wkvb, cos, sa, sb)


def _softmax_parts(s, scale):
    c = scale * LOG2_E
    m = jnp.max(s, axis=-1, keepdims=True)
    p = jnp.exp2(s * c - m * c)
    return p, jnp.sum(p, axis=-1, keepdims=True)


def _attn0_kernel(lam_init, qd_ref, kd_ref, vd_ref, qm_ref, km_ref, vm_ref, lam_ref, sg_ref, o_ref):
    lv = lam_ref[...]
    lam = (jnp.exp(jnp.sum(lv[0:1] * lv[1:2], axis=1, keepdims=True))
           - jnp.exp(jnp.sum(lv[2:3] * lv[3:4], axis=1, keepdims=True)) + lam_init)
    lane = lax.broadcasted_iota(jnp.int32, (TILE, 128), 1)

    def heads(k_lo, nk):
        def diff_scores(hd):
            sl = slice(hd * 128, (hd + 1) * 128)
            q = qd_ref[:, sl]
            zero = jnp.zeros_like(q)
            qq = jnp.concatenate([jnp.where(lane < DIFF_DH, q, zero),
                                  jnp.where(lane >= DIFF_DH, q, zero)], axis=0)
            return _dot_nt(qq, kd_ref[k_lo:k_lo + nk, sl])

        def diff_finish(hd, s):
            sl = slice(hd * 128, (hd + 1) * 128)
            p, l = _softmax_parts(s, DIFF_SCALE)
            pv = _dot(p.astype(BF16), vd_ref[k_lo:k_lo + nk, sl])
            o = pv[:TILE] / l[:TILE] - lam * (pv[TILE:] / l[TILE:])
            o_ref[:, sl] = (_rms(o, sg_ref[...]) * (1.0 - lam_init)).astype(BF16)

        def mla_scores(hd):
            ql = slice(hd * MLA_QK_PAD, (hd + 1) * MLA_QK_PAD)
            return _dot_nt(qm_ref[:, ql], km_ref[k_lo:k_lo + nk, ql])

        def mla_finish(hd, s):
            p, l = _softmax_parts(s, MLA_SCALE)
            pv = _dot(p.astype(BF16), vm_ref[k_lo:k_lo + nk, hd * 128:(hd + 1) * 128])
            o_ref[:, 512 + hd * 128:512 + (hd + 1) * 128] = (pv / l).astype(BF16)

        jobs = ([(diff_scores, diff_finish, hd) for hd in range(DIFF_HEADS)]
                + [(mla_scores, mla_finish, hd) for hd in range(MLA_HEADS)])
        s_next = jobs[0][0](jobs[0][2])
        for n, (_, finish, hd) in enumerate(jobs):
            s = s_next
            if n + 1 < len(jobs):
                s_next = jobs[n + 1][0](jobs[n + 1][2])
            finish(hd, s)

    is_ctx = pl.program_id(1) == LAT_TILES

    @pl.when(jnp.logical_not(is_ctx))
    def _():
        heads(0, SMP_ROWS)

    @pl.when(is_ctx)
    def _():
        heads(SEQ, CTX)


def _attn0(qd, kd, vd, qm, km, vm, lam_vec, subln_g, lam_init, nb):
    q_map = lambda b, p: (b * SMP_TILES + p, 0)
    k_map = lambda b, p: (b, 0)
    const = lambda b, p: (0, 0)
    return pl.pallas_call(
        functools.partial(_attn0_kernel, lam_init),
        grid=(nb, SMP_TILES),
        in_specs=[pl.BlockSpec((TILE, 512), q_map),
                  pl.BlockSpec((SMP_ROWS, 512), k_map),
                  pl.BlockSpec((SMP_ROWS, 512), k_map),
                  pl.BlockSpec((TILE, 1024), q_map),
                  pl.BlockSpec((SMP_ROWS, 1024), k_map),
                  pl.BlockSpec((SMP_ROWS, 512), k_map),
                  pl.BlockSpec(lam_vec.shape, const),
                  pl.BlockSpec((1, 128), const)],
        out_specs=pl.BlockSpec((TILE, D), q_map),
        out_shape=jax.ShapeDtypeStruct((nb * SMP_ROWS, D), BF16),
        compiler_params=_params(("arbitrary", "arbitrary")),
        name="attn_even",
    )(qd, kd, vd, qm, km, vm, lam_vec, subln_g)


def _oproj_kernel(dual, *refs):
    if dual:
        (o_ref, xl_ref, xc_ref, mod_ref, wo_ref, pg_ref, fg_ref, rw_ref, rb_ref,
         xn_ref, hf_ref, mi_ref, mw_ref, seg_ref) = refs
        is_ctx = pl.program_id(0) % SMP_TILES == LAT_TILES
        x = jnp.where(is_ctx, xc_ref[...], xl_ref[...])
    else:
        (o_ref, x_ref, mod_ref, wo_ref, pg_ref, fg_ref, rw_ref, rb_ref,
         xn_ref, hf_ref, mi_ref, mw_ref, seg_ref) = refs
        x = x_ref[...]

    y = _dot(o_ref[...], wo_ref[...])
    xn = x + _mod(mod_ref, 2) * _rms(y, pg_ref[...])
    xn_ref[...] = xn
    hf = _rms(xn, fg_ref[...]) * (1.0 + _mod(mod_ref, 4)) + _mod(mod_ref, 3)
    hi = hf.astype(BF16)
    hf_ref[...] = hi

    lo = (hf - hi.astype(F32)).astype(BF16)
    lt = _dot_nt(rw_ref[...], jnp.concatenate([hi, lo], axis=0))
    ne = N_EXPERTS
    logits = (lt[:ne, :TILE] + lt[ne:, :TILE]) + (lt[:ne, TILE:] + lt[ne:, TILE:]) + rb_ref[...]

    eid = lax.broadcasted_iota(jnp.int32, (ne, TILE), 0).astype(F32)
    sels, vals = [], []
    for _ in range(TOP_K):
        m = jnp.max(logits, axis=0, keepdims=True)
        idx = jnp.min(jnp.where(logits == m, eid, float(ne)), axis=0, keepdims=True)
        sel = eid == idx
        sels.append(sel)
        vals.append(m)
        logits = jnp.where(sel, -jnp.inf, logits)
    ex = [jnp.exp(v - vals[0]) for v in vals]
    den = ex[0] + ex[1] + ex[2] + ex[3]

    onehot = (sels[0] | sels[1] | sels[2] | sels[3]).astype(F32)
    r_i = lax.broadcasted_iota(jnp.int32, (TILE, TILE), 0)
    c_i = lax.broadcasted_iota(jnp.int32, (TILE, TILE), 1)
    before = _dot(onehot.astype(BF16), (r_i < c_i).astype(BF16))
    cnt = jnp.sum(onehot, axis=1, keepdims=True)
    seg_len = jnp.floor((cnt + (SEG_ALIGN - 1)) * (1.0 / SEG_ALIGN)) * SEG_ALIGN
    e_r = lax.broadcasted_iota(jnp.int32, (ne, ne), 0)
    e_c = lax.broadcasted_iota(jnp.int32, (ne, ne), 1)
    seg_len_b = jnp.broadcast_to(seg_len, (ne, META_LANES))
    seg_off = _dot_f32((e_c < e_r).astype(F32), seg_len_b)[:, 0:1]
    pos = before + seg_off
    sub = lax.broadcasted_iota(jnp.int32, (8, TILE), 0)
    mi = jnp.zeros((8, TILE), jnp.int32)
    mw = jnp.zeros((8, TILE), F32)
    for k in range(TOP_K):
        row = jnp.sum(jnp.where(sels[k], pos, 0.0), axis=0, keepdims=True).astype(jnp.int32)
        mi = jnp.where(sub == k, row, mi)
        mw = jnp.where(sub == k, ex[k] / den, mw)
    mi_ref[0] = mi
    mw_ref[0] = mw
    seg_ref[0] = seg_len_b.astype(jnp.int32)


def _oproj(o, x_args, mods, wo, pg, fg, rw, rb, nb, all_tokens):
    const = lambda i: (0, 0)
    tile_map = lambda i: (i, 0)
    tile3_map = lambda i: (i, 0, 0)
    rw_hi = rw.astype(BF16)
    rw_lo = (rw - rw_hi.astype(F32)).astype(BF16)
    rw_split = jnp.concatenate([rw_hi.T, rw_lo.T], axis=0)
    if all_tokens:
        n_tiles = nb * SMP_TILES
        x_specs = [pl.BlockSpec((TILE, D), _x_lat_map), pl.BlockSpec((TILE, D), _x_ctx_map)]
        mod_map = _mod_map_all(nb)
    else:
        n_tiles = nb * LAT_TILES
        x_specs = [pl.BlockSpec((TILE, D), lambda i: (_lat_of_all(i), 0))]
        mod_map = lambda i: (i // LAT_TILES, 0, 0)
    rows = n_tiles * TILE
    return pl.pallas_call(
        functools.partial(_oproj_kernel, all_tokens),
        grid=(n_tiles,),
        in_specs=[pl.BlockSpec((TILE, D), tile_map)] + x_specs + [
            pl.BlockSpec((1, 1, 6 * D), mod_map),
            pl.BlockSpec((D, D), const),
            pl.BlockSpec((1, D), const),
            pl.BlockSpec((1, D), const),
            pl.BlockSpec((2 * N_EXPERTS, D), const),
            pl.BlockSpec((N_EXPERTS, 1), const)],
        out_specs=[pl.BlockSpec((TILE, D), tile_map),
                   pl.BlockSpec((TILE, D), tile_map),
                   pl.BlockSpec((1, 8, TILE), tile3_map),
                   pl.BlockSpec((1, 8, TILE), tile3_map),
                   pl.BlockSpec((1, N_EXPERTS, META_LANES), tile3_map)],
        out_shape=[jax.ShapeDtypeStruct((rows, D), F32),
                   jax.ShapeDtypeStruct((rows, D), BF16),
                   jax.ShapeDtypeStruct((n_tiles, 8, TILE), jnp.int32),
                   jax.ShapeDtypeStruct((n_tiles, 8, TILE), F32),
                   jax.ShapeDtypeStruct((n_tiles, N_EXPERTS, META_LANES), jnp.int32)],
        compiler_params=_params(("arbitrary",)),
        name="oproj_router",
    )(o, *x_args, mods, wo, pg, fg, rw_split, rb.reshape(N_EXPERTS, 1))


def _for_pieces(n, largest, fn):
    piece = largest
    while piece >= SEG_ALIGN:
        done = (n // (2 * piece)) * (2 * piece)
        pl.when((n & piece) != 0)(functools.partial(fn, done, piece))
        piece //= 2


def _segment_copies(seg_ref, local_ref, hbm_ref, sem, to_hbm, action):
    def per_expert(e, carry):
        def piece_copy(done, rows):
            loc = local_ref.at[pl.ds(pl.multiple_of(seg_ref[0, 1, e] + done, SEG_ALIGN), rows)]
            glb = hbm_ref.at[pl.ds(pl.multiple_of(seg_ref[0, 2, e] + done, SEG_ALIGN), rows)]
            action(pltpu.make_async_copy(loc, glb, sem) if to_hbm else pltpu.make_async_copy(glb, loc, sem))

        _for_pieces(seg_ref[0, 0, e], TILE, piece_copy)
        return carry

    lax.fori_loop(0, N_EXPERTS, per_expert, 0)


def _zero_fill_copies(tail_ref, free_ref, zbuf, xs_ref, sem, action):
    def per_expert(e, carry):
        def piece_copy(done, rows):
            dst = xs_ref.at[pl.ds(pl.multiple_of(tail_ref[0, e] + done, SEG_ALIGN), rows)]
            action(pltpu.make_async_copy(zbuf.at[pl.ds(0, rows)], dst, sem))

        _for_pieces(tail_ref[1, e], EXPERT_BLOCK // 2, piece_copy)
        return carry

    def per_block(j, carry):
        first = pl.multiple_of(free_ref[0] + j * EXPERT_BLOCK, EXPERT_BLOCK)
        action(pltpu.make_async_copy(zbuf, xs_ref.at[pl.ds(first, EXPERT_BLOCK)], sem))
        return carry

    lax.fori_loop(0, N_EXPERTS, per_expert, 0)
    lax.fori_loop(0, free_ref[1], per_block, 0)


def _dispatch_kernel(seg_ref, tail_ref, free_ref, lp_ref, h_ref, xs_ref, sbuf, zbuf, sem):
    row = lax.broadcasted_iota(jnp.int32, (SORT_ROWS, TILE), 0)
    lp = lp_ref[0]
    hit = row == lp[0:1]
    for k in range(1, TOP_K):
        hit = hit | (row == lp[k:k + 1])
    sbuf[...] = _dot(jnp.where(hit, 1.0, 0.0).astype(BF16), h_ref[...])
    _segment_copies(seg_ref, sbuf, xs_ref, sem, True, lambda cp: cp.start())
    _segment_copies(seg_ref, sbuf, xs_ref, sem, True, lambda cp: cp.wait())

    @pl.when(pl.program_id(0) == pl.num_programs(0) - 1)
    def _():
        zbuf[...] = jnp.zeros_like(zbuf)
        _zero_fill_copies(tail_ref, free_ref, zbuf, xs_ref, sem, lambda cp: cp.start())
        _zero_fill_copies(tail_ref, free_ref, zbuf, xs_ref, sem, lambda cp: cp.wait())


def _dispatch(segs, tails, free, lpos_t, hf, m_rows):
    n_tiles = hf.shape[0] // TILE
    return pl.pallas_call(
        _dispatch_kernel,
        grid=(n_tiles,),
        in_specs=[pl.BlockSpec((1, 3, N_EXPERTS), lambda i: (i, 0, 0), memory_space=pltpu.SMEM),
                  pl.BlockSpec(memory_space=pltpu.SMEM),
                  pl.BlockSpec(memory_space=pltpu.SMEM),
                  pl.BlockSpec((1, 8, TILE), lambda i: (i, 0, 0)),
                  pl.BlockSpec((TILE, D), lambda i: (i, 0))],
        out_specs=pl.BlockSpec(memory_space=pl.ANY),
        out_shape=jax.ShapeDtypeStruct((m_rows, D), F32),
        scratch_shapes=[pltpu.VMEM((SORT_ROWS, D), F32), pltpu.VMEM((EXPERT_BLOCK, D), F32),
                        pltpu.SemaphoreType.DMA],
        compiler_params=_params(("arbitrary",)),
        name="moe_dispatch",
    )(segs, tails, free, lpos_t, hf)


def _expert_kernel(be_ref, nu_ref, xs_ref, w1_ref, b1_ref, w2_ref, b2_ref, ys_ref):
    i = pl.program_id(0)

    @pl.when(i < nu_ref[0])
    def _():
        gu = _dot(xs_ref[...].astype(BF16), w1_ref[0]) + b1_ref[0]
        gate = jnp.minimum(gu[:, :D_FF], SWIGLU_LIMIT)
        lin = jnp.clip(gu[:, D_FF:], -SWIGLU_LIMIT, SWIGLU_LIMIT)
        act = gate * jax.nn.sigmoid(SWIGLU_ALPHA * gate) * (lin + 1.0)
        ys_ref[...] = _dot(act.astype(BF16), w2_ref[0]) + b2_ref[0]

    @pl.when(i >= nu_ref[0])
    def _():
        ys_ref[...] = jnp.zeros_like(ys_ref)


def _experts(blk_e, n_used, xs, w1, b1, w2, b2):
    n_blocks = xs.shape[0] // EXPERT_BLOCK
    in_row_map = lambda i, be, nu: (jnp.minimum(i, nu[0] - 1), 0)
    e_map = lambda i, be, nu: (be[i], 0, 0)
    grid_spec = pltpu.PrefetchScalarGridSpec(
        num_scalar_prefetch=2,
        grid=(n_blocks,),
        in_specs=[pl.BlockSpec((EXPERT_BLOCK, D), in_row_map),
                  pl.BlockSpec((1, D, 2 * D_FF), e_map),
                  pl.BlockSpec((1, 1, 2 * D_FF), e_map),
                  pl.BlockSpec((1, D_FF, D), e_map),
                  pl.BlockSpec((1, 1, D), e_map)],
        out_specs=pl.BlockSpec((EXPERT_BLOCK, D), lambda i, be, nu: (i, 0)))
    return pl.pallas_call(
        _expert_kernel,
        grid_spec=grid_spec,
        out_shape=jax.ShapeDtypeStruct(xs.shape, F32),
        compiler_params=_params(("arbitrary",)),
        name="moe_experts",
    )(blk_e, n_used, xs, w1, b1.reshape(N_EXPERTS, 1, 2 * D_FF), w2, b2.reshape(N_EXPERTS, 1, D))


def _combine_kernel(seg_ref, mi_ref, mw_ref, x_ref, mod_ref, g_ref, ys_ref, o_ref, ybuf, sem):
    @pl.when(pl.program_id(0) == 0)
    def _():
        ybuf[...] = jnp.zeros_like(ybuf)

    _segment_copies(seg_ref, ybuf, ys_ref, sem, False, lambda cp: cp.start())
    col = lax.broadcasted_iota(jnp.int32, (TILE, SORT_ROWS), 1)
    mi = mi_ref[...]
    mw = mw_ref[...]
    wm = jnp.zeros((TILE, SORT_ROWS), F32)
    for k in range(TOP_K):
        wm = jnp.where(col == mi[:, k:k + 1], mw[:, k:k + 1], wm)
    _segment_copies(seg_ref, ybuf, ys_ref, sem, False, lambda cp: cp.wait())
    y = _dot(wm.astype(BF16), ybuf[...].astype(BF16))
    o_ref[...] = x_ref[...] + _mod(mod_ref, 5) * _rms(y, g_ref[...])


def _combine(segs, mi, mw, xn, mods, g, ys, nb, all_tokens):
    n_tiles = xn.shape[0] // TILE
    tile_map = lambda i: (i, 0)
    mod_map = _mod_map_all(nb) if all_tokens else (lambda i: (i // LAT_TILES, 0, 0))
    return pl.pallas_call(
        _combine_kernel,
        grid=(n_tiles,),
        in_specs=[pl.BlockSpec((1, 3, N_EXPERTS), lambda i: (i, 0, 0), memory_space=pltpu.SMEM),
                  pl.BlockSpec((TILE, 8), tile_map),
                  pl.BlockSpec((TILE, 8), tile_map),
                  pl.BlockSpec((TILE, D), tile_map),
                  pl.BlockSpec((1, 1, 6 * D), mod_map),
                  pl.BlockSpec((1, D), lambda i: (0, 0)),
                  pl.BlockSpec(memory_space=pl.ANY)],
        out_specs=pl.BlockSpec((TILE, D), tile_map),
        out_shape=jax.ShapeDtypeStruct(xn.shape, F32),
        scratch_shapes=[pltpu.VMEM((SORT_ROWS, D), F32), pltpu.SemaphoreType.DMA],
        compiler_params=_params(("arbitrary",)),
        name="moe_combine",
    )(segs, mi, mw, xn, mods, g, ys)


def _moe(hf, mi, mw, seg, xn, mods, g, w1, b1, w2, b2, nb, all_tokens):
    t = hf.shape[0]
    n_tiles = t // TILE
    rows_max = t * TOP_K + n_tiles * N_EXPERTS * (SEG_ALIGN - 1)
    n_blocks = -(-rows_max // EXPERT_BLOCK) + N_EXPERTS
    seg_len = seg[:, :, 0]
    counts = jnp.sum(seg_len, axis=0)
    padded = (counts + EXPERT_BLOCK - 1) // EXPERT_BLOCK * EXPERT_BLOCK
    pad_end = jnp.cumsum(padded)
    pad_start = pad_end - padded
    seg_first = pad_start[None, :] + jnp.cumsum(seg_len, axis=0) - seg_len
    seg_local = jnp.cumsum(seg_len, axis=1) - seg_len
    segs = jnp.stack([seg_len, seg_local, seg_first], axis=1).astype(jnp.int32)
    by_token = lambda a: a.transpose(0, 2, 1).reshape(t, 8)
    blk_row = jnp.arange(n_blocks, dtype=jnp.int32) * EXPERT_BLOCK
    blk_e = jnp.minimum(jnp.sum(pad_end[None, :] <= blk_row[:, None], axis=1), N_EXPERTS - 1).astype(jnp.int32)
    n_used = (pad_end[-1:] // EXPERT_BLOCK).astype(jnp.int32)
    tails = jnp.stack([pad_start + counts, padded - counts]).astype(jnp.int32)
    free = jnp.concatenate([pad_end[-1:], n_blocks - n_used]).astype(jnp.int32)
    xs = _dispatch(segs, tails, free, mi, hf, n_blocks * EXPERT_BLOCK)
    ys = _experts(blk_e, n_used, xs, w1, b1, w2, b2)
    return _combine(segs, by_token(mi), by_token(mw), xn, mods, g, ys, nb, all_tokens)


def _proj1_kernel(x_ref, mod_ref, g_ref, w_ref, q_ref, k_ref, v_ref):
    h = (_rms(x_ref[...], g_ref[...]) * (1.0 + _mod(mod_ref, 1)) + _mod(mod_ref, 0)).astype(BF16)
    qkv = _dot(h, w_ref[...])
    q_ref[...] = (qkv[:, :D] * NA_SCALE).astype(BF16)
    k_ref[...] = qkv[:, D:2 * D].astype(BF16)
    v_ref[...] = qkv[:, 2 * D:].astype(BF16)


def _proj1(x1, mods, g, w, nb):
    n_tiles = nb * SMP_TILES
    tile_map = lambda i: (i, 0)
    return pl.pallas_call(
        _proj1_kernel,
        grid=(n_tiles,),
        in_specs=[pl.BlockSpec((TILE, D), tile_map),
                  pl.BlockSpec((1, 1, 6 * D), _mod_map_all(nb)),
                  pl.BlockSpec((1, D), lambda i: (0, 0)),
                  pl.BlockSpec((D, 3 * D), lambda i: (0, 0))],
        out_specs=[pl.BlockSpec((TILE, D), tile_map)] * 3,
        out_shape=[jax.ShapeDtypeStruct((n_tiles * TILE, D), BF16)] * 3,
        compiler_params=_params(("arbitrary",)),
        name="proj_odd",
    )(x1, mods, g, w)


def _na_row_start(r):
    return jnp.clip(r - WIN_ROWS // 2, 0, GRID_H - WIN_ROWS)


def _na_kernel(q_ref, k_ref, v_ref, bias_ref, o_ref):
    r = pl.program_id(1)
    k0 = pl.multiple_of(_na_row_start(r) * GRID_W, GRID_W)
    n_loc = WIN_ROWS * GRID_W
    lane = lax.broadcasted_iota(jnp.int32, (GRID_W, 128), 1)
    pairs = range(NA_HEADS // 2)
    sls = [slice(j * 128, (j + 1) * 128) for j in pairs]
    qqs = []
    for j in pairs:
        q = q_ref[:, sls[j]]
        zero = jnp.zeros_like(q)
        qqs.append(jnp.concatenate([jnp.where(lane < NA_DH, q, zero),
                                    jnp.where(lane >= NA_DH, q, zero)], axis=0))
    s_loc = jnp.concatenate([_dot_nt(qqs[j], k_ref[pl.ds(k0, n_loc), sls[j]]) for j in pairs], axis=0)
    s_loc = s_loc + bias_ref[0]
    s_ctx = jnp.concatenate([_dot_nt(qqs[j], k_ref[SEQ:, sls[j]]) for j in pairs], axis=0)
    m = jnp.maximum(jnp.max(s_loc, axis=-1, keepdims=True), jnp.max(s_ctx, axis=-1, keepdims=True))
    p_loc = jnp.exp(s_loc - m)
    p_ctx = jnp.exp(s_ctx - m)
    inv_l = 1.0 / (jnp.sum(p_loc, axis=-1, keepdims=True) + jnp.sum(p_ctx, axis=-1, keepdims=True))
    p_loc = p_loc.astype(BF16)
    p_ctx = p_ctx.astype(BF16)
    for j in pairs:
        rows = slice(j * 2 * GRID_W, (j + 1) * 2 * GRID_W)
        pv = (_dot(p_loc[rows], v_ref[pl.ds(k0, n_loc), sls[j]])
              + _dot(p_ctx[rows], v_ref[SEQ:, sls[j]])) * inv_l[rows]
        o_ref[:, sls[j]] = jnp.where(lane < NA_DH, pv[:GRID_W], pv[GRID_W:]).astype(BF16)


def _na_bias(rpb):
    mid = WIN_ROWS // 2
    pat_rows = list(range(mid)) + [mid] + list(range(GRID_H - mid + 1, GRID_H))
    r = np.array(pat_rows)
    rs = np.clip(r - mid, 0, GRID_H - WIN_ROWS)
    row_off = rs[:, None] + np.arange(WIN_ROWS)[None, :] - r[:, None] + WIN_ROWS - 1
    c = np.arange(GRID_W)
    q_start = np.clip(c - WIN_COLS // 2, 0, GRID_W - WIN_COLS)[:, None]
    kc = np.arange(GRID_W)[None, :]
    valid = (kc >= q_start) & (kc < q_start + WIN_COLS)
    col_off = np.clip(kc - c[:, None] + WIN_COLS - 1, 0, 2 * WIN_COLS - 2)
    sel_r = np.eye(2 * WIN_ROWS - 1, dtype=np.float32)[row_off]
    sel_c = np.eye(2 * WIN_COLS - 1, dtype=np.float32)[col_off]
    b = jnp.einsum('pia,hab,ckb->hpick', sel_r, rpb.astype(F32), sel_c, precision=lax.Precision.HIGHEST)
    b = jnp.where(valid[None, None, None], b, MASK_VALUE)
    b = b.transpose(1, 0, 3, 2, 4).reshape(len(pat_rows), NA_HEADS * GRID_W, WIN_ROWS * GRID_W)
    return b


def _na_pattern(r):
    mid = WIN_ROWS // 2
    return jnp.where(r < mid, r, jnp.where(r <= GRID_H - mid, mid, r - (GRID_H - 2 * mid)))


def _na(q, k, v, bias, nb):
    blocks_per_smp = SMP_ROWS // GRID_W
    return pl.pallas_call(
        _na_kernel,
        grid=(nb, GRID_H),
        in_specs=[pl.BlockSpec((GRID_W, D), lambda b, r: (b * blocks_per_smp + r, 0)),
                  pl.BlockSpec((SMP_ROWS, D), lambda b, r: (b, 0)),
                  pl.BlockSpec((SMP_ROWS, D), lambda b, r: (b, 0)),
                  pl.BlockSpec((1,) + bias.shape[1:], lambda b, r: (_na_pattern(r), 0, 0))],
        out_specs=pl.BlockSpec((GRID_W, D), lambda b, r: (b * GRID_H + r, 0)),
        out_shape=jax.ShapeDtypeStruct((nb * SEQ, D), BF16),
        compiler_params=_params(("arbitrary", "arbitrary")),
        name="na_attn",
    )(q, k, v, bias)


def kernel(x, c, ctx, c_ctx, ada_w, ada_b, mix_pre_g, mix_post_g, ffn_pre_g, ffn_post_g, even_w_in,
           diff_lambda, diff_subln_g, mla_q_norm_g, mla_w_qb, mla_kv_norm_g, mla_w_kvb, even_w_out,
           na_w_qkv, na_rpb, na_w_out, router_w, router_b, moe_w1, moe_b1, moe_w2, moe_b2):
    nb = x.shape[0]
    assert x.shape[1:] == (SEQ, D) and ctx.shape[1:] == (CTX, D)
    x2 = x.reshape(nb * SEQ, D)
    c2 = ctx.reshape(nb * CTX, D)
    row = lambda a: a.reshape(1, -1)

    mods = _ada(jnp.concatenate([c, c_ctx[None, :]], axis=0), ada_w, ada_b)
    mods0 = mods[0].reshape(nb + 1, 1, 6 * D)
    mods1 = mods[1].reshape(nb + 1, 1, 6 * D)

    lam_init = 0.8 - 0.6 * math.exp(-0.3 * 0)
    w_in = even_w_in[0]
    wa = jnp.pad(w_in, ((0, 0), (0, 2048 - w_in.shape[1]))).astype(BF16)
    wqb = mla_w_qb[0].reshape(MLA_Q_LORA, MLA_HEADS, MLA_NOPE + MLA_ROPE)
    wqb = jnp.pad(wqb, ((0, 0), (0, 0), (0, MLA_QK_PAD - MLA_NOPE - MLA_ROPE)))
    wqb = wqb.reshape(MLA_Q_LORA, MLA_HEADS * MLA_QK_PAD).astype(BF16)
    qd, kd, vd, qm, km, vm = _proj0(x2, c2, mods0, row(mix_pre_g[0]), wa, row(mla_q_norm_g[0]), wqb,
                                    row(mla_kv_norm_g[0]), mla_w_kvb[0].astype(BF16), nb)
    o0 = _attn0(qd, kd, vd, qm, km, vm, diff_lambda[0], row(diff_subln_g[0]), lam_init, nb)
    xn, hf, mi, mw, seg = _oproj(o0, (x2, c2), mods0, even_w_out[0].astype(BF16), row(mix_post_g[0]),
                                 row(ffn_pre_g[0]), router_w[0], row(router_b[0]), nb, True)
    x1 = _moe(hf, mi, mw, seg, xn, mods0, row(ffn_post_g[0]), moe_w1[0].astype(BF16), moe_b1[0],
              moe_w2[0].astype(BF16), moe_b2[0], nb, True)

    q, k, v = _proj1(x1, mods1, row(mix_pre_g[1]), na_w_qkv[0].astype(BF16), nb)
    o1 = _na(q, k, v, _na_bias(na_rpb[0]), nb)
    xn, hf, mi, mw, seg = _oproj(o1, (x1,), mods1, na_w_out[0].astype(BF16), row(mix_post_g[1]),
                                 row(ffn_pre_g[1]), router_w[1], row(router_b[1]), nb, False)
    out = _moe(hf, mi, mw, seg, xn, mods1, row(ffn_post_g[1]), moe_w1[1].astype(BF16), moe_b1[1],
               moe_w2[1].astype(BF16), moe_b2[1], nb, False)
    return out.reshape(nb, SEQ, D)
```

```python
import functools
import math

import numpy as np
import jax
import jax.numpy as jnp
from jax import lax
from jax.experimental import pallas as pl
from jax.experimental.pallas import tpu as pltpu

F32 = jnp.float32
BF16 = jnp.bfloat16

D = 1024
SEQ = 2048
CTX = 256
GRID_W = 64
GRID_H = SEQ // GRID_W
TILE = 256
LAT_TILES = SEQ // TILE
SMP_TILES = LAT_TILES + 1
SMP_ROWS = SEQ + CTX
EPS = 1e-6
ROPE_BASE = 10000.0

DIFF_HEADS = 4
DIFF_DH = 64
DIFF_SCALE = DIFF_DH ** -0.5
MLA_HEADS = 4
MLA_Q_LORA = 256
MLA_KV_LORA = 128
MLA_NOPE = 128
MLA_ROPE = 64
MLA_V = 128
MLA_SCALE = (MLA_NOPE + MLA_ROPE) ** -0.5
MLA_QK_PAD = 256
NA_HEADS = 16
NA_DH = 64
NA_SCALE = NA_DH ** -0.5
WIN_ROWS = 8
WIN_COLS = 16
N_EXPERTS = 32
TOP_K = 4
D_FF = 1024
SWIGLU_ALPHA = 1.702
SWIGLU_LIMIT = 7.0
EXPERT_BLOCK = 512
SEG_ALIGN = 8
SORT_ROWS = -(-(TILE * TOP_K + N_EXPERTS * (SEG_ALIGN - 1)) // 256) * 256
MASK_VALUE = -1e30
LOG2_E = math.log2(math.e)
META_LANES = 128
VMEM_LIMIT = 60 * 1024 * 1024

_NT = (((1,), (1,)), ((), ()))


def _dot(a, b):
    return jnp.dot(a, b, preferred_element_type=F32)


def _dot_nt(a, b):
    return lax.dot_general(a, b, _NT, preferred_element_type=F32)


def _dot_f32(a, b):
    return jnp.dot(a, b, preferred_element_type=F32, precision=lax.Precision.HIGHEST)


def _rms(x, g):
    return x * lax.rsqrt(jnp.mean(x * x, axis=-1, keepdims=True) + EPS) * g


def _mod(mod_ref, k):
    return mod_ref[0, :, k * D:(k + 1) * D]


def _params(sem):
    return pltpu.CompilerParams(dimension_semantics=sem, vmem_limit_bytes=VMEM_LIMIT)


def _ada_kernel(c_ref, w_ref, b_ref, o_ref):
    c = c_ref[...]
    s = c * jax.nn.sigmoid(c)
    o_ref[0] = _dot_f32(s, w_ref[0]) + b_ref[0]


def _ada(cc, ada_w, ada_b):
    depth = ada_w.shape[0]
    n = cc.shape[0]
    nt = 6 * D // D
    return pl.pallas_call(
        _ada_kernel,
        grid=(depth, nt),
        in_specs=[pl.BlockSpec((n, D), lambda l, j: (0, 0)),
                  pl.BlockSpec((1, D, D), lambda l, j: (l, 0, j)),
                  pl.BlockSpec((1, 1, D), lambda l, j: (l, 0, j))],
        out_specs=pl.BlockSpec((1, n, D), lambda l, j: (l, 0, j)),
        out_shape=jax.ShapeDtypeStruct((depth, n, 6 * D), F32),
        compiler_params=_params(("arbitrary", "arbitrary")),
        name="ada",
    )(cc, ada_w, ada_b.reshape(depth, 1, 6 * D))


def _x_lat_map(i):
    return ((i // SMP_TILES) * LAT_TILES + jnp.minimum(i % SMP_TILES, LAT_TILES - 1), 0)


def _x_ctx_map(i):
    return (i // SMP_TILES, 0)


def _mod_map_all(nb):
    return lambda i: (jnp.where(i % SMP_TILES == LAT_TILES, nb, i // SMP_TILES), 0, 0)


def _lat_of_all(i):
    return (i // LAT_TILES) * SMP_TILES + i % LAT_TILES


def _rope(x, cos, sa, sb):
    return x * cos + pltpu.roll(x, 112, 1) * sa + pltpu.roll(x, 16, 1) * sb


def _proj0_kernel(x_ref, c_ref, mod_ref, g_ref, wa_ref, qng_ref, wqb_ref, kvng_ref, wkvb_ref,
                  cos_ref, sa_ref, sb_ref, qd_ref, kd_ref, vd_ref, qm_ref, km_ref, vm_ref):
    is_ctx = pl.program_id(0) % SMP_TILES == LAT_TILES
    x = jnp.where(is_ctx, c_ref[...], x_ref[...])
    h = (_rms(x, g_ref[...]) * (1.0 + _mod(mod_ref, 1)) + _mod(mod_ref, 0)).astype(BF16)
    big = _dot(h, wa_ref[...])
    cos, sa, sb = cos_ref[...], sa_ref[...], sb_ref[...]
    nd = DIFF_HEADS * 2 * DIFF_DH
    for j in range(nd // 128):
        sl = slice(j * 128, (j + 1) * 128)
        qd_ref[:, sl] = _rope(big[:, j * 128:(j + 1) * 128], cos, sa, sb).astype(BF16)
        kd_ref[:, sl] = _rope(big[:, nd + j * 128:nd + (j + 1) * 128], cos, sa, sb).astype(BF16)
    vd_ref[...] = big[:, 2 * nd:3 * nd].astype(BF16)
    o = 3 * nd
    cq = big[:, o:o + MLA_Q_LORA]
    ckv = big[:, o + MLA_Q_LORA:o + MLA_Q_LORA + MLA_KV_LORA]
    kpe = _rope(big[:, o + MLA_Q_LORA + MLA_KV_LORA:], cos, sa, sb).astype(BF16)
    qm = _dot(_rms(cq, qng_ref[...]).astype(BF16), wqb_ref[...])
    kv = _dot(_rms(ckv, kvng_ref[...]).astype(BF16), wkvb_ref[...])
    for hd in range(MLA_HEADS):
        b0 = hd * MLA_QK_PAD
        qm_ref[:, b0:b0 + 128] = qm[:, b0:b0 + 128].astype(BF16)
        qm_ref[:, b0 + 128:b0 + 256] = _rope(qm[:, b0 + 128:b0 + 256], cos, sa, sb).astype(BF16)
        km_ref[:, b0:b0 + 128] = kv[:, hd * 256:hd * 256 + 128].astype(BF16)
        km_ref[:, b0 + 128:b0 + 256] = kpe
        vm_ref[:, hd * 128:(hd + 1) * 128] = kv[:, hd * 256 + 128:(hd + 1) * 256].astype(BF16)


def _rope_tables():
    t = np.arange(SEQ)
    half = 16
    inv = (ROPE_BASE ** (-np.arange(half, dtype=np.float32) * 2.0 / (2 * half))).astype(np.float32)
    ang_r = (t // GRID_W).astype(np.float32)[:, None] * inv[None, :]
    ang_c = (t % GRID_W).astype(np.float32)[:, None] * inv[None, :]
    cr, sr, cc, sc = np.cos(ang_r), np.sin(ang_r), np.cos(ang_c), np.sin(ang_c)
    z = np.zeros_like(cr)
    cos = np.concatenate([cr, cr, cc, cc], axis=1)
    sa = np.concatenate([-sr, z, -sc, z], axis=1)
    sb = np.concatenate([z, sr, z, sc], axis=1)

    def full(tab, fill):
        tab = np.tile(tab, (1, 2))
        ident = np.full((CTX, 128), fill, np.float32)
        return jnp.asarray(np.concatenate([tab, ident], axis=0).astype(np.float32))

    return full(cos, 1.0), full(sa, 0.0), full(sb, 0.0)


def _proj0(x2, c2, mods, g, wa, qng, wqb, kvng, wkvb, nb):
    n_tiles = nb * SMP_TILES
    rows = nb * SMP_ROWS
    cos, sa, sb = _rope_tables()
    tile_map = lambda i: (i, 0)
    const = lambda i: (0, 0)
    tab_spec = pl.BlockSpec((TILE, 128), lambda i: (i % SMP_TILES, 0))
    widths = (512, 512, 512, 1024, 1024, 512)
    return pl.pallas_call(
        _proj0_kernel,
        grid=(n_tiles,),
        in_specs=[pl.BlockSpec((TILE, D), _x_lat_map),
                  pl.BlockSpec((TILE, D), _x_ctx_map),
                  pl.BlockSpec((1, 1, 6 * D), _mod_map_all(nb)),
                  pl.BlockSpec((1, D), const),
                  pl.BlockSpec(wa.shape, const),
                  pl.BlockSpec((1, MLA_Q_LORA), const),
                  pl.BlockSpec(wqb.shape, const),
                  pl.BlockSpec((1, MLA_KV_LORA), const),
                  pl.BlockSpec(wkvb.shape, const),
                  tab_spec, tab_spec, tab_spec],
        out_specs=[pl.BlockSpec((TILE, w), tile_map) for w in widths],
        out_shape=[jax.ShapeDtypeStruct((rows, w), BF16) for w in widths],
        compiler_params=_params(("arbitrary",)),
        name="proj_even",
    )(x2, c2, mods, g, wa, qng, wqb, kvng, wkvb, cos, sa, sb)


def _softmax_parts(s, scale):
    c = scale * LOG2_E
    m = jnp.max(s, axis=-1, keepdims=True)
    p = jnp.exp2(s * c - m * c)
    return p, jnp.sum(p, axis=-1, keepdims=True)


def _attn0_kernel(lam_init, qd_ref, kd_ref, vd_ref, qm_ref, km_ref, vm_ref, lam_ref, sg_ref, o_ref):
    lv = lam_ref[...]
    lam = (jnp.exp(jnp.sum(lv[0:1] * lv[1:2], axis=1, keepdims=True))
           - jnp.exp(jnp.sum(lv[2:3] * lv[3:4], axis=1, keepdims=True)) + lam_init)
    lane = lax.broadcasted_iota(jnp.int32, (TILE, 128), 1)

    def heads(k_lo, nk):
        def diff_scores(hd):
            sl = slice(hd * 128, (hd + 1) * 128)
            q = qd_ref[:, sl]
            zero = jnp.zeros_like(q)
            qq = jnp.concatenate([jnp.where(lane < DIFF_DH, q, zero),
                                  jnp.where(lane >= DIFF_DH, q, zero)], axis=0)
            return _dot_nt(qq, kd_ref[k_lo:k_lo + nk, sl])

        def diff_finish(hd, s):
            sl = slice(hd * 128, (hd + 1) * 128)
            p, l = _softmax_parts(s, DIFF_SCALE)
            pv = _dot(p.astype(BF16), vd_ref[k_lo:k_lo + nk, sl])
            o = pv[:TILE] / l[:TILE] - lam * (pv[TILE:] / l[TILE:])
            o_ref[:, sl] = (_rms(o, sg_ref[...]) * (1.0 - lam_init)).astype(BF16)

        def mla_scores(hd):
            ql = slice(hd * MLA_QK_PAD, (hd + 1) * MLA_QK_PAD)
            return _dot_nt(qm_ref[:, ql], km_ref[k_lo:k_lo + nk, ql])

        def mla_finish(hd, s):
            p, l = _softmax_parts(s, MLA_SCALE)
            pv = _dot(p.astype(BF16), vm_ref[k_lo:k_lo + nk, hd * 128:(hd + 1) * 128])
            o_ref[:, 512 + hd * 128:512 + (hd + 1) * 128] = (pv / l).astype(BF16)

        jobs = ([(diff_scores, diff_finish, hd) for hd in range(DIFF_HEADS)]
                + [(mla_scores, mla_finish, hd) for hd in range(MLA_HEADS)])
        s_next = jobs[0][0](jobs[0][2])
        for n, (_, finish, hd) in enumerate(jobs):
            s = s_next
            if n + 1 < len(jobs):
                s_next = jobs[n + 1][0](jobs[n + 1][2])
            finish(hd, s)

    is_ctx = pl.program_id(1) == LAT_TILES

    @pl.when(jnp.logical_not(is_ctx))
    def _():
        heads(0, SMP_ROWS)

    @pl.when(is_ctx)
    def _():
        heads(SEQ, CTX)


def _attn0(qd, kd, vd, qm, km, vm, lam_vec, subln_g, lam_init, nb):
    q_map = lambda b, p: (b * SMP_TILES + p, 0)
    k_map = lambda b, p: (b, 0)
    const = lambda b, p: (0, 0)
    return pl.pallas_call(
        functools.partial(_attn0_kernel, lam_init),
        grid=(nb, SMP_TILES),
        in_specs=[pl.BlockSpec((TILE, 512), q_map),
                  pl.BlockSpec((SMP_ROWS, 512), k_map),
                  pl.BlockSpec((SMP_ROWS, 512), k_map),
                  pl.BlockSpec((TILE, 1024), q_map),
                  pl.BlockSpec((SMP_ROWS, 1024), k_map),
                  pl.BlockSpec((SMP_ROWS, 512), k_map),
                  pl.BlockSpec(lam_vec.shape, const),
                  pl.BlockSpec((1, 128), const)],
        out_specs=pl.BlockSpec((TILE, D), q_map),
        out_shape=jax.ShapeDtypeStruct((nb * SMP_ROWS, D), BF16),
        compiler_params=_params(("arbitrary", "arbitrary")),
        name="attn_even",
    )(qd, kd, vd, qm, km, vm, lam_vec, subln_g)


def _oproj_kernel(dual, *refs):
    if dual:
        (o_ref, xl_ref, xc_ref, mod_ref, wo_ref, pg_ref, fg_ref, rw_ref, rb_ref,
         xn_ref, hf_ref, mi_ref, mw_ref, seg_ref) = refs
        is_ctx = pl.program_id(0) % SMP_TILES == LAT_TILES
        x = jnp.where(is_ctx, xc_ref[...], xl_ref[...])
    else:
        (o_ref, x_ref, mod_ref, wo_ref, pg_ref, fg_ref, rw_ref, rb_ref,
         xn_ref, hf_ref, mi_ref, mw_ref, seg_ref) = refs
        x = x_ref[...]

    y = _dot(o_ref[...], wo_ref[...])
    xn = x + _mod(mod_ref, 2) * _rms(y, pg_ref[...])
    xn_ref[...] = xn
    hf = _rms(xn, fg_ref[...]) * (1.0 + _mod(mod_ref, 4)) + _mod(mod_ref, 3)
    hi = hf.astype(BF16)
    hf_ref[...] = hi

    lo = (hf - hi.astype(F32)).astype(BF16)
    lt = _dot_nt(rw_ref[...], jnp.concatenate([hi, lo], axis=0))
    ne = N_EXPERTS
    logits = (lt[:ne, :TILE] + lt[ne:, :TILE]) + (lt[:ne, TILE:] + lt[ne:, TILE:]) + rb_ref[...]

    eid = lax.broadcasted_iota(jnp.int32, (ne, TILE), 0).astype(F32)
    sels, vals = [], []
    for _ in range(TOP_K):
        m = jnp.max(logits, axis=0, keepdims=True)
        idx = jnp.min(jnp.where(logits == m, eid, float(ne)), axis=0, keepdims=True)
        sel = eid == idx
        sels.append(sel)
        vals.append(m)
        logits = jnp.where(sel, -jnp.inf, logits)
    ex = [jnp.exp(v - vals[0]) for v in vals]
    den = ex[0] + ex[1] + ex[2] + ex[3]

    onehot = (sels[0] | sels[1] | sels[2] | sels[3]).astype(F32)
    r_i = lax.broadcasted_iota(jnp.int32, (TILE, TILE), 0)
    c_i = lax.broadcasted_iota(jnp.int32, (TILE, TILE), 1)
    before = _dot(onehot.astype(BF16), (r_i < c_i).astype(BF16))
    cnt = jnp.sum(onehot, axis=1, keepdims=True)
    seg_len = jnp.floor((cnt + (SEG_ALIGN - 1)) * (1.0 / SEG_ALIGN)) * SEG_ALIGN
    e_r = lax.broadcasted_iota(jnp.int32, (ne, ne), 0)
    e_c = lax.broadcasted_iota(jnp.int32, (ne, ne), 1)
    seg_len_b = jnp.broadcast_to(seg_len, (ne, META_LANES))
    seg_off = _dot_f32((e_c < e_r).astype(F32), seg_len_b)[:, 0:1]
    pos = before + seg_off
    sub = lax.broadcasted_iota(jnp.int32, (8, TILE), 0)
    mi = jnp.zeros((8, TILE), jnp.int32)
    mw = jnp.zeros((8, TILE), F32)
    for k in range(TOP_K):
        row = jnp.sum(jnp.where(sels[k], pos, 0.0), axis=0, keepdims=True).astype(jnp.int32)
        mi = jnp.where(sub == k, row, mi)
        mw = jnp.where(sub == k, ex[k] / den, mw)
    mi_ref[0] = mi
    mw_ref[0] = mw
    seg_ref[0] = seg_len_b.astype(jnp.int32)


def _oproj(o, x_args, mods, wo, pg, fg, rw, rb, nb, all_tokens):
    const = lambda i: (0, 0)
    tile_map = lambda i: (i, 0)
    tile3_map = lambda i: (i, 0, 0)
    rw_hi = rw.astype(BF16)
    rw_lo = (rw - rw_hi.astype(F32)).astype(BF16)
    rw_split = jnp.concatenate([rw_hi.T, rw_lo.T], axis=0)
    if all_tokens:
        n_tiles = nb * SMP_TILES
        x_specs = [pl.BlockSpec((TILE, D), _x_lat_map), pl.BlockSpec((TILE, D), _x_ctx_map)]
        mod_map = _mod_map_all(nb)
    else:
        n_tiles = nb * LAT_TILES
        x_specs = [pl.BlockSpec((TILE, D), lambda i: (_lat_of_all(i), 0))]
        mod_map = lambda i: (i // LAT_TILES, 0, 0)
    rows = n_tiles * TILE
    return pl.pallas_call(
        functools.partial(_oproj_kernel, all_tokens),
        grid=(n_tiles,),
        in_specs=[pl.BlockSpec((TILE, D), tile_map)] + x_specs + [
            pl.BlockSpec((1, 1, 6 * D), mod_map),
            pl.BlockSpec((D, D), const),
            pl.BlockSpec((1, D), const),
            pl.BlockSpec((1, D), const),
            pl.BlockSpec((2 * N_EXPERTS, D), const),
            pl.BlockSpec((N_EXPERTS, 1), const)],
        out_specs=[pl.BlockSpec((TILE, D), tile_map),
                   pl.BlockSpec((TILE, D), tile_map),
                   pl.BlockSpec((1, 8, TILE), tile3_map),
                   pl.BlockSpec((1, 8, TILE), tile3_map),
                   pl.BlockSpec((1, N_EXPERTS, META_LANES), tile3_map)],
        out_shape=[jax.ShapeDtypeStruct((rows, D), F32),
                   jax.ShapeDtypeStruct((rows, D), BF16),
                   jax.ShapeDtypeStruct((n_tiles, 8, TILE), jnp.int32),
                   jax.ShapeDtypeStruct((n_tiles, 8, TILE), F32),
                   jax.ShapeDtypeStruct((n_tiles, N_EXPERTS, META_LANES), jnp.int32)],
        compiler_params=_params(("arbitrary",)),
        name="oproj_router",
    )(o, *x_args, mods, wo, pg, fg, rw_split, rb.reshape(N_EXPERTS, 1))


def _for_pieces(n, largest, fn):
    piece = largest
    while piece >= SEG_ALIGN:
        done = (n // (2 * piece)) * (2 * piece)
        pl.when((n & piece) != 0)(functools.partial(fn, done, piece))
        piece //= 2


def _segment_copies(seg_ref, local_ref, hbm_ref, sem, to_hbm, action):
    def per_expert(e, carry):
        def piece_copy(done, rows):
            loc = local_ref.at[pl.ds(pl.multiple_of(seg_ref[0, 1, e] + done, SEG_ALIGN), rows)]
            glb = hbm_ref.at[pl.ds(pl.multiple_of(seg_ref[0, 2, e] + done, SEG_ALIGN), rows)]
            action(pltpu.make_async_copy(loc, glb, sem) if to_hbm else pltpu.make_async_copy(glb, loc, sem))

        _for_pieces(seg_ref[0, 0, e], TILE, piece_copy)
        return carry

    lax.fori_loop(0, N_EXPERTS, per_expert, 0)


def _zero_fill_copies(tail_ref, free_ref, zbuf, xs_ref, sem, action):
    def per_expert(e, carry):
        def piece_copy(done, rows):
            dst = xs_ref.at[pl.ds(pl.multiple_of(tail_ref[0, e] + done, SEG_ALIGN), rows)]
            action(pltpu.make_async_copy(zbuf.at[pl.ds(0, rows)], dst, sem))

        _for_pieces(tail_ref[1, e], EXPERT_BLOCK // 2, piece_copy)
        return carry

    def per_block(j, carry):
        first = pl.multiple_of(free_ref[0] + j * EXPERT_BLOCK, EXPERT_BLOCK)
        action(pltpu.make_async_copy(zbuf, xs_ref.at[pl.ds(first, EXPERT_BLOCK)], sem))
        return carry

    lax.fori_loop(0, N_EXPERTS, per_expert, 0)
    lax.fori_loop(0, free_ref[1], per_block, 0)


def _dispatch_kernel(seg_ref, seg_prev_ref, tail_ref, free_ref, lp_ref, h_ref, xs_ref, sbuf, zbuf, sem):
    i = pl.program_id(0)
    last = pl.num_programs(0) - 1
    slot = i % 2
    row = lax.broadcasted_iota(jnp.int32, (SORT_ROWS, TILE), 0)
    lp = lp_ref[0]
    hit = row == lp[0:1]
    for k in range(1, TOP_K):
        hit = hit | (row == lp[k:k + 1])
    sbuf[slot] = _dot(jnp.where(hit, 1.0, 0.0).astype(BF16), h_ref[...])
    _segment_copies(seg_ref, sbuf.at[slot], xs_ref, sem.at[slot], True, lambda cp: cp.start())

    @pl.when(i > 0)
    def _():
        _segment_copies(seg_prev_ref, sbuf.at[1 - slot], xs_ref, sem.at[1 - slot], True, lambda cp: cp.wait())

    @pl.when(i == last)
    def _():
        _segment_copies(seg_ref, sbuf.at[slot], xs_ref, sem.at[slot], True, lambda cp: cp.wait())
        zbuf[...] = jnp.zeros_like(zbuf)
        _zero_fill_copies(tail_ref, free_ref, zbuf, xs_ref, sem.at[slot], lambda cp: cp.start())
        _zero_fill_copies(tail_ref, free_ref, zbuf, xs_ref, sem.at[slot], lambda cp: cp.wait())


def _dispatch(segs, tails, free, lpos_t, hf, m_rows):
    n_tiles = hf.shape[0] // TILE
    seg_block = (1, 3, N_EXPERTS)
    return pl.pallas_call(
        _dispatch_kernel,
        grid=(n_tiles,),
        in_specs=[pl.BlockSpec(seg_block, lambda i: (i, 0, 0), memory_space=pltpu.SMEM),
                  pl.BlockSpec(seg_block, lambda i: (jnp.maximum(i - 1, 0), 0, 0), memory_space=pltpu.SMEM),
                  pl.BlockSpec(memory_space=pltpu.SMEM),
                  pl.BlockSpec(memory_space=pltpu.SMEM),
                  pl.BlockSpec((1, 8, TILE), lambda i: (i, 0, 0)),
                  pl.BlockSpec((TILE, D), lambda i: (i, 0))],
        out_specs=pl.BlockSpec(memory_space=pl.ANY),
        out_shape=jax.ShapeDtypeStruct((m_rows, D), F32),
        scratch_shapes=[pltpu.VMEM((2, SORT_ROWS, D), F32), pltpu.VMEM((EXPERT_BLOCK, D), F32),
                        pltpu.SemaphoreType.DMA((2,))],
        compiler_params=_params(("arbitrary",)),
        name="moe_dispatch",
    )(segs, segs, tails, free, lpos_t, hf)


def _expert_kernel(be_ref, nu_ref, xs_ref, w1_ref, b1_ref, w2_ref, b2_ref, ys_ref, w1b, w2b):
    i = pl.program_id(0)
    used = i < nu_ref[0]
    new_expert = jnp.logical_or(i == 0, be_ref[i] != be_ref[jnp.maximum(i - 1, 0)])

    @pl.when(jnp.logical_and(used, new_expert))
    def _():
        w1b[...] = w1_ref[0, 0].astype(BF16)
        w2b[...] = w2_ref[0, 0].astype(BF16)

    @pl.when(used)
    def _():
        gu = _dot(xs_ref[...].astype(BF16), w1b[...]) + b1_ref[0, 0]
        gate = jnp.minimum(gu[:, :D_FF], SWIGLU_LIMIT)
        lin = jnp.clip(gu[:, D_FF:], -SWIGLU_LIMIT, SWIGLU_LIMIT)
        act = gate * jax.nn.sigmoid(SWIGLU_ALPHA * gate) * (lin + 1.0)
        ys_ref[...] = _dot(act.astype(BF16), w2b[...]) + b2_ref[0, 0]

    @pl.when(i >= nu_ref[0])
    def _():
        ys_ref[...] = jnp.zeros_like(ys_ref)


def _experts(blk_e, n_used, xs, layer, w1, b1, w2, b2):
    n_blocks = xs.shape[0] // EXPERT_BLOCK
    in_row_map = lambda i, be, nu: (jnp.minimum(i, nu[0] - 1), 0)
    e_map = lambda i, be, nu: (layer, be[i], 0, 0)
    grid_spec = pltpu.PrefetchScalarGridSpec(
        num_scalar_prefetch=2,
        grid=(n_blocks,),
        in_specs=[pl.BlockSpec((EXPERT_BLOCK, D), in_row_map),
                  pl.BlockSpec((1, 1, D, 2 * D_FF), e_map),
                  pl.BlockSpec((1, 1, 1, 2 * D_FF), e_map),
                  pl.BlockSpec((1, 1, D_FF, D), e_map),
                  pl.BlockSpec((1, 1, 1, D), e_map)],
        out_specs=pl.BlockSpec((EXPERT_BLOCK, D), lambda i, be, nu: (i, 0)),
        scratch_shapes=[pltpu.VMEM((D, 2 * D_FF), BF16), pltpu.VMEM((D_FF, D), BF16)])
    return pl.pallas_call(
        _expert_kernel,
        grid_spec=grid_spec,
        out_shape=jax.ShapeDtypeStruct(xs.shape, F32),
        compiler_params=_params(("arbitrary",)),
        name="moe_experts",
    )(blk_e, n_used, xs, w1, b1[:, :, None, :], w2, b2[:, :, None, :])


def _combine_kernel(seg_ref, seg_next_ref, mi_ref, mw_ref, x_ref, mod_ref, g_ref, ys_ref, o_ref, ybuf, sem):
    i = pl.program_id(0)
    slot = i % 2

    @pl.when(i == 0)
    def _():
        ybuf[...] = jnp.zeros_like(ybuf)
        _segment_copies(seg_ref, ybuf.at[0], ys_ref, sem.at[0], False, lambda cp: cp.start())

    @pl.when(i + 1 < pl.num_programs(0))
    def _():
        _segment_copies(seg_next_ref, ybuf.at[1 - slot], ys_ref, sem.at[1 - slot], False, lambda cp: cp.start())

    col = lax.broadcasted_iota(jnp.int32, (TILE, SORT_ROWS), 1)
    mi = mi_ref[...]
    mw = mw_ref[...]
    wm = jnp.zeros((TILE, SORT_ROWS), F32)
    for k in range(TOP_K):
        wm = jnp.where(col == mi[:, k:k + 1], mw[:, k:k + 1], wm)
    _segment_copies(seg_ref, ybuf.at[slot], ys_ref, sem.at[slot], False, lambda cp: cp.wait())
    y = _dot(wm.astype(BF16), ybuf[slot].astype(BF16))
    o_ref[...] = x_ref[...] + _mod(mod_ref, 5) * _rms(y, g_ref[...])


def _combine(segs, mi, mw, xn, mods, g, ys, nb, all_tokens):
    n_tiles = xn.shape[0] // TILE
    tile_map = lambda i: (i, 0)
    mod_map = _mod_map_all(nb) if all_tokens else (lambda i: (i // LAT_TILES, 0, 0))
    seg_block = (1, 3, N_EXPERTS)
    return pl.pallas_call(
        _combine_kernel,
        grid=(n_tiles,),
        in_specs=[pl.BlockSpec(seg_block, lambda i: (i, 0, 0), memory_space=pltpu.SMEM),
                  pl.BlockSpec(seg_block, lambda i: (jnp.minimum(i + 1, n_tiles - 1), 0, 0),
                               memory_space=pltpu.SMEM),
                  pl.BlockSpec((TILE, 8), tile_map),
                  pl.BlockSpec((TILE, 8), tile_map),
                  pl.BlockSpec((TILE, D), tile_map),
                  pl.BlockSpec((1, 1, 6 * D), mod_map),
                  pl.BlockSpec((1, D), lambda i: (0, 0)),
                  pl.BlockSpec(memory_space=pl.ANY)],
        out_specs=pl.BlockSpec((TILE, D), tile_map),
        out_shape=jax.ShapeDtypeStruct(xn.shape, F32),
        scratch_shapes=[pltpu.VMEM((2, SORT_ROWS, D), F32), pltpu.SemaphoreType.DMA((2,))],
        compiler_params=_params(("arbitrary",)),
        name="moe_combine",
    )(segs, segs, mi, mw, xn, mods, g, ys)


def _moe(hf, mi, mw, seg, xn, mods, g, layer, w1, b1, w2, b2, nb, all_tokens):
    t = hf.shape[0]
    n_tiles = t // TILE
    rows_max = t * TOP_K + n_tiles * N_EXPERTS * (SEG_ALIGN - 1)
    n_blocks = -(-rows_max // EXPERT_BLOCK) + N_EXPERTS
    seg_len = seg[:, :, 0]
    counts = jnp.sum(seg_len, axis=0)
    padded = (counts + EXPERT_BLOCK - 1) // EXPERT_BLOCK * EXPERT_BLOCK
    pad_end = jnp.cumsum(padded)
    pad_start = pad_end - padded
    seg_first = pad_start[None, :] + jnp.cumsum(seg_len, axis=0) - seg_len
    seg_local = jnp.cumsum(seg_len, axis=1) - seg_len
    segs = jnp.stack([seg_len, seg_local, seg_first], axis=1).astype(jnp.int32)
    by_token = lambda a: a.transpose(0, 2, 1).reshape(t, 8)
    blk_row = jnp.arange(n_blocks, dtype=jnp.int32) * EXPERT_BLOCK
    blk_e = jnp.minimum(jnp.sum(pad_end[None, :] <= blk_row[:, None], axis=1), N_EXPERTS - 1).astype(jnp.int32)
    n_used = (pad_end[-1:] // EXPERT_BLOCK).astype(jnp.int32)
    tails = jnp.stack([pad_start + counts, padded - counts]).astype(jnp.int32)
    free = jnp.concatenate([pad_end[-1:], n_blocks - n_used]).astype(jnp.int32)
    xs = _dispatch(segs, tails, free, mi, hf, n_blocks * EXPERT_BLOCK)
    ys = _experts(blk_e, n_used, xs, layer, w1, b1, w2, b2)
    return _combine(segs, by_token(mi), by_token(mw), xn, mods, g, ys, nb, all_tokens)


def _proj1_kernel(x_ref, mod_ref, g_ref, w_ref, q_ref, k_ref, v_ref):
    h = (_rms(x_ref[...], g_ref[...]) * (1.0 + _mod(mod_ref, 1)) + _mod(mod_ref, 0)).astype(BF16)
    qkv = _dot(h, w_ref[...])
    q_ref[...] = (qkv[:, :D] * NA_SCALE).astype(BF16)
    k_ref[...] = qkv[:, D:2 * D].astype(BF16)
    v_ref[...] = qkv[:, 2 * D:].astype(BF16)


def _proj1(x1, mods, g, w, nb):
    n_tiles = nb * SMP_TILES
    tile_map = lambda i: (i, 0)
    return pl.pallas_call(
        _proj1_kernel,
        grid=(n_tiles,),
        in_specs=[pl.BlockSpec((TILE, D), tile_map),
                  pl.BlockSpec((1, 1, 6 * D), _mod_map_all(nb)),
                  pl.BlockSpec((1, D), lambda i: (0, 0)),
                  pl.BlockSpec((D, 3 * D), lambda i: (0, 0))],
        out_specs=[pl.BlockSpec((TILE, D), tile_map)] * 3,
        out_shape=[jax.ShapeDtypeStruct((n_tiles * TILE, D), BF16)] * 3,
        compiler_params=_params(("arbitrary",)),
        name="proj_odd",
    )(x1, mods, g, w)


def _na_row_start(r):
    return jnp.clip(r - WIN_ROWS // 2, 0, GRID_H - WIN_ROWS)


def _na_kernel(q_ref, k_ref, v_ref, bias_ref, o_ref):
    r = pl.program_id(1)
    k0 = pl.multiple_of(_na_row_start(r) * GRID_W, GRID_W)
    n_loc = WIN_ROWS * GRID_W
    lane = lax.broadcasted_iota(jnp.int32, (GRID_W, 128), 1)
    pairs = range(NA_HEADS // 2)
    sls = [slice(j * 128, (j + 1) * 128) for j in pairs]
    qqs = []
    for j in pairs:
        q = q_ref[:, sls[j]]
        zero = jnp.zeros_like(q)
        qqs.append(jnp.concatenate([jnp.where(lane < NA_DH, q, zero),
                                    jnp.where(lane >= NA_DH, q, zero)], axis=0))
    s_loc = jnp.concatenate([_dot_nt(qqs[j], k_ref[pl.ds(k0, n_loc), sls[j]]) for j in pairs], axis=0)
    s_loc = s_loc + bias_ref[0]
    s_ctx = jnp.concatenate([_dot_nt(qqs[j], k_ref[SEQ:, sls[j]]) for j in pairs], axis=0)
    m = jnp.maximum(jnp.max(s_loc, axis=-1, keepdims=True), jnp.max(s_ctx, axis=-1, keepdims=True))
    p_loc = jnp.exp(s_loc - m)
    p_ctx = jnp.exp(s_ctx - m)
    inv_l = 1.0 / (jnp.sum(p_loc, axis=-1, keepdims=True) + jnp.sum(p_ctx, axis=-1, keepdims=True))
    p_loc = p_loc.astype(BF16)
    p_ctx = p_ctx.astype(BF16)
    for j in pairs:
        rows = slice(j * 2 * GRID_W, (j + 1) * 2 * GRID_W)
        pv = (_dot(p_loc[rows], v_ref[pl.ds(k0, n_loc), sls[j]])
              + _dot(p_ctx[rows], v_ref[SEQ:, sls[j]])) * inv_l[rows]
        o_ref[:, sls[j]] = jnp.where(lane < NA_DH, pv[:GRID_W], pv[GRID_W:]).astype(BF16)


def _na_bias(rpb):
    mid = WIN_ROWS // 2
    pat_rows = list(range(mid)) + [mid] + list(range(GRID_H - mid + 1, GRID_H))
    r = np.array(pat_rows)
    rs = np.clip(r - mid, 0, GRID_H - WIN_ROWS)
    row_off = rs[:, None] + np.arange(WIN_ROWS)[None, :] - r[:, None] + WIN_ROWS - 1
    c = np.arange(GRID_W)
    q_start = np.clip(c - WIN_COLS // 2, 0, GRID_W - WIN_COLS)[:, None]
    kc = np.arange(GRID_W)[None, :]
    valid = (kc >= q_start) & (kc < q_start + WIN_COLS)
    col_off = np.clip(kc - c[:, None] + WIN_COLS - 1, 0, 2 * WIN_COLS - 2)
    sel_r = np.eye(2 * WIN_ROWS - 1, dtype=np.float32)[row_off]
    sel_c = np.eye(2 * WIN_COLS - 1, dtype=np.float32)[col_off]
    b = jnp.einsum('pia,hab,ckb->hpick', sel_r, rpb.astype(F32), sel_c, precision=lax.Precision.HIGHEST)
    b = jnp.where(valid[None, None, None], b, MASK_VALUE)
    b = b.transpose(1, 0, 3, 2, 4).reshape(len(pat_rows), NA_HEADS * GRID_W, WIN_ROWS * GRID_W)
    return b


def _na_pattern(r):
    mid = WIN_ROWS // 2
    return jnp.where(r < mid, r, jnp.where(r <= GRID_H - mid, mid, r - (GRID_H - 2 * mid)))


def _na(q, k, v, bias, nb):
    blocks_per_smp = SMP_ROWS // GRID_W
    return pl.pallas_call(
        _na_kernel,
        grid=(nb, GRID_H),
        in_specs=[pl.BlockSpec((GRID_W, D), lambda b, r: (b * blocks_per_smp + r, 0)),
                  pl.BlockSpec((SMP_ROWS, D), lambda b, r: (b, 0)),
                  pl.BlockSpec((SMP_ROWS, D), lambda b, r: (b, 0)),
                  pl.BlockSpec((1,) + bias.shape[1:], lambda b, r: (_na_pattern(r), 0, 0))],
        out_specs=pl.BlockSpec((GRID_W, D), lambda b, r: (b * GRID_H + r, 0)),
        out_shape=jax.ShapeDtypeStruct((nb * SEQ, D), BF16),
        compiler_params=_params(("arbitrary", "arbitrary")),
        name="na_attn",
    )(q, k, v, bias)


def kernel(x, c, ctx, c_ctx, ada_w, ada_b, mix_pre_g, mix_post_g, ffn_pre_g, ffn_post_g, even_w_in,
           diff_lambda, diff_subln_g, mla_q_norm_g, mla_w_qb, mla_kv_norm_g, mla_w_kvb, even_w_out,
           na_w_qkv, na_rpb, na_w_out, router_w, router_b, moe_w1, moe_b1, moe_w2, moe_b2):
    nb = x.shape[0]
    assert x.shape[1:] == (SEQ, D) and ctx.shape[1:] == (CTX, D)
    x2 = x.reshape(nb * SEQ, D)
    c2 = ctx.reshape(nb * CTX, D)
    row = lambda a: a.reshape(1, -1)

    mods = _ada(jnp.concatenate([c, c_ctx[None, :]], axis=0), ada_w, ada_b)
    mods0 = mods[0].reshape(nb + 1, 1, 6 * D)
    mods1 = mods[1].reshape(nb + 1, 1, 6 * D)

    lam_init = 0.8 - 0.6 * math.exp(-0.3 * 0)
    w_in = even_w_in[0]
    wa = jnp.pad(w_in, ((0, 0), (0, 2048 - w_in.shape[1]))).astype(BF16)
    wqb = mla_w_qb[0].reshape(MLA_Q_LORA, MLA_HEADS, MLA_NOPE + MLA_ROPE)
    wqb = jnp.pad(wqb, ((0, 0), (0, 0), (0, MLA_QK_PAD - MLA_NOPE - MLA_ROPE)))
    wqb = wqb.reshape(MLA_Q_LORA, MLA_HEADS * MLA_QK_PAD).astype(BF16)
    qd, kd, vd, qm, km, vm = _proj0(x2, c2, mods0, row(mix_pre_g[0]), wa, row(mla_q_norm_g[0]), wqb,
                                    row(mla_kv_norm_g[0]), mla_w_kvb[0].astype(BF16), nb)
    o0 = _attn0(qd, kd, vd, qm, km, vm, diff_lambda[0], row(diff_subln_g[0]), lam_init, nb)
    xn, hf, mi, mw, seg = _oproj(o0, (x2, c2), mods0, even_w_out[0].astype(BF16), row(mix_post_g[0]),
                                 row(ffn_pre_g[0]), router_w[0], row(router_b[0]), nb, True)
    x1 = _moe(hf, mi, mw, seg, xn, mods0, row(ffn_post_g[0]), 0, moe_w1, moe_b1, moe_w2, moe_b2, nb, True)

    q, k, v = _proj1(x1, mods1, row(mix_pre_g[1]), na_w_qkv[0].astype(BF16), nb)
    o1 = _na(q, k, v, _na_bias(na_rpb[0]), nb)
    xn, hf, mi, mw, seg = _oproj(o1, (x1,), mods1, na_w_out[0].astype(BF16), row(mix_post_g[1]),
                                 row(ffn_pre_g[1]), router_w[1], row(router_b[1]), nb, False)
    out = _moe(hf, mi, mw, seg, xn, mods1, row(ffn_post_g[1]), 1, moe_w1, moe_b1, moe_w2, moe_b2, nb, False)
    return out.reshape(nb, SEQ, D)
```

```python
import functools
import math

import numpy as np
import jax
import jax.numpy as jnp
from jax import lax
from jax.experimental import pallas as pl
from jax.experimental.pallas import tpu as pltpu

F32 = jnp.float32
BF16 = jnp.bfloat16

D = 1024
SEQ = 2048
CTX = 256
GRID_W = 64
GRID_H = SEQ // GRID_W
TILE = 256
LAT_TILES = SEQ // TILE
SMP_TILES = LAT_TILES + 1
SMP_ROWS = SEQ + CTX
EPS = 1e-6
ROPE_BASE = 10000.0

DIFF_HEADS = 4
DIFF_DH = 64
DIFF_SCALE = DIFF_DH ** -0.5
MLA_HEADS = 4
MLA_Q_LORA = 256
MLA_KV_LORA = 128
MLA_NOPE = 128
MLA_ROPE = 64
MLA_V = 128
MLA_SCALE = (MLA_NOPE + MLA_ROPE) ** -0.5
MLA_QK_PAD = 256
NA_HEADS = 16
NA_DH = 64
NA_SCALE = NA_DH ** -0.5
WIN_ROWS = 8
WIN_COLS = 16
N_EXPERTS = 32
TOP_K = 4
D_FF = 1024
SWIGLU_ALPHA = 1.702
SWIGLU_LIMIT = 7.0
EXPERT_BLOCK = 512
SEG_ALIGN = 8
SORT_ROWS = -(-(TILE * TOP_K + N_EXPERTS * (SEG_ALIGN - 1)) // 256) * 256
MASK_VALUE = -1e30
LOG2_E = math.log2(math.e)
META_LANES = 128
VMEM_LIMIT = 60 * 1024 * 1024

_NT = (((1,), (1,)), ((), ()))


def _dot(a, b):
    return jnp.dot(a, b, preferred_element_type=F32)


def _dot_nt(a, b):
    return lax.dot_general(a, b, _NT, preferred_element_type=F32)


def _dot_f32(a, b):
    return jnp.dot(a, b, preferred_element_type=F32, precision=lax.Precision.HIGHEST)


def _rms(x, g):
    return x * lax.rsqrt(jnp.mean(x * x, axis=-1, keepdims=True) + EPS) * g


def _mod(mod_ref, k):
    return mod_ref[0, :, k * D:(k + 1) * D]


def _params(sem):
    return pltpu.CompilerParams(dimension_semantics=sem, vmem_limit_bytes=VMEM_LIMIT)


def _ada_kernel(c_ref, w_ref, b_ref, o_ref):
    c = c_ref[...]
    s = c * jax.nn.sigmoid(c)
    o_ref[0] = _dot_f32(s, w_ref[0]) + b_ref[0]


def _ada(cc, ada_w, ada_b):
    depth = ada_w.shape[0]
    n = cc.shape[0]
    nt = 6 * D // D
    return pl.pallas_call(
        _ada_kernel,
        grid=(depth, nt),
        in_specs=[pl.BlockSpec((n, D), lambda l, j: (0, 0)),
                  pl.BlockSpec((1, D, D), lambda l, j: (l, 0, j)),
                  pl.BlockSpec((1, 1, D), lambda l, j: (l, 0, j))],
        out_specs=pl.BlockSpec((1, n, D), lambda l, j: (l, 0, j)),
        out_shape=jax.ShapeDtypeStruct((depth, n, 6 * D), F32),
        compiler_params=_params(("arbitrary", "arbitrary")),
        name="ada",
    )(cc, ada_w, ada_b.reshape(depth, 1, 6 * D))


def _x_lat_map(i):
    return ((i // SMP_TILES) * LAT_TILES + jnp.minimum(i % SMP_TILES, LAT_TILES - 1), 0)


def _x_ctx_map(i):
    return (i // SMP_TILES, 0)


def _mod_map_all(nb):
    return lambda i: (jnp.where(i % SMP_TILES == LAT_TILES, nb, i // SMP_TILES), 0, 0)


def _lat_of_all(i):
    return (i // LAT_TILES) * SMP_TILES + i % LAT_TILES


def _rope(x, cos, sa, sb):
    return x * cos + pltpu.roll(x, 112, 1) * sa + pltpu.roll(x, 16, 1) * sb


def _proj0_kernel(x_ref, c_ref, mod_ref, g_ref, wa_ref, qng_ref, wqb_ref, kvng_ref, wkvb_ref,
                  cos_ref, sa_ref, sb_ref, qd_ref, kd_ref, vd_ref, qm_ref, km_ref, vm_ref):
    is_ctx = pl.program_id(0) % SMP_TILES == LAT_TILES
    x = jnp.where(is_ctx, c_ref[...], x_ref[...])
    h = (_rms(x, g_ref[...]) * (1.0 + _mod(mod_ref, 1)) + _mod(mod_ref, 0)).astype(BF16)
    big = _dot(h, wa_ref[...])
    cos, sa, sb = cos_ref[...], sa_ref[...], sb_ref[...]
    nd = DIFF_HEADS * 2 * DIFF_DH
    for j in range(nd // 128):
        sl = slice(j * 128, (j + 1) * 128)
        qd_ref[:, sl] = _rope(big[:, j * 128:(j + 1) * 128], cos, sa, sb).astype(BF16)
        kd_ref[:, sl] = _rope(big[:, nd + j * 128:nd + (j + 1) * 128], cos, sa, sb).astype(BF16)
    vd_ref[...] = big[:, 2 * nd:3 * nd].astype(BF16)
    o = 3 * nd
    cq = big[:, o:o + MLA_Q_LORA]
    ckv = big[:, o + MLA_Q_LORA:o + MLA_Q_LORA + MLA_KV_LORA]
    kpe = _rope(big[:, o + MLA_Q_LORA + MLA_KV_LORA:], cos, sa, sb).astype(BF16)
    qm = _dot(_rms(cq, qng_ref[...]).astype(BF16), wqb_ref[...])
    kv = _dot(_rms(ckv, kvng_ref[...]).astype(BF16), wkvb_ref[...])
    for hd in range(MLA_HEADS):
        b0 = hd * MLA_QK_PAD
        qm_ref[:, b0:b0 + 128] = qm[:, b0:b0 + 128].astype(BF16)
        qm_ref[:, b0 + 128:b0 + 256] = _rope(qm[:, b0 + 128:b0 + 256], cos, sa, sb).astype(BF16)
        km_ref[:, b0:b0 + 128] = kv[:, hd * 256:hd * 256 + 128].astype(BF16)
        km_ref[:, b0 + 128:b0 + 256] = kpe
        vm_ref[:, hd * 128:(hd + 1) * 128] = kv[:, hd * 256 + 128:(hd + 1) * 256].astype(BF16)


def _rope_tables():
    t = np.arange(SEQ)
    half = 16
    inv = (ROPE_BASE ** (-np.arange(half, dtype=np.float32) * 2.0 / (2 * half))).astype(np.float32)
    ang_r = (t // GRID_W).astype(np.float32)[:, None] * inv[None, :]
    ang_c = (t % GRID_W).astype(np.float32)[:, None] * inv[None, :]
    cr, sr, cc, sc = np.cos(ang_r), np.sin(ang_r), np.cos(ang_c), np.sin(ang_c)
    z = np.zeros_like(cr)
    cos = np.concatenate([cr, cr, cc, cc], axis=1)
    sa = np.concatenate([-sr, z, -sc, z], axis=1)
    sb = np.concatenate([z, sr, z, sc], axis=1)

    def full(tab, fill):
        tab = np.tile(tab, (1, 2))
        ident = np.full((CTX, 128), fill, np.float32)
        return jnp.asarray(np.concatenate([tab, ident], axis=0).astype(np.float32))

    return full(cos, 1.0), full(sa, 0.0), full(sb, 0.0)


def _proj0(x2, c2, mods, g, wa, qng, wqb, kvng, wkvb, nb):
    n_tiles = nb * SMP_TILES
    rows = nb * SMP_ROWS
    cos, sa, sb = _rope_tables()
    tile_map = lambda i: (i, 0)
    const = lambda i: (0, 0)
    tab_spec = pl.BlockSpec((TILE, 128), lambda i: (i % SMP_TILES, 0))
    widths = (512, 512, 512, 1024, 1024, 512)
    return pl.pallas_call(
        _proj0_kernel,
        grid=(n_tiles,),
        in_specs=[pl.BlockSpec((TILE, D), _x_lat_map),
                  pl.BlockSpec((TILE, D), _x_ctx_map),
                  pl.BlockSpec((1, 1, 6 * D), _mod_map_all(nb)),
                  pl.BlockSpec((1, D), const),
                  pl.BlockSpec(wa.shape, const),
                  pl.BlockSpec((1, MLA_Q_LORA), const),
                  pl.BlockSpec(wqb.shape, const),
                  pl.BlockSpec((1, MLA_KV_LORA), const),
                  pl.BlockSpec(wkvb.shape, const),
                  tab_spec, tab_spec, tab_spec],
        out_specs=[pl.BlockSpec((TILE, w), tile_map) for w in widths],
        out_shape=[jax.ShapeDtypeStruct((rows, w), BF16) for w in widths],
        compiler_params=_params(("arbitrary",)),
        name="proj_even",
    )(x2, c2, mods, g, wa, qng, wqb, kvng, wkvb, cos, sa, sb)


def _softmax_parts(s, scale):
    c = scale * LOG2_E
    m = jnp.max(s, axis=-1, keepdims=True)
    p = jnp.exp2(s * c - m * c)
    return p, jnp.sum(p, axis=-1, keepdims=True)


def _attn0_kernel(lam_init, qd_ref, kd_ref, vd_ref, qm_ref, km_ref, vm_ref, lam_ref, sg_ref, o_ref):
    lv = lam_ref[...]
    lam = (jnp.exp(jnp.sum(lv[0:1] * lv[1:2], axis=1, keepdims=True))
           - jnp.exp(jnp.sum(lv[2:3] * lv[3:4], axis=1, keepdims=True)) + lam_init)
    lane = lax.broadcasted_iota(jnp.int32, (TILE, 128), 1)

    def heads(k_lo, nk):
        def diff_scores(hd):
            sl = slice(hd * 128, (hd + 1) * 128)
            q = qd_ref[:, sl]
            zero = jnp.zeros_like(q)
            qq = jnp.concatenate([jnp.where(lane < DIFF_DH, q, zero),
                                  jnp.where(lane >= DIFF_DH, q, zero)], axis=0)
            return _dot_nt(qq, kd_ref[k_lo:k_lo + nk, sl])

        def diff_finish(hd, s):
            sl = slice(hd * 128, (hd + 1) * 128)
            p, l = _softmax_parts(s, DIFF_SCALE)
            pv = _dot(p.astype(BF16), vd_ref[k_lo:k_lo + nk, sl])
            o = pv[:TILE] / l[:TILE] - lam * (pv[TILE:] / l[TILE:])
            o_ref[:, sl] = (_rms(o, sg_ref[...]) * (1.0 - lam_init)).astype(BF16)

        def mla_scores(hd):
            ql = slice(hd * MLA_QK_PAD, (hd + 1) * MLA_QK_PAD)
            return _dot_nt(qm_ref[:, ql], km_ref[k_lo:k_lo + nk, ql])

        def mla_finish(hd, s):
            p, l = _softmax_parts(s, MLA_SCALE)
            pv = _dot(p.astype(BF16), vm_ref[k_lo:k_lo + nk, hd * 128:(hd + 1) * 128])
            o_ref[:, 512 + hd * 128:512 + (hd + 1) * 128] = (pv / l).astype(BF16)

        jobs = ([(diff_scores, diff_finish, hd) for hd in range(DIFF_HEADS)]
                + [(mla_scores, mla_finish, hd) for hd in range(MLA_HEADS)])
        s_next = jobs[0][0](jobs[0][2])
        for n, (_, finish, hd) in enumerate(jobs):
            s = s_next
            if n + 1 < len(jobs):
                s_next = jobs[n + 1][0](jobs[n + 1][2])
            finish(hd, s)

    is_ctx = pl.program_id(1) == LAT_TILES

    @pl.when(jnp.logical_not(is_ctx))
    def _():
        heads(0, SMP_ROWS)

    @pl.when(is_ctx)
    def _():
        heads(SEQ, CTX)


def _attn0(qd, kd, vd, qm, km, vm, lam_vec, subln_g, lam_init, nb):
    q_map = lambda b, p: (b * SMP_TILES + p, 0)
    k_map = lambda b, p: (b, 0)
    const = lambda b, p: (0, 0)
    return pl.pallas_call(
        functools.partial(_attn0_kernel, lam_init),
        grid=(nb, SMP_TILES),
        in_specs=[pl.BlockSpec((TILE, 512), q_map),
                  pl.BlockSpec((SMP_ROWS, 512), k_map),
                  pl.BlockSpec((SMP_ROWS, 512), k_map),
                  pl.BlockSpec((TILE, 1024), q_map),
                  pl.BlockSpec((SMP_ROWS, 1024), k_map),
                  pl.BlockSpec((SMP_ROWS, 512), k_map),
                  pl.BlockSpec(lam_vec.shape, const),
                  pl.BlockSpec((1, 128), const)],
        out_specs=pl.BlockSpec((TILE, D), q_map),
        out_shape=jax.ShapeDtypeStruct((nb * SMP_ROWS, D), BF16),
        compiler_params=_params(("arbitrary", "arbitrary")),
        name="attn_even",
    )(qd, kd, vd, qm, km, vm, lam_vec, subln_g)


def _oproj_kernel(dual, *refs):
    if dual:
        (o_ref, xl_ref, xc_ref, mod_ref, wo_ref, pg_ref, fg_ref, rw_ref, rb_ref,
         xn_ref, hf_ref, mi_ref, mw_ref, seg_ref) = refs
        is_ctx = pl.program_id(0) % SMP_TILES == LAT_TILES
        x = jnp.where(is_ctx, xc_ref[...], xl_ref[...])
    else:
        (o_ref, x_ref, mod_ref, wo_ref, pg_ref, fg_ref, rw_ref, rb_ref,
         xn_ref, hf_ref, mi_ref, mw_ref, seg_ref) = refs
        x = x_ref[...]

    y = _dot(o_ref[...], wo_ref[...])
    xn = x + _mod(mod_ref, 2) * _rms(y, pg_ref[...])
    xn_ref[...] = xn
    hf = _rms(xn, fg_ref[...]) * (1.0 + _mod(mod_ref, 4)) + _mod(mod_ref, 3)
    hi = hf.astype(BF16)
    hf_ref[...] = hi

    lo = (hf - hi.astype(F32)).astype(BF16)
    lt = _dot_nt(rw_ref[...], jnp.concatenate([hi, lo], axis=0))
    ne = N_EXPERTS
    logits = (lt[:ne, :TILE] + lt[ne:, :TILE]) + (lt[:ne, TILE:] + lt[ne:, TILE:]) + rb_ref[...]

    eid = lax.broadcasted_iota(jnp.int32, (ne, TILE), 0).astype(F32)
    sels, vals = [], []
    for _ in range(TOP_K):
        m = jnp.max(logits, axis=0, keepdims=True)
        idx = jnp.min(jnp.where(logits == m, eid, float(ne)), axis=0, keepdims=True)
        sel = eid == idx
        sels.append(sel)
        vals.append(m)
        logits = jnp.where(sel, -jnp.inf, logits)
    ex = [jnp.exp(v - vals[0]) for v in vals]
    den = ex[0] + ex[1] + ex[2] + ex[3]

    onehot = (sels[0] | sels[1] | sels[2] | sels[3]).astype(F32)
    r_i = lax.broadcasted_iota(jnp.int32, (TILE, TILE), 0)
    c_i = lax.broadcasted_iota(jnp.int32, (TILE, TILE), 1)
    before = _dot(onehot.astype(BF16), (r_i < c_i).astype(BF16))
    cnt = jnp.sum(onehot, axis=1, keepdims=True)
    seg_len = jnp.floor((cnt + (SEG_ALIGN - 1)) * (1.0 / SEG_ALIGN)) * SEG_ALIGN
    e_r = lax.broadcasted_iota(jnp.int32, (ne, ne), 0)
    e_c = lax.broadcasted_iota(jnp.int32, (ne, ne), 1)
    seg_len_b = jnp.broadcast_to(seg_len, (ne, META_LANES))
    seg_off = _dot_f32((e_c < e_r).astype(F32), seg_len_b)[:, 0:1]
    pos = before + seg_off
    sub = lax.broadcasted_iota(jnp.int32, (8, TILE), 0)
    mi = jnp.zeros((8, TILE), jnp.int32)
    mw = jnp.zeros((8, TILE), F32)
    for k in range(TOP_K):
        row = jnp.sum(jnp.where(sels[k], pos, 0.0), axis=0, keepdims=True).astype(jnp.int32)
        mi = jnp.where(sub == k, row, mi)
        mw = jnp.where(sub == k, ex[k] / den, mw)
    mi_ref[0] = mi
    mw_ref[0] = mw
    seg_ref[0] = seg_len_b.astype(jnp.int32)


def _oproj(o, x_args, mods, wo, pg, fg, rw, rb, nb, all_tokens):
    const = lambda i: (0, 0)
    tile_map = lambda i: (i, 0)
    tile3_map = lambda i: (i, 0, 0)
    rw_hi = rw.astype(BF16)
    rw_lo = (rw - rw_hi.astype(F32)).astype(BF16)
    rw_split = jnp.concatenate([rw_hi.T, rw_lo.T], axis=0)
    if all_tokens:
        n_tiles = nb * SMP_TILES
        x_specs = [pl.BlockSpec((TILE, D), _x_lat_map), pl.BlockSpec((TILE, D), _x_ctx_map)]
        mod_map = _mod_map_all(nb)
    else:
        n_tiles = nb * LAT_TILES
        x_specs = [pl.BlockSpec((TILE, D), lambda i: (_lat_of_all(i), 0))]
        mod_map = lambda i: (i // LAT_TILES, 0, 0)
    rows = n_tiles * TILE
    return pl.pallas_call(
        functools.partial(_oproj_kernel, all_tokens),
        grid=(n_tiles,),
        in_specs=[pl.BlockSpec((TILE, D), tile_map)] + x_specs + [
            pl.BlockSpec((1, 1, 6 * D), mod_map),
            pl.BlockSpec((D, D), const),
            pl.BlockSpec((1, D), const),
            pl.BlockSpec((1, D), const),
            pl.BlockSpec((2 * N_EXPERTS, D), const),
            pl.BlockSpec((N_EXPERTS, 1), const)],
        out_specs=[pl.BlockSpec((TILE, D), tile_map),
                   pl.BlockSpec((TILE, D), tile_map),
                   pl.BlockSpec((1, 8, TILE), tile3_map),
                   pl.BlockSpec((1, 8, TILE), tile3_map),
                   pl.BlockSpec((1, N_EXPERTS, META_LANES), tile3_map)],
        out_shape=[jax.ShapeDtypeStruct((rows, D), F32),
                   jax.ShapeDtypeStruct((rows, D), BF16),
                   jax.ShapeDtypeStruct((n_tiles, 8, TILE), jnp.int32),
                   jax.ShapeDtypeStruct((n_tiles, 8, TILE), F32),
                   jax.ShapeDtypeStruct((n_tiles, N_EXPERTS, META_LANES), jnp.int32)],
        compiler_params=_params(("arbitrary",)),
        name="oproj_router",
    )(o, *x_args, mods, wo, pg, fg, rw_split, rb.reshape(N_EXPERTS, 1))


U32 = jnp.uint32
HALF = D // 2


def _pack_rows(x):
    bits = pltpu.bitcast(x.astype(BF16).astype(F32), U32)
    return (bits[:, :HALF] >> 16) | (bits[:, HALF:] & jnp.uint32(0xFFFF0000))


def _unpack_rows(w):
    lo = pltpu.bitcast(w << 16, F32).astype(BF16)
    hi = pltpu.bitcast(w & jnp.uint32(0xFFFF0000), F32).astype(BF16)
    return jnp.concatenate([lo, hi], axis=1)


def _for_pieces(n, largest, fn):
    piece = largest
    while piece >= SEG_ALIGN:
        done = (n // (2 * piece)) * (2 * piece)
        pl.when((n & piece) != 0)(functools.partial(fn, done, piece))
        piece //= 2


def _segment_copies(seg_ref, local_ref, hbm_ref, sem, to_hbm, action):
    def per_expert(e, carry):
        def piece_copy(done, rows):
            loc = local_ref.at[pl.ds(pl.multiple_of(seg_ref[0, 1, e] + done, SEG_ALIGN), rows)]
            glb = hbm_ref.at[pl.ds(pl.multiple_of(seg_ref[0, 2, e] + done, SEG_ALIGN), rows)]
            action(pltpu.make_async_copy(loc, glb, sem) if to_hbm else pltpu.make_async_copy(glb, loc, sem))

        _for_pieces(seg_ref[0, 0, e], TILE, piece_copy)
        return carry

    lax.fori_loop(0, N_EXPERTS, per_expert, 0)


def _zero_fill_copies(tail_ref, free_ref, zbuf, xs_ref, sem, action):
    def per_expert(e, carry):
        def piece_copy(done, rows):
            dst = xs_ref.at[pl.ds(pl.multiple_of(tail_ref[0, e] + done, SEG_ALIGN), rows)]
            action(pltpu.make_async_copy(zbuf.at[pl.ds(0, rows)], dst, sem))

        _for_pieces(tail_ref[1, e], EXPERT_BLOCK // 2, piece_copy)
        return carry

    def per_block(j, carry):
        first = pl.multiple_of(free_ref[0] + j * EXPERT_BLOCK, EXPERT_BLOCK)
        action(pltpu.make_async_copy(zbuf, xs_ref.at[pl.ds(first, EXPERT_BLOCK)], sem))
        return carry

    lax.fori_loop(0, N_EXPERTS, per_expert, 0)
    lax.fori_loop(0, free_ref[1], per_block, 0)


def _dispatch_kernel(seg_ref, seg_prev_ref, tail_ref, free_ref, lp_ref, h_ref, xs_ref, sbuf, zbuf, sem):
    i = pl.program_id(0)
    last = pl.num_programs(0) - 1
    slot = i % 2
    row = lax.broadcasted_iota(jnp.int32, (SORT_ROWS, TILE), 0)
    lp = lp_ref[0]
    hit = row == lp[0:1]
    for k in range(1, TOP_K):
        hit = hit | (row == lp[k:k + 1])
    sbuf[slot] = _pack_rows(_dot(jnp.where(hit, 1.0, 0.0).astype(BF16), h_ref[...]))
    _segment_copies(seg_ref, sbuf.at[slot], xs_ref, sem.at[slot], True, lambda cp: cp.start())

    @pl.when(i > 0)
    def _():
        _segment_copies(seg_prev_ref, sbuf.at[1 - slot], xs_ref, sem.at[1 - slot], True, lambda cp: cp.wait())

    @pl.when(i == last)
    def _():
        _segment_copies(seg_ref, sbuf.at[slot], xs_ref, sem.at[slot], True, lambda cp: cp.wait())
        zbuf[...] = jnp.zeros_like(zbuf)
        _zero_fill_copies(tail_ref, free_ref, zbuf, xs_ref, sem.at[slot], lambda cp: cp.start())
        _zero_fill_copies(tail_ref, free_ref, zbuf, xs_ref, sem.at[slot], lambda cp: cp.wait())


def _dispatch(segs, tails, free, lpos_t, hf, m_rows):
    n_tiles = hf.shape[0] // TILE
    seg_block = (1, 3, N_EXPERTS)
    return pl.pallas_call(
        _dispatch_kernel,
        grid=(n_tiles,),
        in_specs=[pl.BlockSpec(seg_block, lambda i: (i, 0, 0), memory_space=pltpu.SMEM),
                  pl.BlockSpec(seg_block, lambda i: (jnp.maximum(i - 1, 0), 0, 0), memory_space=pltpu.SMEM),
                  pl.BlockSpec(memory_space=pltpu.SMEM),
                  pl.BlockSpec(memory_space=pltpu.SMEM),
                  pl.BlockSpec((1, 8, TILE), lambda i: (i, 0, 0)),
                  pl.BlockSpec((TILE, D), lambda i: (i, 0))],
        out_specs=pl.BlockSpec(memory_space=pl.ANY),
        out_shape=jax.ShapeDtypeStruct((m_rows, HALF), U32),
        scratch_shapes=[pltpu.VMEM((2, SORT_ROWS, HALF), U32), pltpu.VMEM((EXPERT_BLOCK, HALF), U32),
                        pltpu.SemaphoreType.DMA((2,))],
        compiler_params=_params(("arbitrary",)),
        name="moe_dispatch",
    )(segs, segs, tails, free, lpos_t, hf)


def _expert_kernel(be_ref, nu_ref, xs_ref, w1_ref, b1_ref, w2_ref, b2_ref, ys_ref, w1b, w2b):
    i = pl.program_id(0)
    used = i < nu_ref[0]
    new_expert = jnp.logical_or(i == 0, be_ref[i] != be_ref[jnp.maximum(i - 1, 0)])

    @pl.when(jnp.logical_and(used, new_expert))
    def _():
        w1b[...] = w1_ref[0, 0].astype(BF16)
        w2b[...] = w2_ref[0, 0].astype(BF16)

    @pl.when(used)
    def _():
        gu = _dot(_unpack_rows(xs_ref[...]), w1b[...]) + b1_ref[0, 0]
        gate = jnp.minimum(gu[:, :D_FF], SWIGLU_LIMIT)
        lin = jnp.clip(gu[:, D_FF:], -SWIGLU_LIMIT, SWIGLU_LIMIT)
        act = gate * jax.nn.sigmoid(SWIGLU_ALPHA * gate) * (lin + 1.0)
        ys_ref[...] = _pack_rows(_dot(act.astype(BF16), w2b[...]) + b2_ref[0, 0])

    @pl.when(i >= nu_ref[0])
    def _():
        ys_ref[...] = jnp.zeros_like(ys_ref)


def _experts(blk_e, n_used, xs, layer, w1, b1, w2, b2):
    n_blocks = xs.shape[0] // EXPERT_BLOCK
    in_row_map = lambda i, be, nu: (jnp.minimum(i, nu[0] - 1), 0)
    e_map = lambda i, be, nu: (layer, be[i], 0, 0)
    grid_spec = pltpu.PrefetchScalarGridSpec(
        num_scalar_prefetch=2,
        grid=(n_blocks,),
        in_specs=[pl.BlockSpec((EXPERT_BLOCK, HALF), in_row_map),
                  pl.BlockSpec((1, 1, D, 2 * D_FF), e_map),
                  pl.BlockSpec((1, 1, 1, 2 * D_FF), e_map),
                  pl.BlockSpec((1, 1, D_FF, D), e_map),
                  pl.BlockSpec((1, 1, 1, D), e_map)],
        out_specs=pl.BlockSpec((EXPERT_BLOCK, HALF), lambda i, be, nu: (i, 0)),
        scratch_shapes=[pltpu.VMEM((D, 2 * D_FF), BF16), pltpu.VMEM((D_FF, D), BF16)])
    return pl.pallas_call(
        _expert_kernel,
        grid_spec=grid_spec,
        out_shape=jax.ShapeDtypeStruct(xs.shape, U32),
        compiler_params=_params(("arbitrary",)),
        name="moe_experts",
    )(blk_e, n_used, xs, w1, b1[:, :, None, :], w2, b2[:, :, None, :])


def _combine_kernel(seg_ref, seg_next_ref, mi_ref, mw_ref, x_ref, mod_ref, g_ref, ys_ref, o_ref, ybuf, sem):
    i = pl.program_id(0)
    slot = i % 2

    @pl.when(i == 0)
    def _():
        ybuf[...] = jnp.zeros_like(ybuf)
        _segment_copies(seg_ref, ybuf.at[0], ys_ref, sem.at[0], False, lambda cp: cp.start())

    @pl.when(i + 1 < pl.num_programs(0))
    def _():
        _segment_copies(seg_next_ref, ybuf.at[1 - slot], ys_ref, sem.at[1 - slot], False, lambda cp: cp.start())

    col = lax.broadcasted_iota(jnp.int32, (TILE, SORT_ROWS), 1)
    mi = mi_ref[...]
    mw = mw_ref[...]
    wm = jnp.zeros((TILE, SORT_ROWS), F32)
    for k in range(TOP_K):
        wm = jnp.where(col == mi[:, k:k + 1], mw[:, k:k + 1], wm)
    _segment_copies(seg_ref, ybuf.at[slot], ys_ref, sem.at[slot], False, lambda cp: cp.wait())
    y = _dot(wm.astype(BF16), _unpack_rows(ybuf[slot]))
    o_ref[...] = x_ref[...] + _mod(mod_ref, 5) * _rms(y, g_ref[...])


def _combine(segs, mi, mw, xn, mods, g, ys, nb, all_tokens):
    n_tiles = xn.shape[0] // TILE
    tile_map = lambda i: (i, 0)
    mod_map = _mod_map_all(nb) if all_tokens else (lambda i: (i // LAT_TILES, 0, 0))
    seg_block = (1, 3, N_EXPERTS)
    return pl.pallas_call(
        _combine_kernel,
        grid=(n_tiles,),
        in_specs=[pl.BlockSpec(seg_block, lambda i: (i, 0, 0), memory_space=pltpu.SMEM),
                  pl.BlockSpec(seg_block, lambda i: (jnp.minimum(i + 1, n_tiles - 1), 0, 0),
                               memory_space=pltpu.SMEM),
                  pl.BlockSpec((TILE, 8), tile_map),
                  pl.BlockSpec((TILE, 8), tile_map),
                  pl.BlockSpec((TILE, D), tile_map),
                  pl.BlockSpec((1, 1, 6 * D), mod_map),
                  pl.BlockSpec((1, D), lambda i: (0, 0)),
                  pl.BlockSpec(memory_space=pl.ANY)],
        out_specs=pl.BlockSpec((TILE, D), tile_map),
        out_shape=jax.ShapeDtypeStruct(xn.shape, F32),
        scratch_shapes=[pltpu.VMEM((2, SORT_ROWS, HALF), U32), pltpu.SemaphoreType.DMA((2,))],
        compiler_params=_params(("arbitrary",)),
        name="moe_combine",
    )(segs, segs, mi, mw, xn, mods, g, ys)


def _moe(hf, mi, mw, seg, xn, mods, g, layer, w1, b1, w2, b2, nb, all_tokens):
    t = hf.shape[0]
    n_tiles = t // TILE
    rows_max = t * TOP_K + n_tiles * N_EXPERTS * (SEG_ALIGN - 1)
    n_blocks = -(-rows_max // EXPERT_BLOCK) + N_EXPERTS
    seg_len = seg[:, :, 0]
    counts = jnp.sum(seg_len, axis=0)
    padded = (counts + EXPERT_BLOCK - 1) // EXPERT_BLOCK * EXPERT_BLOCK
    pad_end = jnp.cumsum(padded)
    pad_start = pad_end - padded
    seg_first = pad_start[None, :] + jnp.cumsum(seg_len, axis=0) - seg_len
    seg_local = jnp.cumsum(seg_len, axis=1) - seg_len
    segs = jnp.stack([seg_len, seg_local, seg_first], axis=1).astype(jnp.int32)
    by_token = lambda a: a.transpose(0, 2, 1).reshape(t, 8)
    blk_row = jnp.arange(n_blocks, dtype=jnp.int32) * EXPERT_BLOCK
    blk_e = jnp.minimum(jnp.sum(pad_end[None, :] <= blk_row[:, None], axis=1), N_EXPERTS - 1).astype(jnp.int32)
    n_used = (pad_end[-1:] // EXPERT_BLOCK).astype(jnp.int32)
    tails = jnp.stack([pad_start + counts, padded - counts]).astype(jnp.int32)
    free = jnp.concatenate([pad_end[-1:], n_blocks - n_used]).astype(jnp.int32)
    xs = _dispatch(segs, tails, free, mi, hf, n_blocks * EXPERT_BLOCK)
    ys = _experts(blk_e, n_used, xs, layer, w1, b1, w2, b2)
    return _combine(segs, by_token(mi), by_token(mw), xn, mods, g, ys, nb, all_tokens)


def _proj1_kernel(x_ref, mod_ref, g_ref, w_ref, q_ref, k_ref, v_ref):
    h = (_rms(x_ref[...], g_ref[...]) * (1.0 + _mod(mod_ref, 1)) + _mod(mod_ref, 0)).astype(BF16)
    qkv = _dot(h, w_ref[...])
    q_ref[...] = (qkv[:, :D] * NA_SCALE).astype(BF16)
    k_ref[...] = qkv[:, D:2 * D].astype(BF16)
    v_ref[...] = qkv[:, 2 * D:].astype(BF16)


def _proj1(x1, mods, g, w, nb):
    n_tiles = nb * SMP_TILES
    tile_map = lambda i: (i, 0)
    return pl.pallas_call(
        _proj1_kernel,
        grid=(n_tiles,),
        in_specs=[pl.BlockSpec((TILE, D), tile_map),
                  pl.BlockSpec((1, 1, 6 * D), _mod_map_all(nb)),
                  pl.BlockSpec((1, D), lambda i: (0, 0)),
                  pl.BlockSpec((D, 3 * D), lambda i: (0, 0))],
        out_specs=[pl.BlockSpec((TILE, D), tile_map)] * 3,
        out_shape=[jax.ShapeDtypeStruct((n_tiles * TILE, D), BF16)] * 3,
        compiler_params=_params(("arbitrary",)),
        name="proj_odd",
    )(x1, mods, g, w)


def _na_row_start(r):
    return jnp.clip(r - WIN_ROWS // 2, 0, GRID_H - WIN_ROWS)


def _na_kernel(q_ref, k_ref, v_ref, bias_ref, o_ref):
    r = pl.program_id(1)
    k0 = pl.multiple_of(_na_row_start(r) * GRID_W, GRID_W)
    n_loc = WIN_ROWS * GRID_W
    lane = lax.broadcasted_iota(jnp.int32, (GRID_W, 128), 1)
    pairs = range(NA_HEADS // 2)
    sls = [slice(j * 128, (j + 1) * 128) for j in pairs]
    qqs = []
    for j in pairs:
        q = q_ref[:, sls[j]]
        zero = jnp.zeros_like(q)
        qqs.append(jnp.concatenate([jnp.where(lane < NA_DH, q, zero),
                                    jnp.where(lane >= NA_DH, q, zero)], axis=0))
    s_loc = jnp.concatenate([_dot_nt(qqs[j], k_ref[pl.ds(k0, n_loc), sls[j]]) for j in pairs], axis=0)
    s_loc = s_loc + bias_ref[0]
    s_ctx = jnp.concatenate([_dot_nt(qqs[j], k_ref[SEQ:, sls[j]]) for j in pairs], axis=0)
    m = jnp.maximum(jnp.max(s_loc, axis=-1, keepdims=True), jnp.max(s_ctx, axis=-1, keepdims=True))
    p_loc = jnp.exp(s_loc - m)
    p_ctx = jnp.exp(s_ctx - m)
    inv_l = 1.0 / (jnp.sum(p_loc, axis=-1, keepdims=True) + jnp.sum(p_ctx, axis=-1, keepdims=True))
    p_loc = p_loc.astype(BF16)
    p_ctx = p_ctx.astype(BF16)
    for j in pairs:
        rows = slice(j * 2 * GRID_W, (j + 1) * 2 * GRID_W)
        pv = (_dot(p_loc[rows], v_ref[pl.ds(k0, n_loc), sls[j]])
              + _dot(p_ctx[rows], v_ref[SEQ:, sls[j]])) * inv_l[rows]
        o_ref[:, sls[j]] = jnp.where(lane < NA_DH, pv[:GRID_W], pv[GRID_W:]).astype(BF16)


def _na_bias(rpb):
    mid = WIN_ROWS // 2
    pat_rows = list(range(mid)) + [mid] + list(range(GRID_H - mid + 1, GRID_H))
    r = np.array(pat_rows)
    rs = np.clip(r - mid, 0, GRID_H - WIN_ROWS)
    row_off = rs[:, None] + np.arange(WIN_ROWS)[None, :] - r[:, None] + WIN_ROWS - 1
    c = np.arange(GRID_W)
    q_start = np.clip(c - WIN_COLS // 2, 0, GRID_W - WIN_COLS)[:, None]
    kc = np.arange(GRID_W)[None, :]
    valid = (kc >= q_start) & (kc < q_start + WIN_COLS)
    col_off = np.clip(kc - c[:, None] + WIN_COLS - 1, 0, 2 * WIN_COLS - 2)
    sel_r = np.eye(2 * WIN_ROWS - 1, dtype=np.float32)[row_off]
    sel_c = np.eye(2 * WIN_COLS - 1, dtype=np.float32)[col_off]
    b = jnp.einsum('pia,hab,ckb->hpick', sel_r, rpb.astype(F32), sel_c, precision=lax.Precision.HIGHEST)
    b = jnp.where(valid[None, None, None], b, MASK_VALUE)
    b = b.transpose(1, 0, 3, 2, 4).reshape(len(pat_rows), NA_HEADS * GRID_W, WIN_ROWS * GRID_W)
    return b


def _na_pattern(r):
    mid = WIN_ROWS // 2
    return jnp.where(r < mid, r, jnp.where(r <= GRID_H - mid, mid, r - (GRID_H - 2 * mid)))


def _na(q, k, v, bias, nb):
    blocks_per_smp = SMP_ROWS // GRID_W
    return pl.pallas_call(
        _na_kernel,
        grid=(nb, GRID_H),
        in_specs=[pl.BlockSpec((GRID_W, D), lambda b, r: (b * blocks_per_smp + r, 0)),
                  pl.BlockSpec((SMP_ROWS, D), lambda b, r: (b, 0)),
                  pl.BlockSpec((SMP_ROWS, D), lambda b, r: (b, 0)),
                  pl.BlockSpec((1,) + bias.shape[1:], lambda b, r: (_na_pattern(r), 0, 0))],
        out_specs=pl.BlockSpec((GRID_W, D), lambda b, r: (b * GRID_H + r, 0)),
        out_shape=jax.ShapeDtypeStruct((nb * SEQ, D), BF16),
        compiler_params=_params(("arbitrary", "arbitrary")),
        name="na_attn",
    )(q, k, v, bias)


def kernel(x, c, ctx, c_ctx, ada_w, ada_b, mix_pre_g, mix_post_g, ffn_pre_g, ffn_post_g, even_w_in,
           diff_lambda, diff_subln_g, mla_q_norm_g, mla_w_qb, mla_kv_norm_g, mla_w_kvb, even_w_out,
           na_w_qkv, na_rpb, na_w_out, router_w, router_b, moe_w1, moe_b1, moe_w2, moe_b2):
    nb = x.shape[0]
    assert x.shape[1:] == (SEQ, D) and ctx.shape[1:] == (CTX, D)
    x2 = x.reshape(nb * SEQ, D)
    c2 = ctx.reshape(nb * CTX, D)
    row = lambda a: a.reshape(1, -1)

    mods = _ada(jnp.concatenate([c, c_ctx[None, :]], axis=0), ada_w, ada_b)
    mods0 = mods[0].reshape(nb + 1, 1, 6 * D)
    mods1 = mods[1].reshape(nb + 1, 1, 6 * D)

    lam_init = 0.8 - 0.6 * math.exp(-0.3 * 0)
    w_in = even_w_in[0]
    wa = jnp.pad(w_in, ((0, 0), (0, 2048 - w_in.shape[1]))).astype(BF16)
    wqb = mla_w_qb[0].reshape(MLA_Q_LORA, MLA_HEADS, MLA_NOPE + MLA_ROPE)
    wqb = jnp.pad(wqb, ((0, 0), (0, 0), (0, MLA_QK_PAD - MLA_NOPE - MLA_ROPE)))
    wqb = wqb.reshape(MLA_Q_LORA, MLA_HEADS * MLA_QK_PAD).astype(BF16)
    qd, kd, vd, qm, km, vm = _proj0(x2, c2, mods0, row(mix_pre_g[0]), wa, row(mla_q_norm_g[0]), wqb,
                                    row(mla_kv_norm_g[0]), mla_w_kvb[0].astype(BF16), nb)
    o0 = _attn0(qd, kd, vd, qm, km, vm, diff_lambda[0], row(diff_subln_g[0]), lam_init, nb)
    xn, hf, mi, mw, seg = _oproj(o0, (x2, c2), mods0, even_w_out[0].astype(BF16), row(mix_post_g[0]),
                                 row(ffn_pre_g[0]), router_w[0], row(router_b[0]), nb, True)
    x1 = _moe(hf, mi, mw, seg, xn, mods0, row(ffn_post_g[0]), 0, moe_w1, moe_b1, moe_w2, moe_b2, nb, True)

    q, k, v = _proj1(x1, mods1, row(mix_pre_g[1]), na_w_qkv[0].astype(BF16), nb)
    o1 = _na(q, k, v, _na_bias(na_rpb[0]), nb)
    xn, hf, mi, mw, seg = _oproj(o1, (x1,), mods1, na_w_out[0].astype(BF16), row(mix_post_g[1]),
                                 row(ffn_pre_g[1]), router_w[1], row(router_b[1]), nb, False)
    out = _moe(hf, mi, mw, seg, xn, mods1, row(ffn_post_g[1]), 1, moe_w1, moe_b1, moe_w2, moe_b2, nb, False)
    return out.reshape(nb, SEQ, D)
```

```python
import functools
import math

import numpy as np
import jax
import jax.numpy as jnp
from jax import lax
from jax.experimental import pallas as pl
from jax.experimental.pallas import tpu as pltpu

F32 = jnp.float32
BF16 = jnp.bfloat16

D = 1024
SEQ = 2048
CTX = 256
GRID_W = 64
GRID_H = SEQ // GRID_W
TILE = 256
LAT_TILES = SEQ // TILE
SMP_TILES = LAT_TILES + 1
SMP_ROWS = SEQ + CTX
EPS = 1e-6
ROPE_BASE = 10000.0

DIFF_HEADS = 4
DIFF_DH = 64
DIFF_SCALE = DIFF_DH ** -0.5
MLA_HEADS = 4
MLA_Q_LORA = 256
MLA_KV_LORA = 128
MLA_NOPE = 128
MLA_ROPE = 64
MLA_V = 128
MLA_SCALE = (MLA_NOPE + MLA_ROPE) ** -0.5
MLA_QK_PAD = 256
NA_HEADS = 16
NA_DH = 64
NA_SCALE = NA_DH ** -0.5
WIN_ROWS = 8
WIN_COLS = 16
N_EXPERTS = 32
TOP_K = 4
D_FF = 1024
SWIGLU_ALPHA = 1.702
SWIGLU_LIMIT = 7.0
EXPERT_BLOCK = 512
SEG_ALIGN = 8
SORT_ROWS = -(-(TILE * TOP_K + N_EXPERTS * (SEG_ALIGN - 1)) // 256) * 256
SORT_ROWS_POW2 = 1 << (SORT_ROWS.bit_length() - 1)
MASK_VALUE = -1e30
LOG2_E = math.log2(math.e)
META_LANES = 128
VMEM_LIMIT = 60 * 1024 * 1024

_NT = (((1,), (1,)), ((), ()))


def _dot(a, b):
    return jnp.dot(a, b, preferred_element_type=F32)


def _dot_nt(a, b):
    return lax.dot_general(a, b, _NT, preferred_element_type=F32)


def _dot_f32(a, b):
    return jnp.dot(a, b, preferred_element_type=F32, precision=lax.Precision.HIGHEST)


def _rms(x, g):
    return x * lax.rsqrt(jnp.mean(x * x, axis=-1, keepdims=True) + EPS) * g


def _mod(mod_ref, k):
    return mod_ref[0, :, k * D:(k + 1) * D]


def _params(sem):
    return pltpu.CompilerParams(dimension_semantics=sem, vmem_limit_bytes=VMEM_LIMIT)


def _ada_kernel(c_ref, w_ref, b_ref, o_ref):
    c = c_ref[...]
    s = c * jax.nn.sigmoid(c)
    o_ref[0] = _dot_f32(s, w_ref[0]) + b_ref[0]


def _ada(cc, ada_w, ada_b):
    depth = ada_w.shape[0]
    n = cc.shape[0]
    nt = 6 * D // D
    return pl.pallas_call(
        _ada_kernel,
        grid=(depth, nt),
        in_specs=[pl.BlockSpec((n, D), lambda l, j: (0, 0)),
                  pl.BlockSpec((1, D, D), lambda l, j: (l, 0, j)),
                  pl.BlockSpec((1, 1, D), lambda l, j: (l, 0, j))],
        out_specs=pl.BlockSpec((1, n, D), lambda l, j: (l, 0, j)),
        out_shape=jax.ShapeDtypeStruct((depth, n, 6 * D), F32),
        compiler_params=_params(("arbitrary", "arbitrary")),
        name="ada",
    )(cc, ada_w, ada_b.reshape(depth, 1, 6 * D))


def _x_lat_map(i):
    return ((i // SMP_TILES) * LAT_TILES + jnp.minimum(i % SMP_TILES, LAT_TILES - 1), 0)


def _x_ctx_map(i):
    return (i // SMP_TILES, 0)


def _mod_map_all(nb):
    return lambda i: (jnp.where(i % SMP_TILES == LAT_TILES, nb, i // SMP_TILES), 0, 0)


def _lat_of_all(i):
    return (i // LAT_TILES) * SMP_TILES + i % LAT_TILES


def _rope(x, cos, sa, sb):
    return x * cos + pltpu.roll(x, 112, 1) * sa + pltpu.roll(x, 16, 1) * sb


def _proj0_kernel(x_ref, c_ref, mod_ref, g_ref, wa_ref, qng_ref, wqb_ref, kvng_ref, wkvb_ref,
                  cos_ref, sa_ref, sb_ref, qd_ref, kd_ref, vd_ref, qm_ref, km_ref, vm_ref):
    is_ctx = pl.program_id(0) % SMP_TILES == LAT_TILES
    x = jnp.where(is_ctx, c_ref[...], x_ref[...])
    h = (_rms(x, g_ref[...]) * (1.0 + _mod(mod_ref, 1)) + _mod(mod_ref, 0)).astype(BF16)
    big = _dot(h, wa_ref[...])
    cos, sa, sb = cos_ref[...], sa_ref[...], sb_ref[...]
    nd = DIFF_HEADS * 2 * DIFF_DH
    for j in range(nd // 128):
        sl = slice(j * 128, (j + 1) * 128)
        qd_ref[:, sl] = _rope(big[:, j * 128:(j + 1) * 128], cos, sa, sb).astype(BF16)
        kd_ref[:, sl] = _rope(big[:, nd + j * 128:nd + (j + 1) * 128], cos, sa, sb).astype(BF16)
    vd_ref[...] = big[:, 2 * nd:3 * nd].astype(BF16)
    o = 3 * nd
    cq = big[:, o:o + MLA_Q_LORA]
    ckv = big[:, o + MLA_Q_LORA:o + MLA_Q_LORA + MLA_KV_LORA]
    kpe = _rope(big[:, o + MLA_Q_LORA + MLA_KV_LORA:], cos, sa, sb).astype(BF16)
    qm = _dot(_rms(cq, qng_ref[...]).astype(BF16), wqb_ref[...])
    kv = _dot(_rms(ckv, kvng_ref[...]).astype(BF16), wkvb_ref[...])
    for hd in range(MLA_HEADS):
        b0 = hd * MLA_QK_PAD
        qm_ref[:, b0:b0 + 128] = qm[:, b0:b0 + 128].astype(BF16)
        qm_ref[:, b0 + 128:b0 + 256] = _rope(qm[:, b0 + 128:b0 + 256], cos, sa, sb).astype(BF16)
        km_ref[:, b0:b0 + 128] = kv[:, hd * 256:hd * 256 + 128].astype(BF16)
        km_ref[:, b0 + 128:b0 + 256] = kpe
        vm_ref[:, hd * 128:(hd + 1) * 128] = kv[:, hd * 256 + 128:(hd + 1) * 256].astype(BF16)


def _rope_tables():
    t = np.arange(SEQ)
    half = 16
    inv = (ROPE_BASE ** (-np.arange(half, dtype=np.float32) * 2.0 / (2 * half))).astype(np.float32)
    ang_r = (t // GRID_W).astype(np.float32)[:, None] * inv[None, :]
    ang_c = (t % GRID_W).astype(np.float32)[:, None] * inv[None, :]
    cr, sr, cc, sc = np.cos(ang_r), np.sin(ang_r), np.cos(ang_c), np.sin(ang_c)
    z = np.zeros_like(cr)
    cos = np.concatenate([cr, cr, cc, cc], axis=1)
    sa = np.concatenate([-sr, z, -sc, z], axis=1)
    sb = np.concatenate([z, sr, z, sc], axis=1)

    def full(tab, fill):
        tab = np.tile(tab, (1, 2))
        ident = np.full((CTX, 128), fill, np.float32)
        return jnp.asarray(np.concatenate([tab, ident], axis=0).astype(np.float32))

    return full(cos, 1.0), full(sa, 0.0), full(sb, 0.0)


def _proj0(x2, c2, mods, g, wa, qng, wqb, kvng, wkvb, nb):
    n_tiles = nb * SMP_TILES
    rows = nb * SMP_ROWS
    cos, sa, sb = _rope_tables()
    tile_map = lambda i: (i, 0)
    const = lambda i: (0, 0)
    tab_spec = pl.BlockSpec((TILE, 128), lambda i: (i % SMP_TILES, 0))
    widths = (512, 512, 512, 1024, 1024, 512)
    return pl.pallas_call(
        _proj0_kernel,
        grid=(n_tiles,),
        in_specs=[pl.BlockSpec((TILE, D), _x_lat_map),
                  pl.BlockSpec((TILE, D), _x_ctx_map),
                  pl.BlockSpec((1, 1, 6 * D), _mod_map_all(nb)),
                  pl.BlockSpec((1, D), const),
                  pl.BlockSpec(wa.shape, const),
                  pl.BlockSpec((1, MLA_Q_LORA), const),
                  pl.BlockSpec(wqb.shape, const),
                  pl.BlockSpec((1, MLA_KV_LORA), const),
                  pl.BlockSpec(wkvb.shape, const),
                  tab_spec, tab_spec, tab_spec],
        out_specs=[pl.BlockSpec((TILE, w), tile_map) for w in widths],
        out_shape=[jax.ShapeDtypeStruct((rows, w), BF16) for w in widths],
        compiler_params=_params(("arbitrary",)),
        name="proj_even",
    )(x2, c2, mods, g, wa, qng, wqb, kvng, wkvb, cos, sa, sb)


def _softmax_parts(s, scale):
    c = scale * LOG2_E
    m = jnp.max(s, axis=-1, keepdims=True)
    p = jnp.exp2(s * c - m * c)
    return p, jnp.sum(p, axis=-1, keepdims=True)


def _attn0_kernel(lam_init, qd_ref, kd_ref, vd_ref, qm_ref, km_ref, vm_ref, lam_ref, sg_ref, o_ref):
    lv = lam_ref[...]
    lam = (jnp.exp(jnp.sum(lv[0:1] * lv[1:2], axis=1, keepdims=True))
           - jnp.exp(jnp.sum(lv[2:3] * lv[3:4], axis=1, keepdims=True)) + lam_init)
    lane = lax.broadcasted_iota(jnp.int32, (TILE, 128), 1)

    def heads(k_lo, nk):
        def diff_scores(hd):
            sl = slice(hd * 128, (hd + 1) * 128)
            q = qd_ref[:, sl]
            zero = jnp.zeros_like(q)
            qq = jnp.concatenate([jnp.where(lane < DIFF_DH, q, zero),
                                  jnp.where(lane >= DIFF_DH, q, zero)], axis=0)
            return _dot_nt(qq, kd_ref[k_lo:k_lo + nk, sl])

        def diff_finish(hd, s):
            sl = slice(hd * 128, (hd + 1) * 128)
            p, l = _softmax_parts(s, DIFF_SCALE)
            pv = _dot(p.astype(BF16), vd_ref[k_lo:k_lo + nk, sl])
            o = pv[:TILE] / l[:TILE] - lam * (pv[TILE:] / l[TILE:])
            o_ref[:, sl] = (_rms(o, sg_ref[...]) * (1.0 - lam_init)).astype(BF16)

        def mla_scores(hd):
            ql = slice(hd * MLA_QK_PAD, (hd + 1) * MLA_QK_PAD)
            return _dot_nt(qm_ref[:, ql], km_ref[k_lo:k_lo + nk, ql])

        def mla_finish(hd, s):
            p, l = _softmax_parts(s, MLA_SCALE)
            pv = _dot(p.astype(BF16), vm_ref[k_lo:k_lo + nk, hd * 128:(hd + 1) * 128])
            o_ref[:, 512 + hd * 128:512 + (hd + 1) * 128] = (pv / l).astype(BF16)

        jobs = ([(diff_scores, diff_finish, hd) for hd in range(DIFF_HEADS)]
                + [(mla_scores, mla_finish, hd) for hd in range(MLA_HEADS)])
        s_next = jobs[0][0](jobs[0][2])
        for n, (_, finish, hd) in enumerate(jobs):
            s = s_next
            if n + 1 < len(jobs):
                s_next = jobs[n + 1][0](jobs[n + 1][2])
            finish(hd, s)

    is_ctx = pl.program_id(1) == LAT_TILES

    @pl.when(jnp.logical_not(is_ctx))
    def _():
        heads(0, SMP_ROWS)

    @pl.when(is_ctx)
    def _():
        heads(SEQ, CTX)


def _attn0(qd, kd, vd, qm, km, vm, lam_vec, subln_g, lam_init, nb):
    q_map = lambda b, p: (b * SMP_TILES + p, 0)
    k_map = lambda b, p: (b, 0)
    const = lambda b, p: (0, 0)
    return pl.pallas_call(
        functools.partial(_attn0_kernel, lam_init),
        grid=(nb, SMP_TILES),
        in_specs=[pl.BlockSpec((TILE, 512), q_map),
                  pl.BlockSpec((SMP_ROWS, 512), k_map),
                  pl.BlockSpec((SMP_ROWS, 512), k_map),
                  pl.BlockSpec((TILE, 1024), q_map),
                  pl.BlockSpec((SMP_ROWS, 1024), k_map),
                  pl.BlockSpec((SMP_ROWS, 512), k_map),
                  pl.BlockSpec(lam_vec.shape, const),
                  pl.BlockSpec((1, 128), const)],
        out_specs=pl.BlockSpec((TILE, D), q_map),
        out_shape=jax.ShapeDtypeStruct((nb * SMP_ROWS, D), BF16),
        compiler_params=_params(("arbitrary", "arbitrary")),
        name="attn_even",
    )(qd, kd, vd, qm, km, vm, lam_vec, subln_g)


def _oproj_kernel(dual, *refs):
    if dual:
        (o_ref, xl_ref, xc_ref, mod_ref, wo_ref, pg_ref, fg_ref, rw_ref, rb_ref,
         xn_ref, hf_ref, mi_ref, mw_ref, seg_ref) = refs
        is_ctx = pl.program_id(0) % SMP_TILES == LAT_TILES
        x = jnp.where(is_ctx, xc_ref[...], xl_ref[...])
    else:
        (o_ref, x_ref, mod_ref, wo_ref, pg_ref, fg_ref, rw_ref, rb_ref,
         xn_ref, hf_ref, mi_ref, mw_ref, seg_ref) = refs
        x = x_ref[...]

    his, los = [], []
    for rs in (slice(0, TILE // 2), slice(TILE // 2, TILE)):
        y = _dot(o_ref[rs, :], wo_ref[...])
        xn = x[rs] + _mod(mod_ref, 2) * _rms(y, pg_ref[...])
        xn_ref[rs, :] = xn
        hf = _rms(xn, fg_ref[...]) * (1.0 + _mod(mod_ref, 4)) + _mod(mod_ref, 3)
        hi = hf.astype(BF16)
        hf_ref[rs, :] = hi
        his.append(hi)
        los.append((hf - hi.astype(F32)).astype(BF16))

    lt = _dot_nt(rw_ref[...], jnp.concatenate(his + los, axis=0))
    ne = N_EXPERTS
    logits = (lt[:ne, :TILE] + lt[ne:, :TILE]) + (lt[:ne, TILE:] + lt[ne:, TILE:]) + rb_ref[...]

    eid = lax.broadcasted_iota(jnp.int32, (ne, TILE), 0).astype(F32)
    sels, vals = [], []
    for _ in range(TOP_K):
        m = jnp.max(logits, axis=0, keepdims=True)
        idx = jnp.min(jnp.where(logits == m, eid, float(ne)), axis=0, keepdims=True)
        sel = eid == idx
        sels.append(sel)
        vals.append(m)
        logits = jnp.where(sel, -jnp.inf, logits)
    ex = [jnp.exp(v - vals[0]) for v in vals]
    den = ex[0] + ex[1] + ex[2] + ex[3]

    onehot = (sels[0] | sels[1] | sels[2] | sels[3]).astype(F32)
    r_i = lax.broadcasted_iota(jnp.int32, (TILE, TILE), 0)
    c_i = lax.broadcasted_iota(jnp.int32, (TILE, TILE), 1)
    before = _dot(onehot.astype(BF16), (r_i < c_i).astype(BF16))
    cnt = jnp.sum(onehot, axis=1, keepdims=True)
    seg_len = jnp.floor((cnt + (SEG_ALIGN - 1)) * (1.0 / SEG_ALIGN)) * SEG_ALIGN
    e_r = lax.broadcasted_iota(jnp.int32, (ne, ne), 0)
    e_c = lax.broadcasted_iota(jnp.int32, (ne, ne), 1)
    seg_len_b = jnp.broadcast_to(seg_len, (ne, META_LANES))
    seg_off = _dot_f32((e_c < e_r).astype(F32), seg_len_b)[:, 0:1]
    pos = before + seg_off
    sub = lax.broadcasted_iota(jnp.int32, (8, TILE), 0)
    mi = jnp.zeros((8, TILE), jnp.int32)
    mw = jnp.zeros((8, TILE), F32)
    for k in range(TOP_K):
        row = jnp.sum(jnp.where(sels[k], pos, 0.0), axis=0, keepdims=True).astype(jnp.int32)
        mi = jnp.where(sub == k, row, mi)
        mw = jnp.where(sub == k, ex[k] / den, mw)
    mi_ref[0] = mi
    mw_ref[0] = mw
    seg_ref[0] = seg_len_b.astype(jnp.int32)


def _oproj(o, x_args, mods, wo, pg, fg, rw, rb, nb, all_tokens):
    const = lambda i: (0, 0)
    tile_map = lambda i: (i, 0)
    tile3_map = lambda i: (i, 0, 0)
    rw_hi = rw.astype(BF16)
    rw_lo = (rw - rw_hi.astype(F32)).astype(BF16)
    rw_split = jnp.concatenate([rw_hi.T, rw_lo.T], axis=0)
    if all_tokens:
        n_tiles = nb * SMP_TILES
        x_specs = [pl.BlockSpec((TILE, D), _x_lat_map), pl.BlockSpec((TILE, D), _x_ctx_map)]
        mod_map = _mod_map_all(nb)
    else:
        n_tiles = nb * LAT_TILES
        x_specs = [pl.BlockSpec((TILE, D), lambda i: (_lat_of_all(i), 0))]
        mod_map = lambda i: (i // LAT_TILES, 0, 0)
    rows = n_tiles * TILE
    return pl.pallas_call(
        functools.partial(_oproj_kernel, all_tokens),
        grid=(n_tiles,),
        in_specs=[pl.BlockSpec((TILE, D), tile_map)] + x_specs + [
            pl.BlockSpec((1, 1, 6 * D), mod_map),
            pl.BlockSpec((D, D), const),
            pl.BlockSpec((1, D), const),
            pl.BlockSpec((1, D), const),
            pl.BlockSpec((2 * N_EXPERTS, D), const),
            pl.BlockSpec((N_EXPERTS, 1), const)],
        out_specs=[pl.BlockSpec((TILE, D), tile_map),
                   pl.BlockSpec((TILE, D), tile_map),
                   pl.BlockSpec((1, 8, TILE), tile3_map),
                   pl.BlockSpec((1, 8, TILE), tile3_map),
                   pl.BlockSpec((1, N_EXPERTS, META_LANES), tile3_map)],
        out_shape=[jax.ShapeDtypeStruct((rows, D), F32),
                   jax.ShapeDtypeStruct((rows, D), BF16),
                   jax.ShapeDtypeStruct((n_tiles, 8, TILE), jnp.int32),
                   jax.ShapeDtypeStruct((n_tiles, 8, TILE), F32),
                   jax.ShapeDtypeStruct((n_tiles, N_EXPERTS, META_LANES), jnp.int32)],
        compiler_params=_params(("arbitrary",)),
        name="oproj_router",
    )(o, *x_args, mods, wo, pg, fg, rw_split, rb.reshape(N_EXPERTS, 1))


U32 = jnp.uint32
HALF = D // 2


def _pack_rows(x):
    bits = pltpu.bitcast(x.astype(BF16).astype(F32), U32)
    return (bits[:, :HALF] >> 16) | (bits[:, HALF:] & jnp.uint32(0xFFFF0000))


def _unpack_rows(w):
    lo = pltpu.bitcast(w << 16, F32).astype(BF16)
    hi = pltpu.bitcast(w & jnp.uint32(0xFFFF0000), F32).astype(BF16)
    return jnp.concatenate([lo, hi], axis=1)


def _for_pieces(n, largest, fn):
    piece = largest
    while piece >= SEG_ALIGN:
        done = (n // (2 * piece)) * (2 * piece)
        pl.when((n & piece) != 0)(functools.partial(fn, done, piece))
        piece //= 2


PIECES = tuple(TILE >> b for b in range((TILE // SEG_ALIGN).bit_length()))
N_PIECES = len(PIECES)
LIST_ROWS = 2 * N_PIECES + 1


def _piece_lists(seg_len, seg_local, seg_first):
    sizes = jnp.asarray(PIECES, jnp.int32)[None, :, None]
    n = seg_len[:, None, :]
    active = (n & sizes) != 0
    done = n // (2 * sizes) * (2 * sizes)
    order = jnp.argsort(jnp.logical_not(active), axis=-1, stable=True)
    loc = jnp.take_along_axis(seg_local[:, None, :] + done, order, axis=-1)
    glb = jnp.take_along_axis(seg_first[:, None, :] + done, order, axis=-1)
    counts = jnp.sum(active, axis=-1)
    tail = jnp.concatenate([counts, jnp.sum(seg_len, axis=-1, keepdims=True)], axis=-1)
    tail = jnp.pad(tail, ((0, 0), (0, N_EXPERTS - tail.shape[-1])))[:, None, :]
    return jnp.concatenate([loc, glb, tail], axis=1).astype(jnp.int32)


def _start_segment_copies(list_ref, local_ref, hbm_ref, sem, to_hbm):
    for b, rows in enumerate(PIECES):
        def start(j, carry, b=b, rows=rows):
            loc = local_ref.at[pl.ds(pl.multiple_of(list_ref[0, b, j], SEG_ALIGN), rows)]
            glb = hbm_ref.at[pl.ds(pl.multiple_of(list_ref[0, N_PIECES + b, j], SEG_ALIGN), rows)]
            (pltpu.make_async_copy(loc, glb, sem) if to_hbm else pltpu.make_async_copy(glb, loc, sem)).start()
            return carry

        lax.fori_loop(0, list_ref[0, 2 * N_PIECES, b], start, 0)


def _wait_segment_copies(list_ref, local_ref, hbm_ref, sem, to_hbm):
    def wait(done, rows):
        del done
        loc = local_ref.at[pl.ds(0, rows)]
        glb = hbm_ref.at[pl.ds(0, rows)]
        (pltpu.make_async_copy(loc, glb, sem) if to_hbm else pltpu.make_async_copy(glb, loc, sem)).wait()

    _for_pieces(list_ref[0, 2 * N_PIECES, N_PIECES], SORT_ROWS_POW2, wait)


def _zero_fill_copies(tail_ref, free_ref, zbuf, xs_ref, sem, action):
    def per_expert(e, carry):
        def piece_copy(done, rows):
            dst = xs_ref.at[pl.ds(pl.multiple_of(tail_ref[0, e] + done, SEG_ALIGN), rows)]
            action(pltpu.make_async_copy(zbuf.at[pl.ds(0, rows)], dst, sem))

        _for_pieces(tail_ref[1, e], EXPERT_BLOCK // 2, piece_copy)
        return carry

    def per_block(j, carry):
        first = pl.multiple_of(free_ref[0] + j * EXPERT_BLOCK, EXPERT_BLOCK)
        action(pltpu.make_async_copy(zbuf, xs_ref.at[pl.ds(first, EXPERT_BLOCK)], sem))
        return carry

    lax.fori_loop(0, N_EXPERTS, per_expert, 0)
    lax.fori_loop(0, free_ref[1], per_block, 0)


def _dispatch_kernel(seg_ref, seg_prev_ref, tail_ref, free_ref, lp_ref, h_ref, xs_ref, sbuf, zbuf, sem):
    i = pl.program_id(0)
    last = pl.num_programs(0) - 1
    slot = i % 2
    row = lax.broadcasted_iota(jnp.int32, (SORT_ROWS, TILE), 0)
    lp = lp_ref[0]
    hit = row == lp[0:1]
    for k in range(1, TOP_K):
        hit = hit | (row == lp[k:k + 1])
    sbuf[slot] = _pack_rows(_dot(jnp.where(hit, 1.0, 0.0).astype(BF16), h_ref[...]))
    _start_segment_copies(seg_ref, sbuf.at[slot], xs_ref, sem.at[slot], True)

    @pl.when(i > 0)
    def _():
        _wait_segment_copies(seg_prev_ref, sbuf.at[1 - slot], xs_ref, sem.at[1 - slot], True)

    @pl.when(i == last)
    def _():
        _wait_segment_copies(seg_ref, sbuf.at[slot], xs_ref, sem.at[slot], True)
        zbuf[...] = jnp.zeros_like(zbuf)
        _zero_fill_copies(tail_ref, free_ref, zbuf, xs_ref, sem.at[slot], lambda cp: cp.start())
        _zero_fill_copies(tail_ref, free_ref, zbuf, xs_ref, sem.at[slot], lambda cp: cp.wait())


def _dispatch(segs, tails, free, lpos_t, hf, m_rows):
    n_tiles = hf.shape[0] // TILE
    seg_block = (1, LIST_ROWS, N_EXPERTS)
    return pl.pallas_call(
        _dispatch_kernel,
        grid=(n_tiles,),
        in_specs=[pl.BlockSpec(seg_block, lambda i: (i, 0, 0), memory_space=pltpu.SMEM),
                  pl.BlockSpec(seg_block, lambda i: (jnp.maximum(i - 1, 0), 0, 0), memory_space=pltpu.SMEM),
                  pl.BlockSpec(memory_space=pltpu.SMEM),
                  pl.BlockSpec(memory_space=pltpu.SMEM),
                  pl.BlockSpec((1, 8, TILE), lambda i: (i, 0, 0)),
                  pl.BlockSpec((TILE, D), lambda i: (i, 0))],
        out_specs=pl.BlockSpec(memory_space=pl.ANY),
        out_shape=jax.ShapeDtypeStruct((m_rows, HALF), U32),
        scratch_shapes=[pltpu.VMEM((2, SORT_ROWS, HALF), U32), pltpu.VMEM((EXPERT_BLOCK, HALF), U32),
                        pltpu.SemaphoreType.DMA((2,))],
        compiler_params=_params(("arbitrary",)),
        name="moe_dispatch",
    )(segs, segs, tails, free, lpos_t, hf)


def _expert_kernel(be_ref, nu_ref, xs_ref, w1_ref, b1_ref, w2_ref, b2_ref, ys_ref, w1b, w2b):
    i = pl.program_id(0)
    used = i < nu_ref[0]
    new_expert = jnp.logical_or(i == 0, be_ref[i] != be_ref[jnp.maximum(i - 1, 0)])

    @pl.when(jnp.logical_and(used, new_expert))
    def _():
        w1b[...] = w1_ref[0, 0].astype(BF16)
        w2b[...] = w2_ref[0, 0].astype(BF16)

    @pl.when(used)
    def _():
        gu = _dot(_unpack_rows(xs_ref[...]), w1b[...]) + b1_ref[0, 0]
        gate = jnp.minimum(gu[:, :D_FF], SWIGLU_LIMIT)
        lin = jnp.clip(gu[:, D_FF:], -SWIGLU_LIMIT, SWIGLU_LIMIT)
        act = gate * jax.nn.sigmoid(SWIGLU_ALPHA * gate) * (lin + 1.0)
        ys_ref[...] = _pack_rows(_dot(act.astype(BF16), w2b[...]) + b2_ref[0, 0])

    @pl.when(i >= nu_ref[0])
    def _():
        ys_ref[...] = jnp.zeros_like(ys_ref)


def _experts(blk_e, n_used, xs, layer, w1, b1, w2, b2):
    n_blocks = xs.shape[0] // EXPERT_BLOCK
    in_row_map = lambda i, be, nu: (jnp.minimum(i, nu[0] - 1), 0)
    e_map = lambda i, be, nu: (layer, be[i], 0, 0)
    grid_spec = pltpu.PrefetchScalarGridSpec(
        num_scalar_prefetch=2,
        grid=(n_blocks,),
        in_specs=[pl.BlockSpec((EXPERT_BLOCK, HALF), in_row_map),
                  pl.BlockSpec((1, 1, D, 2 * D_FF), e_map),
                  pl.BlockSpec((1, 1, 1, 2 * D_FF), e_map),
                  pl.BlockSpec((1, 1, D_FF, D), e_map),
                  pl.BlockSpec((1, 1, 1, D), e_map)],
        out_specs=pl.BlockSpec((EXPERT_BLOCK, HALF), lambda i, be, nu: (i, 0)),
        scratch_shapes=[pltpu.VMEM((D, 2 * D_FF), BF16), pltpu.VMEM((D_FF, D), BF16)])
    return pl.pallas_call(
        _expert_kernel,
        grid_spec=grid_spec,
        out_shape=jax.ShapeDtypeStruct(xs.shape, U32),
        compiler_params=_params(("arbitrary",)),
        name="moe_experts",
    )(blk_e, n_used, xs, w1, b1[:, :, None, :], w2, b2[:, :, None, :])


def _combine_kernel(seg_ref, seg_next_ref, mi_ref, mw_ref, x_ref, mod_ref, g_ref, ys_ref, o_ref, ybuf, sem):
    i = pl.program_id(0)
    slot = i % 2

    @pl.when(i == 0)
    def _():
        ybuf[...] = jnp.zeros_like(ybuf)
        _start_segment_copies(seg_ref, ybuf.at[0], ys_ref, sem.at[0], False)

    @pl.when(i + 1 < pl.num_programs(0))
    def _():
        _start_segment_copies(seg_next_ref, ybuf.at[1 - slot], ys_ref, sem.at[1 - slot], False)

    half = TILE // 2
    col = lax.broadcasted_iota(jnp.int32, (half, SORT_ROWS), 1)
    wms = []
    for rs in (slice(0, half), slice(half, TILE)):
        mi = mi_ref[rs, :]
        mw = mw_ref[rs, :]
        wm = jnp.zeros((half, SORT_ROWS), F32)
        for k in range(TOP_K):
            wm = jnp.where(col == mi[:, k:k + 1], mw[:, k:k + 1], wm)
        wms.append(wm.astype(BF16))
    _wait_segment_copies(seg_ref, ybuf.at[slot], ys_ref, sem.at[slot], False)
    yb = _unpack_rows(ybuf[slot])
    for wm, rs in zip(wms, (slice(0, half), slice(half, TILE))):
        y = _dot(wm, yb)
        o_ref[rs, :] = x_ref[rs, :] + _mod(mod_ref, 5) * _rms(y, g_ref[...])


def _combine(segs, mi, mw, xn, mods, g, ys, nb, all_tokens):
    n_tiles = xn.shape[0] // TILE
    tile_map = lambda i: (i, 0)
    mod_map = _mod_map_all(nb) if all_tokens else (lambda i: (i // LAT_TILES, 0, 0))
    seg_block = (1, LIST_ROWS, N_EXPERTS)
    return pl.pallas_call(
        _combine_kernel,
        grid=(n_tiles,),
        in_specs=[pl.BlockSpec(seg_block, lambda i: (i, 0, 0), memory_space=pltpu.SMEM),
                  pl.BlockSpec(seg_block, lambda i: (jnp.minimum(i + 1, n_tiles - 1), 0, 0),
                               memory_space=pltpu.SMEM),
                  pl.BlockSpec((TILE, 8), tile_map),
                  pl.BlockSpec((TILE, 8), tile_map),
                  pl.BlockSpec((TILE, D), tile_map),
                  pl.BlockSpec((1, 1, 6 * D), mod_map),
                  pl.BlockSpec((1, D), lambda i: (0, 0)),
                  pl.BlockSpec(memory_space=pl.ANY)],
        out_specs=pl.BlockSpec((TILE, D), tile_map),
        out_shape=jax.ShapeDtypeStruct(xn.shape, F32),
        scratch_shapes=[pltpu.VMEM((2, SORT_ROWS, HALF), U32), pltpu.SemaphoreType.DMA((2,))],
        compiler_params=_params(("arbitrary",)),
        name="moe_combine",
    )(segs, segs, mi, mw, xn, mods, g, ys)


def _moe(hf, mi, mw, seg, xn, mods, g, layer, w1, b1, w2, b2, nb, all_tokens):
    t = hf.shape[0]
    n_tiles = t // TILE
    rows_max = t * TOP_K + n_tiles * N_EXPERTS * (SEG_ALIGN - 1)
    n_blocks = -(-rows_max // EXPERT_BLOCK) + N_EXPERTS
    seg_len = seg[:, :, 0]
    counts = jnp.sum(seg_len, axis=0)
    padded = (counts + EXPERT_BLOCK - 1) // EXPERT_BLOCK * EXPERT_BLOCK
    pad_end = jnp.cumsum(padded)
    pad_start = pad_end - padded
    seg_first = pad_start[None, :] + jnp.cumsum(seg_len, axis=0) - seg_len
    seg_local = jnp.cumsum(seg_len, axis=1) - seg_len
    segs = _piece_lists(seg_len, seg_local, seg_first)
    by_token = lambda a: a.transpose(0, 2, 1).reshape(t, 8)
    blk_row = jnp.arange(n_blocks, dtype=jnp.int32) * EXPERT_BLOCK
    blk_e = jnp.minimum(jnp.sum(pad_end[None, :] <= blk_row[:, None], axis=1), N_EXPERTS - 1).astype(jnp.int32)
    n_used = (pad_end[-1:] // EXPERT_BLOCK).astype(jnp.int32)
    tails = jnp.stack([pad_start + counts, padded - counts]).astype(jnp.int32)
    free = jnp.concatenate([pad_end[-1:], n_blocks - n_used]).astype(jnp.int32)
    xs = _dispatch(segs, tails, free, mi, hf, n_blocks * EXPERT_BLOCK)
    ys = _experts(blk_e, n_used, xs, layer, w1, b1, w2, b2)
    return _combine(segs, by_token(mi), by_token(mw), xn, mods, g, ys, nb, all_tokens)


def _proj1_kernel(x_ref, mod_ref, g_ref, w_ref, q_ref, k_ref, v_ref):
    h = (_rms(x_ref[...], g_ref[...]) * (1.0 + _mod(mod_ref, 1)) + _mod(mod_ref, 0)).astype(BF16)
    qkv = _dot(h, w_ref[...])
    q_ref[...] = (qkv[:, :D] * NA_SCALE).astype(BF16)
    k_ref[...] = qkv[:, D:2 * D].astype(BF16)
    v_ref[...] = qkv[:, 2 * D:].astype(BF16)


def _proj1(x1, mods, g, w, nb):
    n_tiles = nb * SMP_TILES
    tile_map = lambda i: (i, 0)
    return pl.pallas_call(
        _proj1_kernel,
        grid=(n_tiles,),
        in_specs=[pl.BlockSpec((TILE, D), tile_map),
                  pl.BlockSpec((1, 1, 6 * D), _mod_map_all(nb)),
                  pl.BlockSpec((1, D), lambda i: (0, 0)),
                  pl.BlockSpec((D, 3 * D), lambda i: (0, 0))],
        out_specs=[pl.BlockSpec((TILE, D), tile_map)] * 3,
        out_shape=[jax.ShapeDtypeStruct((n_tiles * TILE, D), BF16)] * 3,
        compiler_params=_params(("arbitrary",)),
        name="proj_odd",
    )(x1, mods, g, w)


def _na_row_start(r):
    return jnp.clip(r - WIN_ROWS // 2, 0, GRID_H - WIN_ROWS)


def _na_kernel(q_ref, k_ref, v_ref, bias_ref, o_ref):
    r = pl.program_id(1)
    k0 = pl.multiple_of(_na_row_start(r) * GRID_W, GRID_W)
    n_loc = WIN_ROWS * GRID_W
    lane = lax.broadcasted_iota(jnp.int32, (GRID_W, 128), 1)
    pairs = range(NA_HEADS // 2)
    sls = [slice(j * 128, (j + 1) * 128) for j in pairs]
    qqs = []
    for j in pairs:
        q = q_ref[:, sls[j]]
        zero = jnp.zeros_like(q)
        qqs.append(jnp.concatenate([jnp.where(lane < NA_DH, q, zero),
                                    jnp.where(lane >= NA_DH, q, zero)], axis=0))
    s_loc = jnp.concatenate([_dot_nt(qqs[j], k_ref[pl.ds(k0, n_loc), sls[j]]) for j in pairs], axis=0)
    s_loc = s_loc + bias_ref[0]
    s_ctx = jnp.concatenate([_dot_nt(qqs[j], k_ref[SEQ:, sls[j]]) for j in pairs], axis=0)
    m = jnp.maximum(jnp.max(s_loc, axis=-1, keepdims=True), jnp.max(s_ctx, axis=-1, keepdims=True))
    p_loc = jnp.exp(s_loc - m)
    p_ctx = jnp.exp(s_ctx - m)
    inv_l = 1.0 / (jnp.sum(p_loc, axis=-1, keepdims=True) + jnp.sum(p_ctx, axis=-1, keepdims=True))
    p_loc = p_loc.astype(BF16)
    p_ctx = p_ctx.astype(BF16)
    for j in pairs:
        rows = slice(j * 2 * GRID_W, (j + 1) * 2 * GRID_W)
        pv = (_dot(p_loc[rows], v_ref[pl.ds(k0, n_loc), sls[j]])
              + _dot(p_ctx[rows], v_ref[SEQ:, sls[j]])) * inv_l[rows]
        o_ref[:, sls[j]] = jnp.where(lane < NA_DH, pv[:GRID_W], pv[GRID_W:]).astype(BF16)


def _na_bias(rpb):
    mid = WIN_ROWS // 2
    pat_rows = list(range(mid)) + [mid] + list(range(GRID_H - mid + 1, GRID_H))
    r = np.array(pat_rows)
    rs = np.clip(r - mid, 0, GRID_H - WIN_ROWS)
    row_off = rs[:, None] + np.arange(WIN_ROWS)[None, :] - r[:, None] + WIN_ROWS - 1
    c = np.arange(GRID_W)
    q_start = np.clip(c - WIN_COLS // 2, 0, GRID_W - WIN_COLS)[:, None]
    kc = np.arange(GRID_W)[None, :]
    valid = (kc >= q_start) & (kc < q_start + WIN_COLS)
    col_off = np.clip(kc - c[:, None] + WIN_COLS - 1, 0, 2 * WIN_COLS - 2)
    sel_r = np.eye(2 * WIN_ROWS - 1, dtype=np.float32)[row_off]
    sel_c = np.eye(2 * WIN_COLS - 1, dtype=np.float32)[col_off]
    b = jnp.einsum('pia,hab,ckb->hpick', sel_r, rpb.astype(F32), sel_c, precision=lax.Precision.HIGHEST)
    b = jnp.where(valid[None, None, None], b, MASK_VALUE)
    b = b.transpose(1, 0, 3, 2, 4).reshape(len(pat_rows), NA_HEADS * GRID_W, WIN_ROWS * GRID_W)
    return b


def _na_pattern(r):
    mid = WIN_ROWS // 2
    return jnp.where(r < mid, r, jnp.where(r <= GRID_H - mid, mid, r - (GRID_H - 2 * mid)))


def _na(q, k, v, bias, nb):
    blocks_per_smp = SMP_ROWS // GRID_W
    return pl.pallas_call(
        _na_kernel,
        grid=(nb, GRID_H),
        in_specs=[pl.BlockSpec((GRID_W, D), lambda b, r: (b * blocks_per_smp + r, 0)),
                  pl.BlockSpec((SMP_ROWS, D), lambda b, r: (b, 0)),
                  pl.BlockSpec((SMP_ROWS, D), lambda b, r: (b, 0)),
                  pl.BlockSpec((1,) + bias.shape[1:], lambda b, r: (_na_pattern(r), 0, 0))],
        out_specs=pl.BlockSpec((GRID_W, D), lambda b, r: (b * GRID_H + r, 0)),
        out_shape=jax.ShapeDtypeStruct((nb * SEQ, D), BF16),
        compiler_params=_params(("arbitrary", "arbitrary")),
        name="na_attn",
    )(q, k, v, bias)


def kernel(x, c, ctx, c_ctx, ada_w, ada_b, mix_pre_g, mix_post_g, ffn_pre_g, ffn_post_g, even_w_in,
           diff_lambda, diff_subln_g, mla_q_norm_g, mla_w_qb, mla_kv_norm_g, mla_w_kvb, even_w_out,
           na_w_qkv, na_rpb, na_w_out, router_w, router_b, moe_w1, moe_b1, moe_w2, moe_b2):
    nb = x.shape[0]
    assert x.shape[1:] == (SEQ, D) and ctx.shape[1:] == (CTX, D)
    x2 = x.reshape(nb * SEQ, D)
    c2 = ctx.reshape(nb * CTX, D)
    row = lambda a: a.reshape(1, -1)

    mods = _ada(jnp.concatenate([c, c_ctx[None, :]], axis=0), ada_w, ada_b)
    mods0 = mods[0].reshape(nb + 1, 1, 6 * D)
    mods1 = mods[1].reshape(nb + 1, 1, 6 * D)

    lam_init = 0.8 - 0.6 * math.exp(-0.3 * 0)
    w_in = even_w_in[0]
    wa = jnp.pad(w_in, ((0, 0), (0, 2048 - w_in.shape[1]))).astype(BF16)
    wqb = mla_w_qb[0].reshape(MLA_Q_LORA, MLA_HEADS, MLA_NOPE + MLA_ROPE)
    wqb = jnp.pad(wqb, ((0, 0), (0, 0), (0, MLA_QK_PAD - MLA_NOPE - MLA_ROPE)))
    wqb = wqb.reshape(MLA_Q_LORA, MLA_HEADS * MLA_QK_PAD).astype(BF16)
    qd, kd, vd, qm, km, vm = _proj0(x2, c2, mods0, row(mix_pre_g[0]), wa, row(mla_q_norm_g[0]), wqb,
                                    row(mla_kv_norm_g[0]), mla_w_kvb[0].astype(BF16), nb)
    o0 = _attn0(qd, kd, vd, qm, km, vm, diff_lambda[0], row(diff_subln_g[0]), lam_init, nb)
    xn, hf, mi, mw, seg = _oproj(o0, (x2, c2), mods0, even_w_out[0].astype(BF16), row(mix_post_g[0]),
                                 row(ffn_pre_g[0]), router_w[0], row(router_b[0]), nb, True)
    x1 = _moe(hf, mi, mw, seg, xn, mods0, row(ffn_post_g[0]), 0, moe_w1, moe_b1, moe_w2, moe_b2, nb, True)

    q, k, v = _proj1(x1, mods1, row(mix_pre_g[1]), na_w_qkv[0].astype(BF16), nb)
    o1 = _na(q, k, v, _na_bias(na_rpb[0]), nb)
    xn, hf, mi, mw, seg = _oproj(o1, (x1,), mods1, na_w_out[0].astype(BF16), row(mix_post_g[1]),
                                 row(ffn_pre_g[1]), router_w[1], row(router_b[1]), nb, False)
    out = _moe(hf, mi, mw, seg, xn, mods1, row(ffn_post_g[1]), 1, moe_w1, moe_b1, moe_w2, moe_b2, nb, False)
    return out.reshape(nb, SEQ, D)
```

```python
import functools
import math

import numpy as np
import jax
import jax.numpy as jnp
from jax import lax
from jax.experimental import pallas as pl
from jax.experimental.pallas import tpu as pltpu

F32 = jnp.float32
BF16 = jnp.bfloat16

D = 1024
SEQ = 2048
CTX = 256
GRID_W = 64
GRID_H = SEQ // GRID_W
TILE = 256
LAT_TILES = SEQ // TILE
SMP_TILES = LAT_TILES + 1
SMP_ROWS = SEQ + CTX
EPS = 1e-6
ROPE_BASE = 10000.0

DIFF_HEADS = 4
DIFF_DH = 64
DIFF_SCALE = DIFF_DH ** -0.5
MLA_HEADS = 4
MLA_Q_LORA = 256
MLA_KV_LORA = 128
MLA_NOPE = 128
MLA_ROPE = 64
MLA_V = 128
MLA_SCALE = (MLA_NOPE + MLA_ROPE) ** -0.5
MLA_QK_PAD = 256
NA_HEADS = 16
NA_DH = 64
NA_SCALE = NA_DH ** -0.5
WIN_ROWS = 8
WIN_COLS = 16
N_EXPERTS = 32
TOP_K = 4
D_FF = 1024
SWIGLU_ALPHA = 1.702
SWIGLU_LIMIT = 7.0
EXPERT_BLOCK = 512
SEG_ALIGN = 8
SORT_ROWS = -(-(TILE * TOP_K + N_EXPERTS * (SEG_ALIGN - 1)) // 256) * 256
SORT_ROWS_POW2 = 1 << (SORT_ROWS.bit_length() - 1)
MASK_VALUE = -1e30
LOG2_E = math.log2(math.e)
META_LANES = 128
VMEM_LIMIT = 60 * 1024 * 1024

_NT = (((1,), (1,)), ((), ()))


def _dot(a, b):
    return jnp.dot(a, b, preferred_element_type=F32)


def _dot_nt(a, b):
    return lax.dot_general(a, b, _NT, preferred_element_type=F32)


def _dot_f32(a, b):
    return jnp.dot(a, b, preferred_element_type=F32, precision=lax.Precision.HIGHEST)


def _rms(x, g):
    return x * lax.rsqrt(jnp.mean(x * x, axis=-1, keepdims=True) + EPS) * g


def _mod(mod_ref, k):
    return mod_ref[0, :, k * D:(k + 1) * D]


def _params(sem):
    return pltpu.CompilerParams(dimension_semantics=sem, vmem_limit_bytes=VMEM_LIMIT)


def _ada_kernel(c_ref, w_ref, b_ref, o_ref):
    c = c_ref[...]
    s = c * jax.nn.sigmoid(c)
    o_ref[0] = _dot_f32(s, w_ref[0]) + b_ref[0]


def _ada(cc, ada_w, ada_b):
    depth = ada_w.shape[0]
    n = cc.shape[0]
    nt = 6 * D // D
    return pl.pallas_call(
        _ada_kernel,
        grid=(depth, nt),
        in_specs=[pl.BlockSpec((n, D), lambda l, j: (0, 0)),
                  pl.BlockSpec((1, D, D), lambda l, j: (l, 0, j)),
                  pl.BlockSpec((1, 1, D), lambda l, j: (l, 0, j))],
        out_specs=pl.BlockSpec((1, n, D), lambda l, j: (l, 0, j)),
        out_shape=jax.ShapeDtypeStruct((depth, n, 6 * D), F32),
        compiler_params=_params(("arbitrary", "arbitrary")),
        name="ada",
    )(cc, ada_w, ada_b.reshape(depth, 1, 6 * D))


def _x_lat_map(i):
    return ((i // SMP_TILES) * LAT_TILES + jnp.minimum(i % SMP_TILES, LAT_TILES - 1), 0)


def _x_ctx_map(i):
    return (i // SMP_TILES, 0)


def _mod_map_all(nb):
    return lambda i: (jnp.where(i % SMP_TILES == LAT_TILES, nb, i // SMP_TILES), 0, 0)


def _lat_of_all(i):
    return (i // LAT_TILES) * SMP_TILES + i % LAT_TILES


def _rope(x, cos, sa, sb):
    return x * cos + pltpu.roll(x, 112, 1) * sa + pltpu.roll(x, 16, 1) * sb


def _proj0_kernel(x_ref, c_ref, mod_ref, g_ref, wa_ref, qng_ref, wqb_ref, kvng_ref, wkvb_ref,
                  cos_ref, sa_ref, sb_ref, qd_ref, kd_ref, vd_ref, qm_ref, km_ref, vm_ref):
    is_ctx = pl.program_id(0) % SMP_TILES == LAT_TILES
    nd = DIFF_HEADS * 2 * DIFF_DH
    o = 3 * nd
    halves = (slice(0, TILE // 2), slice(TILE // 2, TILE))
    bigs = []
    for rs in halves:
        x = jnp.where(is_ctx, c_ref[rs, :], x_ref[rs, :])
        h = (_rms(x, g_ref[...]) * (1.0 + _mod(mod_ref, 1)) + _mod(mod_ref, 0)).astype(BF16)
        bigs.append(_dot(h, wa_ref[...]))
    for rs, big in zip(halves, bigs):
        cos, sa, sb = cos_ref[rs, :], sa_ref[rs, :], sb_ref[rs, :]
        for j in range(nd // 128):
            sl = slice(j * 128, (j + 1) * 128)
            qd_ref[rs, sl] = _rope(big[:, j * 128:(j + 1) * 128], cos, sa, sb).astype(BF16)
            kd_ref[rs, sl] = _rope(big[:, nd + j * 128:nd + (j + 1) * 128], cos, sa, sb).astype(BF16)
        vd_ref[rs, :] = big[:, 2 * nd:3 * nd].astype(BF16)
        cq = big[:, o:o + MLA_Q_LORA]
        ckv = big[:, o + MLA_Q_LORA:o + MLA_Q_LORA + MLA_KV_LORA]
        kpe = _rope(big[:, o + MLA_Q_LORA + MLA_KV_LORA:], cos, sa, sb).astype(BF16)
        qm = _dot(_rms(cq, qng_ref[...]).astype(BF16), wqb_ref[...])
        kv = _dot(_rms(ckv, kvng_ref[...]).astype(BF16), wkvb_ref[...])
        for hd in range(MLA_HEADS):
            b0 = hd * MLA_QK_PAD
            qm_ref[rs, b0:b0 + 128] = qm[:, b0:b0 + 128].astype(BF16)
            qm_ref[rs, b0 + 128:b0 + 256] = _rope(qm[:, b0 + 128:b0 + 256], cos, sa, sb).astype(BF16)
            km_ref[rs, b0:b0 + 128] = kv[:, hd * 256:hd * 256 + 128].astype(BF16)
            km_ref[rs, b0 + 128:b0 + 256] = kpe
            vm_ref[rs, hd * 128:(hd + 1) * 128] = kv[:, hd * 256 + 128:(hd + 1) * 256].astype(BF16)


def _rope_tables():
    t = np.arange(SEQ)
    half = 16
    inv = (ROPE_BASE ** (-np.arange(half, dtype=np.float32) * 2.0 / (2 * half))).astype(np.float32)
    ang_r = (t // GRID_W).astype(np.float32)[:, None] * inv[None, :]
    ang_c = (t % GRID_W).astype(np.float32)[:, None] * inv[None, :]
    cr, sr, cc, sc = np.cos(ang_r), np.sin(ang_r), np.cos(ang_c), np.sin(ang_c)
    z = np.zeros_like(cr)
    cos = np.concatenate([cr, cr, cc, cc], axis=1)
    sa = np.concatenate([-sr, z, -sc, z], axis=1)
    sb = np.concatenate([z, sr, z, sc], axis=1)

    def full(tab, fill):
        tab = np.tile(tab, (1, 2))
        ident = np.full((CTX, 128), fill, np.float32)
        return jnp.asarray(np.concatenate([tab, ident], axis=0).astype(np.float32))

    return full(cos, 1.0), full(sa, 0.0), full(sb, 0.0)


def _proj0(x2, c2, mods, g, wa, qng, wqb, kvng, wkvb, nb):
    n_tiles = nb * SMP_TILES
    rows = nb * SMP_ROWS
    cos, sa, sb = _rope_tables()
    tile_map = lambda i: (i, 0)
    const = lambda i: (0, 0)
    tab_spec = pl.BlockSpec((TILE, 128), lambda i: (i % SMP_TILES, 0))
    widths = (512, 512, 512, 1024, 1024, 512)
    return pl.pallas_call(
        _proj0_kernel,
        grid=(n_tiles,),
        in_specs=[pl.BlockSpec((TILE, D), _x_lat_map),
                  pl.BlockSpec((TILE, D), _x_ctx_map),
                  pl.BlockSpec((1, 1, 6 * D), _mod_map_all(nb)),
                  pl.BlockSpec((1, D), const),
                  pl.BlockSpec(wa.shape, const),
                  pl.BlockSpec((1, MLA_Q_LORA), const),
                  pl.BlockSpec(wqb.shape, const),
                  pl.BlockSpec((1, MLA_KV_LORA), const),
                  pl.BlockSpec(wkvb.shape, const),
                  tab_spec, tab_spec, tab_spec],
        out_specs=[pl.BlockSpec((TILE, w), tile_map) for w in widths],
        out_shape=[jax.ShapeDtypeStruct((rows, w), BF16) for w in widths],
        compiler_params=_params(("arbitrary",)),
        name="proj_even",
    )(x2, c2, mods, g, wa, qng, wqb, kvng, wkvb, cos, sa, sb)


def _softmax_parts(s, scale):
    c = scale * LOG2_E
    m = jnp.max(s, axis=-1, keepdims=True)
    p = jnp.exp2(s * c - m * c)
    return p, jnp.sum(p, axis=-1, keepdims=True)


def _attn0_kernel(lam_init, qd_ref, kd_ref, vd_ref, qm_ref, km_ref, vm_ref, lam_ref, sg_ref, o_ref):
    lv = lam_ref[...]
    lam = (jnp.exp(jnp.sum(lv[0:1] * lv[1:2], axis=1, keepdims=True))
           - jnp.exp(jnp.sum(lv[2:3] * lv[3:4], axis=1, keepdims=True)) + lam_init)
    lane = lax.broadcasted_iota(jnp.int32, (TILE, 128), 1)

    def heads(k_lo, nk):
        def diff_scores(hd):
            sl = slice(hd * 128, (hd + 1) * 128)
            q = qd_ref[:, sl]
            zero = jnp.zeros_like(q)
            qq = jnp.concatenate([jnp.where(lane < DIFF_DH, q, zero),
                                  jnp.where(lane >= DIFF_DH, q, zero)], axis=0)
            return _dot_nt(qq, kd_ref[k_lo:k_lo + nk, sl])

        def diff_finish(hd, s):
            sl = slice(hd * 128, (hd + 1) * 128)
            p, l = _softmax_parts(s, DIFF_SCALE)
            pv = _dot(p.astype(BF16), vd_ref[k_lo:k_lo + nk, sl])
            o = pv[:TILE] / l[:TILE] - lam * (pv[TILE:] / l[TILE:])
            o_ref[:, sl] = (_rms(o, sg_ref[...]) * (1.0 - lam_init)).astype(BF16)

        def mla_scores(hd):
            ql = slice(hd * MLA_QK_PAD, (hd + 1) * MLA_QK_PAD)
            return _dot_nt(qm_ref[:, ql], km_ref[k_lo:k_lo + nk, ql])

        def mla_finish(hd, s):
            p, l = _softmax_parts(s, MLA_SCALE)
            pv = _dot(p.astype(BF16), vm_ref[k_lo:k_lo + nk, hd * 128:(hd + 1) * 128])
            o_ref[:, 512 + hd * 128:512 + (hd + 1) * 128] = (pv / l).astype(BF16)

        jobs = ([(diff_scores, diff_finish, hd) for hd in range(DIFF_HEADS)]
                + [(mla_scores, mla_finish, hd) for hd in range(MLA_HEADS)])
        s_next = jobs[0][0](jobs[0][2])
        for n, (_, finish, hd) in enumerate(jobs):
            s = s_next
            if n + 1 < len(jobs):
                s_next = jobs[n + 1][0](jobs[n + 1][2])
            finish(hd, s)

    is_ctx = pl.program_id(1) == LAT_TILES

    @pl.when(jnp.logical_not(is_ctx))
    def _():
        heads(0, SMP_ROWS)

    @pl.when(is_ctx)
    def _():
        heads(SEQ, CTX)


def _attn0(qd, kd, vd, qm, km, vm, lam_vec, subln_g, lam_init, nb):
    q_map = lambda b, p: (b * SMP_TILES + p, 0)
    k_map = lambda b, p: (b, 0)
    const = lambda b, p: (0, 0)
    return pl.pallas_call(
        functools.partial(_attn0_kernel, lam_init),
        grid=(nb, SMP_TILES),
        in_specs=[pl.BlockSpec((TILE, 512), q_map),
                  pl.BlockSpec((SMP_ROWS, 512), k_map),
                  pl.BlockSpec((SMP_ROWS, 512), k_map),
                  pl.BlockSpec((TILE, 1024), q_map),
                  pl.BlockSpec((SMP_ROWS, 1024), k_map),
                  pl.BlockSpec((SMP_ROWS, 512), k_map),
                  pl.BlockSpec(lam_vec.shape, const),
                  pl.BlockSpec((1, 128), const)],
        out_specs=pl.BlockSpec((TILE, D), q_map),
        out_shape=jax.ShapeDtypeStruct((nb * SMP_ROWS, D), BF16),
        compiler_params=_params(("arbitrary", "arbitrary")),
        name="attn_even",
    )(qd, kd, vd, qm, km, vm, lam_vec, subln_g)


def _oproj_kernel(dual, *refs):
    if dual:
        (o_ref, xl_ref, xc_ref, mod_ref, wo_ref, pg_ref, fg_ref, rw_ref, rb_ref,
         xn_ref, hf_ref, mi_ref, mw_ref, seg_ref) = refs
        is_ctx = pl.program_id(0) % SMP_TILES == LAT_TILES
        x = jnp.where(is_ctx, xc_ref[...], xl_ref[...])
    else:
        (o_ref, x_ref, mod_ref, wo_ref, pg_ref, fg_ref, rw_ref, rb_ref,
         xn_ref, hf_ref, mi_ref, mw_ref, seg_ref) = refs
        x = x_ref[...]

    his, los = [], []
    for rs in (slice(0, TILE // 2), slice(TILE // 2, TILE)):
        y = _dot(o_ref[rs, :], wo_ref[...])
        xn = x[rs] + _mod(mod_ref, 2) * _rms(y, pg_ref[...])
        xn_ref[rs, :] = xn
        hf = _rms(xn, fg_ref[...]) * (1.0 + _mod(mod_ref, 4)) + _mod(mod_ref, 3)
        hi = hf.astype(BF16)
        hf_ref[rs, :] = hi
        his.append(hi)
        los.append((hf - hi.astype(F32)).astype(BF16))

    lt = _dot_nt(rw_ref[...], jnp.concatenate(his + los, axis=0))
    ne = N_EXPERTS
    logits = (lt[:ne, :TILE] + lt[ne:, :TILE]) + (lt[:ne, TILE:] + lt[ne:, TILE:]) + rb_ref[...]

    eid = lax.broadcasted_iota(jnp.int32, (ne, TILE), 0).astype(F32)
    sels, vals = [], []
    for _ in range(TOP_K):
        m = jnp.max(logits, axis=0, keepdims=True)
        idx = jnp.min(jnp.where(logits == m, eid, float(ne)), axis=0, keepdims=True)
        sel = eid == idx
        sels.append(sel)
        vals.append(m)
        logits = jnp.where(sel, -jnp.inf, logits)
    ex = [jnp.exp(v - vals[0]) for v in vals]
    den = ex[0] + ex[1] + ex[2] + ex[3]

    onehot = (sels[0] | sels[1] | sels[2] | sels[3]).astype(F32)
    r_i = lax.broadcasted_iota(jnp.int32, (TILE, TILE), 0)
    c_i = lax.broadcasted_iota(jnp.int32, (TILE, TILE), 1)
    before = _dot(onehot.astype(BF16), (r_i < c_i).astype(BF16))
    cnt = jnp.sum(onehot, axis=1, keepdims=True)
    seg_len = jnp.floor((cnt + (SEG_ALIGN - 1)) * (1.0 / SEG_ALIGN)) * SEG_ALIGN
    e_r = lax.broadcasted_iota(jnp.int32, (ne, ne), 0)
    e_c = lax.broadcasted_iota(jnp.int32, (ne, ne), 1)
    seg_len_b = jnp.broadcast_to(seg_len, (ne, META_LANES))
    seg_off = _dot_f32((e_c < e_r).astype(F32), seg_len_b)[:, 0:1]
    pos = before + seg_off
    sub = lax.broadcasted_iota(jnp.int32, (8, TILE), 0)
    mi = jnp.zeros((8, TILE), jnp.int32)
    mw = jnp.zeros((8, TILE), F32)
    for k in range(TOP_K):
        row = jnp.sum(jnp.where(sels[k], pos, 0.0), axis=0, keepdims=True).astype(jnp.int32)
        mi = jnp.where(sub == k, row, mi)
        mw = jnp.where(sub == k, ex[k] / den, mw)
    mi_ref[0] = mi
    mw_ref[0] = mw
    seg_ref[0] = seg_len_b.astype(jnp.int32)


def _oproj(o, x_args, mods, wo, pg, fg, rw, rb, nb, all_tokens):
    const = lambda i: (0, 0)
    tile_map = lambda i: (i, 0)
    tile3_map = lambda i: (i, 0, 0)
    rw_hi = rw.astype(BF16)
    rw_lo = (rw - rw_hi.astype(F32)).astype(BF16)
    rw_split = jnp.concatenate([rw_hi.T, rw_lo.T], axis=0)
    if all_tokens:
        n_tiles = nb * SMP_TILES
        x_specs = [pl.BlockSpec((TILE, D), _x_lat_map), pl.BlockSpec((TILE, D), _x_ctx_map)]
        mod_map = _mod_map_all(nb)
    else:
        n_tiles = nb * LAT_TILES
        x_specs = [pl.BlockSpec((TILE, D), lambda i: (_lat_of_all(i), 0))]
        mod_map = lambda i: (i // LAT_TILES, 0, 0)
    rows = n_tiles * TILE
    return pl.pallas_call(
        functools.partial(_oproj_kernel, all_tokens),
        grid=(n_tiles,),
        in_specs=[pl.BlockSpec((TILE, D), tile_map)] + x_specs + [
            pl.BlockSpec((1, 1, 6 * D), mod_map),
            pl.BlockSpec((D, D), const),
            pl.BlockSpec((1, D), const),
            pl.BlockSpec((1, D), const),
            pl.BlockSpec((2 * N_EXPERTS, D), const),
            pl.BlockSpec((N_EXPERTS, 1), const)],
        out_specs=[pl.BlockSpec((TILE, D), tile_map),
                   pl.BlockSpec((TILE, D), tile_map),
                   pl.BlockSpec((1, 8, TILE), tile3_map),
                   pl.BlockSpec((1, 8, TILE), tile3_map),
                   pl.BlockSpec((1, N_EXPERTS, META_LANES), tile3_map)],
        out_shape=[jax.ShapeDtypeStruct((rows, D), F32),
                   jax.ShapeDtypeStruct((rows, D), BF16),
                   jax.ShapeDtypeStruct((n_tiles, 8, TILE), jnp.int32),
                   jax.ShapeDtypeStruct((n_tiles, 8, TILE), F32),
                   jax.ShapeDtypeStruct((n_tiles, N_EXPERTS, META_LANES), jnp.int32)],
        compiler_params=_params(("arbitrary",)),
        name="oproj_router",
    )(o, *x_args, mods, wo, pg, fg, rw_split, rb.reshape(N_EXPERTS, 1))


U32 = jnp.uint32
HALF = D // 2


def _pack_rows(x):
    bits = pltpu.bitcast(x.astype(BF16).astype(F32), U32)
    return (bits[:, :HALF] >> 16) | (bits[:, HALF:] & jnp.uint32(0xFFFF0000))


def _unpack_rows(w):
    lo = pltpu.bitcast(w << 16, F32).astype(BF16)
    hi = pltpu.bitcast(w & jnp.uint32(0xFFFF0000), F32).astype(BF16)
    return jnp.concatenate([lo, hi], axis=1)


def _for_pieces(n, largest, fn):
    piece = largest
    while piece >= SEG_ALIGN:
        done = (n // (2 * piece)) * (2 * piece)
        pl.when((n & piece) != 0)(functools.partial(fn, done, piece))
        piece //= 2


PIECES = tuple(TILE >> b for b in range((TILE // SEG_ALIGN).bit_length()))
N_PIECES = len(PIECES)
LIST_ROWS = 2 * N_PIECES + 1


def _piece_lists(seg_len, seg_local, seg_first):
    sizes = jnp.asarray(PIECES, jnp.int32)[None, :, None]
    n = seg_len[:, None, :]
    active = (n & sizes) != 0
    done = n // (2 * sizes) * (2 * sizes)
    slot = jnp.cumsum(active, axis=-1) - 1
    put = active[..., :, None] & (slot[..., :, None] == jnp.arange(N_EXPERTS))
    compact = lambda v: jnp.sum(jnp.where(put, v[..., :, None], 0), axis=-2)
    loc = compact(seg_local[:, None, :] + done)
    glb = compact(seg_first[:, None, :] + done)
    counts = jnp.sum(active, axis=-1)
    tail = jnp.concatenate([counts, jnp.sum(seg_len, axis=-1, keepdims=True)], axis=-1)
    tail = jnp.pad(tail, ((0, 0), (0, N_EXPERTS - tail.shape[-1])))[:, None, :]
    return jnp.concatenate([loc, glb, tail], axis=1).astype(jnp.int32)


def _start_segment_copies(list_ref, local_ref, hbm_ref, sem, to_hbm):
    for b, rows in enumerate(PIECES):
        def start(j, carry, b=b, rows=rows):
            loc = local_ref.at[pl.ds(pl.multiple_of(list_ref[0, b, j], SEG_ALIGN), rows)]
            glb = hbm_ref.at[pl.ds(pl.multiple_of(list_ref[0, N_PIECES + b, j], SEG_ALIGN), rows)]
            (pltpu.make_async_copy(loc, glb, sem) if to_hbm else pltpu.make_async_copy(glb, loc, sem)).start()
            return carry

        lax.fori_loop(0, list_ref[0, 2 * N_PIECES, b], start, 0)


def _wait_segment_copies(list_ref, local_ref, hbm_ref, sem, to_hbm):
    def wait(done, rows):
        del done
        loc = local_ref.at[pl.ds(0, rows)]
        glb = hbm_ref.at[pl.ds(0, rows)]
        (pltpu.make_async_copy(loc, glb, sem) if to_hbm else pltpu.make_async_copy(glb, loc, sem)).wait()

    _for_pieces(list_ref[0, 2 * N_PIECES, N_PIECES], SORT_ROWS_POW2, wait)


def _zero_fill_copies(tail_ref, free_ref, zbuf, xs_ref, sem, action):
    def per_expert(e, carry):
        def piece_copy(done, rows):
            dst = xs_ref.at[pl.ds(pl.multiple_of(tail_ref[0, e] + done, SEG_ALIGN), rows)]
            action(pltpu.make_async_copy(zbuf.at[pl.ds(0, rows)], dst, sem))

        _for_pieces(tail_ref[1, e], EXPERT_BLOCK // 2, piece_copy)
        return carry

    def per_block(j, carry):
        first = pl.multiple_of(free_ref[0] + j * EXPERT_BLOCK, EXPERT_BLOCK)
        action(pltpu.make_async_copy(zbuf, xs_ref.at[pl.ds(first, EXPERT_BLOCK)], sem))
        return carry

    lax.fori_loop(0, N_EXPERTS, per_expert, 0)
    lax.fori_loop(0, free_ref[1], per_block, 0)


def _dispatch_kernel(seg_ref, seg_prev_ref, tail_ref, free_ref, lp_ref, h_ref, xs_ref, sbuf, zbuf, sem):
    i = pl.program_id(0)
    last = pl.num_programs(0) - 1
    slot = i % 2
    row = lax.broadcasted_iota(jnp.int32, (SORT_ROWS, TILE), 0)
    lp = lp_ref[0]
    hit = row == lp[0:1]
    for k in range(1, TOP_K):
        hit = hit | (row == lp[k:k + 1])
    sbuf[slot] = _pack_rows(_dot(jnp.where(hit, 1.0, 0.0).astype(BF16), h_ref[...]))
    _start_segment_copies(seg_ref, sbuf.at[slot], xs_ref, sem.at[slot], True)

    @pl.when(i > 0)
    def _():
        _wait_segment_copies(seg_prev_ref, sbuf.at[1 - slot], xs_ref, sem.at[1 - slot], True)

    @pl.when(i == last)
    def _():
        _wait_segment_copies(seg_ref, sbuf.at[slot], xs_ref, sem.at[slot], True)
        zbuf[...] = jnp.zeros_like(zbuf)
        _zero_fill_copies(tail_ref, free_ref, zbuf, xs_ref, sem.at[slot], lambda cp: cp.start())
        _zero_fill_copies(tail_ref, free_ref, zbuf, xs_ref, sem.at[slot], lambda cp: cp.wait())


def _dispatch(segs, tails, free, lpos_t, hf, m_rows):
    n_tiles = hf.shape[0] // TILE
    seg_block = (1, LIST_ROWS, N_EXPERTS)
    return pl.pallas_call(
        _dispatch_kernel,
        grid=(n_tiles,),
        in_specs=[pl.BlockSpec(seg_block, lambda i: (i, 0, 0), memory_space=pltpu.SMEM),
                  pl.BlockSpec(seg_block, lambda i: (jnp.maximum(i - 1, 0), 0, 0), memory_space=pltpu.SMEM),
                  pl.BlockSpec(memory_space=pltpu.SMEM),
                  pl.BlockSpec(memory_space=pltpu.SMEM),
                  pl.BlockSpec((1, 8, TILE), lambda i: (i, 0, 0)),
                  pl.BlockSpec((TILE, D), lambda i: (i, 0))],
        out_specs=pl.BlockSpec(memory_space=pl.ANY),
        out_shape=jax.ShapeDtypeStruct((m_rows, HALF), U32),
        scratch_shapes=[pltpu.VMEM((2, SORT_ROWS, HALF), U32), pltpu.VMEM((EXPERT_BLOCK, HALF), U32),
                        pltpu.SemaphoreType.DMA((2,))],
        compiler_params=_params(("arbitrary",)),
        name="moe_dispatch",
    )(segs, segs, tails, free, lpos_t, hf)


def _expert_kernel(be_ref, nu_ref, xs_ref, w1_ref, b1_ref, w2_ref, b2_ref, ys_ref, w1b, w2b):
    i = pl.program_id(0)
    used = i < nu_ref[0]
    new_expert = jnp.logical_or(i == 0, be_ref[i] != be_ref[jnp.maximum(i - 1, 0)])

    @pl.when(jnp.logical_and(used, new_expert))
    def _():
        w1b[...] = w1_ref[0, 0].astype(BF16)
        w2b[...] = w2_ref[0, 0].astype(BF16)

    @pl.when(used)
    def _():
        gu = _dot(_unpack_rows(xs_ref[...]), w1b[...]) + b1_ref[0, 0]
        gate = jnp.minimum(gu[:, :D_FF], SWIGLU_LIMIT)
        lin = jnp.clip(gu[:, D_FF:], -SWIGLU_LIMIT, SWIGLU_LIMIT)
        act = gate * jax.nn.sigmoid(SWIGLU_ALPHA * gate) * (lin + 1.0)
        ys_ref[...] = _pack_rows(_dot(act.astype(BF16), w2b[...]) + b2_ref[0, 0])

    @pl.when(i >= nu_ref[0])
    def _():
        ys_ref[...] = jnp.zeros_like(ys_ref)


def _experts(blk_e, n_used, xs, layer, w1, b1, w2, b2):
    n_blocks = xs.shape[0] // EXPERT_BLOCK
    in_row_map = lambda i, be, nu: (jnp.minimum(i, nu[0] - 1), 0)
    e_map = lambda i, be, nu: (layer, be[i], 0, 0)
    grid_spec = pltpu.PrefetchScalarGridSpec(
        num_scalar_prefetch=2,
        grid=(n_blocks,),
        in_specs=[pl.BlockSpec((EXPERT_BLOCK, HALF), in_row_map),
                  pl.BlockSpec((1, 1, D, 2 * D_FF), e_map),
                  pl.BlockSpec((1, 1, 1, 2 * D_FF), e_map),
                  pl.BlockSpec((1, 1, D_FF, D), e_map),
                  pl.BlockSpec((1, 1, 1, D), e_map)],
        out_specs=pl.BlockSpec((EXPERT_BLOCK, HALF), lambda i, be, nu: (i, 0)),
        scratch_shapes=[pltpu.VMEM((D, 2 * D_FF), BF16), pltpu.VMEM((D_FF, D), BF16)])
    return pl.pallas_call(
        _expert_kernel,
        grid_spec=grid_spec,
        out_shape=jax.ShapeDtypeStruct(xs.shape, U32),
        compiler_params=_params(("arbitrary",)),
        name="moe_experts",
    )(blk_e, n_used, xs, w1, b1[:, :, None, :], w2, b2[:, :, None, :])


def _combine_kernel(seg_ref, seg_next_ref, mi_ref, mw_ref, x_ref, mod_ref, g_ref, ys_ref, o_ref, ybuf, sem):
    i = pl.program_id(0)
    slot = i % 2

    @pl.when(i == 0)
    def _():
        ybuf[...] = jnp.zeros_like(ybuf)
        _start_segment_copies(seg_ref, ybuf.at[0], ys_ref, sem.at[0], False)

    @pl.when(i + 1 < pl.num_programs(0))
    def _():
        _start_segment_copies(seg_next_ref, ybuf.at[1 - slot], ys_ref, sem.at[1 - slot], False)

    half = TILE // 2
    col = lax.broadcasted_iota(jnp.int32, (half, SORT_ROWS), 1)
    wms = []
    for rs in (slice(0, half), slice(half, TILE)):
        mi = mi_ref[rs, :]
        mw = mw_ref[rs, :]
        wm = jnp.zeros((half, SORT_ROWS), F32)
        for k in range(TOP_K):
            wm = jnp.where(col == mi[:, k:k + 1], mw[:, k:k + 1], wm)
        wms.append(wm.astype(BF16))
    _wait_segment_copies(seg_ref, ybuf.at[slot], ys_ref, sem.at[slot], False)
    yb = _unpack_rows(ybuf[slot])
    for wm, rs in zip(wms, (slice(0, half), slice(half, TILE))):
        y = _dot(wm, yb)
        o_ref[rs, :] = x_ref[rs, :] + _mod(mod_ref, 5) * _rms(y, g_ref[...])


def _combine(segs, mi, mw, xn, mods, g, ys, nb, all_tokens):
    n_tiles = xn.shape[0] // TILE
    tile_map = lambda i: (i, 0)
    mod_map = _mod_map_all(nb) if all_tokens else (lambda i: (i // LAT_TILES, 0, 0))
    seg_block = (1, LIST_ROWS, N_EXPERTS)
    return pl.pallas_call(
        _combine_kernel,
        grid=(n_tiles,),
        in_specs=[pl.BlockSpec(seg_block, lambda i: (i, 0, 0), memory_space=pltpu.SMEM),
                  pl.BlockSpec(seg_block, lambda i: (jnp.minimum(i + 1, n_tiles - 1), 0, 0),
                               memory_space=pltpu.SMEM),
                  pl.BlockSpec((TILE, 8), tile_map),
                  pl.BlockSpec((TILE, 8), tile_map),
                  pl.BlockSpec((TILE, D), tile_map),
                  pl.BlockSpec((1, 1, 6 * D), mod_map),
                  pl.BlockSpec((1, D), lambda i: (0, 0)),
                  pl.BlockSpec(memory_space=pl.ANY)],
        out_specs=pl.BlockSpec((TILE, D), tile_map),
        out_shape=jax.ShapeDtypeStruct(xn.shape, F32),
        scratch_shapes=[pltpu.VMEM((2, SORT_ROWS, HALF), U32), pltpu.SemaphoreType.DMA((2,))],
        compiler_params=_params(("arbitrary",)),
        name="moe_combine",
    )(segs, segs, mi, mw, xn, mods, g, ys)


def _moe(hf, mi, mw, seg, xn, mods, g, layer, w1, b1, w2, b2, nb, all_tokens):
    t = hf.shape[0]
    n_tiles = t // TILE
    rows_max = t * TOP_K + n_tiles * N_EXPERTS * (SEG_ALIGN - 1)
    n_blocks = -(-rows_max // EXPERT_BLOCK) + N_EXPERTS
    seg_len = seg[:, :, 0]
    counts = jnp.sum(seg_len, axis=0)
    padded = (counts + EXPERT_BLOCK - 1) // EXPERT_BLOCK * EXPERT_BLOCK
    pad_end = jnp.cumsum(padded)
    pad_start = pad_end - padded
    seg_first = pad_start[None, :] + jnp.cumsum(seg_len, axis=0) - seg_len
    seg_local = jnp.cumsum(seg_len, axis=1) - seg_len
    segs = _piece_lists(seg_len, seg_local, seg_first)
    by_token = lambda a: a.transpose(0, 2, 1).reshape(t, 8)
    blk_row = jnp.arange(n_blocks, dtype=jnp.int32) * EXPERT_BLOCK
    blk_e = jnp.minimum(jnp.sum(pad_end[None, :] <= blk_row[:, None], axis=1), N_EXPERTS - 1).astype(jnp.int32)
    n_used = (pad_end[-1:] // EXPERT_BLOCK).astype(jnp.int32)
    tails = jnp.stack([pad_start + counts, padded - counts]).astype(jnp.int32)
    free = jnp.concatenate([pad_end[-1:], n_blocks - n_used]).astype(jnp.int32)
    xs = _dispatch(segs, tails, free, mi, hf, n_blocks * EXPERT_BLOCK)
    ys = _experts(blk_e, n_used, xs, layer, w1, b1, w2, b2)
    return _combine(segs, by_token(mi), by_token(mw), xn, mods, g, ys, nb, all_tokens)


def _proj1_kernel(x_ref, mod_ref, g_ref, w_ref, q_ref, k_ref, v_ref):
    h = (_rms(x_ref[...], g_ref[...]) * (1.0 + _mod(mod_ref, 1)) + _mod(mod_ref, 0)).astype(BF16)
    qkv = _dot(h, w_ref[...])
    q_ref[...] = (qkv[:, :D] * NA_SCALE).astype(BF16)
    k_ref[...] = qkv[:, D:2 * D].astype(BF16)
    v_ref[...] = qkv[:, 2 * D:].astype(BF16)


def _proj1(x1, mods, g, w, nb):
    n_tiles = nb * SMP_TILES
    tile_map = lambda i: (i, 0)
    return pl.pallas_call(
        _proj1_kernel,
        grid=(n_tiles,),
        in_specs=[pl.BlockSpec((TILE, D), tile_map),
                  pl.BlockSpec((1, 1, 6 * D), _mod_map_all(nb)),
                  pl.BlockSpec((1, D), lambda i: (0, 0)),
                  pl.BlockSpec((D, 3 * D), lambda i: (0, 0))],
        out_specs=[pl.BlockSpec((TILE, D), tile_map)] * 3,
        out_shape=[jax.ShapeDtypeStruct((n_tiles * TILE, D), BF16)] * 3,
        compiler_params=_params(("arbitrary",)),
        name="proj_odd",
    )(x1, mods, g, w)


def _na_row_start(r):
    return jnp.clip(r - WIN_ROWS // 2, 0, GRID_H - WIN_ROWS)


NA_STEP_ROWS = 4


def _na_kernel(q_ref, k_ref, v_ref, *rest):
    bias_refs, o_ref = rest[:NA_STEP_ROWS], rest[NA_STEP_ROWS]
    n_loc = WIN_ROWS * GRID_W
    lane = lax.broadcasted_iota(jnp.int32, (GRID_W, 128), 1)
    pairs = range(NA_HEADS // 2)
    sls = [slice(j * 128, (j + 1) * 128) for j in pairs]
    k0s, qqs = [], []
    for t in range(NA_STEP_ROWS):
        r = pl.program_id(1) * NA_STEP_ROWS + t
        k0s.append(pl.multiple_of(_na_row_start(r) * GRID_W, GRID_W))
        for j in pairs:
            q = q_ref[t * GRID_W:(t + 1) * GRID_W, sls[j]]
            zero = jnp.zeros_like(q)
            qqs.append(jnp.concatenate([jnp.where(lane < NA_DH, q, zero),
                                        jnp.where(lane >= NA_DH, q, zero)], axis=0))
    jobs = [(t, j) for t in range(NA_STEP_ROWS) for j in pairs]
    s_loc = jnp.concatenate([_dot_nt(qqs[n], k_ref[pl.ds(k0s[t], n_loc), sls[j]])
                             for n, (t, j) in enumerate(jobs)], axis=0)
    s_loc = s_loc + jnp.concatenate([b[0] for b in bias_refs], axis=0)
    s_ctx = jnp.concatenate([_dot_nt(qqs[n], k_ref[SEQ:, sls[j]]) for n, (t, j) in enumerate(jobs)], axis=0)
    m = jnp.maximum(jnp.max(s_loc, axis=-1, keepdims=True), jnp.max(s_ctx, axis=-1, keepdims=True))
    p_loc = jnp.exp(s_loc - m)
    p_ctx = jnp.exp(s_ctx - m)
    inv_l = 1.0 / (jnp.sum(p_loc, axis=-1, keepdims=True) + jnp.sum(p_ctx, axis=-1, keepdims=True))
    p_loc = p_loc.astype(BF16)
    p_ctx = p_ctx.astype(BF16)
    for n, (t, j) in enumerate(jobs):
        rows = slice(n * 2 * GRID_W, (n + 1) * 2 * GRID_W)
        pv = (_dot(p_loc[rows], v_ref[pl.ds(k0s[t], n_loc), sls[j]])
              + _dot(p_ctx[rows], v_ref[SEQ:, sls[j]])) * inv_l[rows]
        o_ref[t * GRID_W:(t + 1) * GRID_W, sls[j]] = jnp.where(lane < NA_DH, pv[:GRID_W],
                                                               pv[GRID_W:]).astype(BF16)


def _na_bias(rpb):
    mid = WIN_ROWS // 2
    pat_rows = list(range(mid)) + [mid] + list(range(GRID_H - mid + 1, GRID_H))
    r = np.array(pat_rows)
    rs = np.clip(r - mid, 0, GRID_H - WIN_ROWS)
    row_off = rs[:, None] + np.arange(WIN_ROWS)[None, :] - r[:, None] + WIN_ROWS - 1
    c = np.arange(GRID_W)
    q_start = np.clip(c - WIN_COLS // 2, 0, GRID_W - WIN_COLS)[:, None]
    kc = np.arange(GRID_W)[None, :]
    valid = (kc >= q_start) & (kc < q_start + WIN_COLS)
    col_off = np.clip(kc - c[:, None] + WIN_COLS - 1, 0, 2 * WIN_COLS - 2)
    sel_r = np.eye(2 * WIN_ROWS - 1, dtype=np.float32)[row_off]
    sel_c = np.eye(2 * WIN_COLS - 1, dtype=np.float32)[col_off]
    b = jnp.einsum('pia,hab,ckb->hpick', sel_r, rpb.astype(F32), sel_c, precision=lax.Precision.HIGHEST)
    b = jnp.where(valid[None, None, None], b, MASK_VALUE)
    b = b.transpose(1, 0, 3, 2, 4).reshape(len(pat_rows), NA_HEADS * GRID_W, WIN_ROWS * GRID_W)
    return b


def _na_pattern(r):
    mid = WIN_ROWS // 2
    return jnp.where(r < mid, r, jnp.where(r <= GRID_H - mid, mid, r - (GRID_H - 2 * mid)))


def _na(q, k, v, bias, nb):
    q_rows = NA_STEP_ROWS * GRID_W
    steps = GRID_H // NA_STEP_ROWS
    blocks_per_smp = SMP_ROWS // q_rows
    bias_specs = [pl.BlockSpec((1,) + bias.shape[1:],
                               lambda b, s, t=t: (_na_pattern(s * NA_STEP_ROWS + t), 0, 0))
                  for t in range(NA_STEP_ROWS)]
    return pl.pallas_call(
        _na_kernel,
        grid=(nb, steps),
        in_specs=[pl.BlockSpec((q_rows, D), lambda b, s: (b * blocks_per_smp + s, 0)),
                  pl.BlockSpec((SMP_ROWS, D), lambda b, s: (b, 0)),
                  pl.BlockSpec((SMP_ROWS, D), lambda b, s: (b, 0))] + bias_specs,
        out_specs=pl.BlockSpec((q_rows, D), lambda b, s: (b * steps + s, 0)),
        out_shape=jax.ShapeDtypeStruct((nb * SEQ, D), BF16),
        compiler_params=_params(("arbitrary", "arbitrary")),
        name="na_attn",
    )(q, k, v, *([bias] * NA_STEP_ROWS))


def kernel(x, c, ctx, c_ctx, ada_w, ada_b, mix_pre_g, mix_post_g, ffn_pre_g, ffn_post_g, even_w_in,
           diff_lambda, diff_subln_g, mla_q_norm_g, mla_w_qb, mla_kv_norm_g, mla_w_kvb, even_w_out,
           na_w_qkv, na_rpb, na_w_out, router_w, router_b, moe_w1, moe_b1, moe_w2, moe_b2):
    nb = x.shape[0]
    assert x.shape[1:] == (SEQ, D) and ctx.shape[1:] == (CTX, D)
    x2 = x.reshape(nb * SEQ, D)
    c2 = ctx.reshape(nb * CTX, D)
    row = lambda a: a.reshape(1, -1)

    mods = _ada(jnp.concatenate([c, c_ctx[None, :]], axis=0), ada_w, ada_b)
    mods0 = mods[0].reshape(nb + 1, 1, 6 * D)
    mods1 = mods[1].reshape(nb + 1, 1, 6 * D)

    lam_init = 0.8 - 0.6 * math.exp(-0.3 * 0)
    w_in = even_w_in[0]
    wa = jnp.pad(w_in, ((0, 0), (0, 2048 - w_in.shape[1]))).astype(BF16)
    wqb = mla_w_qb[0].reshape(MLA_Q_LORA, MLA_HEADS, MLA_NOPE + MLA_ROPE)
    wqb = jnp.pad(wqb, ((0, 0), (0, 0), (0, MLA_QK_PAD - MLA_NOPE - MLA_ROPE)))
    wqb = wqb.reshape(MLA_Q_LORA, MLA_HEADS * MLA_QK_PAD).astype(BF16)
    qd, kd, vd, qm, km, vm = _proj0(x2, c2, mods0, row(mix_pre_g[0]), wa, row(mla_q_norm_g[0]), wqb,
                                    row(mla_kv_norm_g[0]), mla_w_kvb[0].astype(BF16), nb)
    o0 = _attn0(qd, kd, vd, qm, km, vm, diff_lambda[0], row(diff_subln_g[0]), lam_init, nb)
    xn, hf, mi, mw, seg = _oproj(o0, (x2, c2), mods0, even_w_out[0].astype(BF16), row(mix_post_g[0]),
                                 row(ffn_pre_g[0]), router_w[0], row(router_b[0]), nb, True)
    x1 = _moe(hf, mi, mw, seg, xn, mods0, row(ffn_post_g[0]), 0, moe_w1, moe_b1, moe_w2, moe_b2, nb, True)

    q, k, v = _proj1(x1, mods1, row(mix_pre_g[1]), na_w_qkv[0].astype(BF16), nb)
    o1 = _na(q, k, v, _na_bias(na_rpb[0]), nb)
    xn, hf, mi, mw, seg = _oproj(o1, (x1,), mods1, na_w_out[0].astype(BF16), row(mix_post_g[1]),
                                 row(ffn_pre_g[1]), router_w[1], row(router_b[1]), nb, False)
    out = _moe(hf, mi, mw, seg, xn, mods1, row(ffn_post_g[1]), 1, moe_w1, moe_b1, moe_w2, moe_b2, nb, False)
    return out.reshape(nb, SEQ, D)
```

```python
import functools
import math

import numpy as np
import jax
import jax.numpy as jnp
from jax import lax
from jax.experimental import pallas as pl
from jax.experimental.pallas import tpu as pltpu

F32 = jnp.float32
BF16 = jnp.bfloat16

D = 1024
SEQ = 2048
CTX = 256
GRID_W = 64
GRID_H = SEQ // GRID_W
TILE = 256
LAT_TILES = SEQ // TILE
SMP_TILES = LAT_TILES + 1
SMP_ROWS = SEQ + CTX
EPS = 1e-6
ROPE_BASE = 10000.0

DIFF_HEADS = 4
DIFF_DH = 64
DIFF_SCALE = DIFF_DH ** -0.5
MLA_HEADS = 4
MLA_Q_LORA = 256
MLA_KV_LORA = 128
MLA_NOPE = 128
MLA_ROPE = 64
MLA_V = 128
MLA_SCALE = (MLA_NOPE + MLA_ROPE) ** -0.5
MLA_QK_PAD = 256
NA_HEADS = 16
NA_DH = 64
NA_SCALE = NA_DH ** -0.5
WIN_ROWS = 8
WIN_COLS = 16
N_EXPERTS = 32
TOP_K = 4
D_FF = 1024
SWIGLU_ALPHA = 1.702
SWIGLU_LIMIT = 7.0
EXPERT_BLOCK = 512
SEG_ALIGN = 8
SORT_ROWS = -(-(TILE * TOP_K + N_EXPERTS * (SEG_ALIGN - 1)) // 256) * 256
SORT_ROWS_POW2 = 1 << (SORT_ROWS.bit_length() - 1)
MASK_VALUE = -1e30
LOG2_E = math.log2(math.e)
META_LANES = 128
VMEM_LIMIT = 60 * 1024 * 1024

_NT = (((1,), (1,)), ((), ()))


def _dot(a, b):
    return jnp.dot(a, b, preferred_element_type=F32)


def _dot_nt(a, b):
    return lax.dot_general(a, b, _NT, preferred_element_type=F32)


def _dot_f32(a, b):
    return jnp.dot(a, b, preferred_element_type=F32, precision=lax.Precision.HIGHEST)


def _rms(x, g):
    return x * lax.rsqrt(jnp.mean(x * x, axis=-1, keepdims=True) + EPS) * g


def _mod(mod_ref, k):
    return mod_ref[0, :, k * D:(k + 1) * D]


def _params(sem):
    return pltpu.CompilerParams(dimension_semantics=sem, vmem_limit_bytes=VMEM_LIMIT)


def _ada_kernel(c_ref, w_ref, b_ref, o_ref):
    c = c_ref[...]
    s = c * jax.nn.sigmoid(c)
    o_ref[0] = _dot_f32(s, w_ref[0]) + b_ref[0]


def _ada(cc, ada_w, ada_b):
    depth = ada_w.shape[0]
    n = cc.shape[0]
    nt = 6 * D // D
    return pl.pallas_call(
        _ada_kernel,
        grid=(depth, nt),
        in_specs=[pl.BlockSpec((n, D), lambda l, j: (0, 0)),
                  pl.BlockSpec((1, D, D), lambda l, j: (l, 0, j)),
                  pl.BlockSpec((1, 1, D), lambda l, j: (l, 0, j))],
        out_specs=pl.BlockSpec((1, n, D), lambda l, j: (l, 0, j)),
        out_shape=jax.ShapeDtypeStruct((depth, n, 6 * D), F32),
        compiler_params=_params(("arbitrary", "arbitrary")),
        name="ada",
    )(cc, ada_w, ada_b.reshape(depth, 1, 6 * D))


def _x_lat_map(i):
    return ((i // SMP_TILES) * LAT_TILES + jnp.minimum(i % SMP_TILES, LAT_TILES - 1), 0)


def _x_ctx_map(i):
    return (i // SMP_TILES, 0)


def _mod_map_all(nb):
    return lambda i: (jnp.where(i % SMP_TILES == LAT_TILES, nb, i // SMP_TILES), 0, 0)


def _lat_of_all(i):
    return (i // LAT_TILES) * SMP_TILES + i % LAT_TILES


def _rope(x, cos, sa, sb):
    return x * cos + pltpu.roll(x, 112, 1) * sa + pltpu.roll(x, 16, 1) * sb


def _proj0_kernel(x_ref, c_ref, mod_ref, g_ref, wa_ref, qng_ref, wqb_ref, kvng_ref, wkvb_ref,
                  cos_ref, sa_ref, sb_ref, qd_ref, kd_ref, vd_ref, qm_ref, km_ref, vm_ref):
    is_ctx = pl.program_id(0) % SMP_TILES == LAT_TILES
    nd = DIFF_HEADS * 2 * DIFF_DH
    o = 3 * nd
    halves = (slice(0, TILE // 2), slice(TILE // 2, TILE))
    bigs = []
    for rs in halves:
        x = jnp.where(is_ctx, c_ref[rs, :], x_ref[rs, :])
        h = (_rms(x, g_ref[...]) * (1.0 + _mod(mod_ref, 1)) + _mod(mod_ref, 0)).astype(BF16)
        bigs.append(_dot(h, wa_ref[...]))
    for rs, big in zip(halves, bigs):
        cos, sa, sb = cos_ref[rs, :], sa_ref[rs, :], sb_ref[rs, :]
        for j in range(nd // 128):
            sl = slice(j * 128, (j + 1) * 128)
            qd_ref[rs, sl] = _rope(big[:, j * 128:(j + 1) * 128], cos, sa, sb).astype(BF16)
            kd_ref[rs, sl] = _rope(big[:, nd + j * 128:nd + (j + 1) * 128], cos, sa, sb).astype(BF16)
        vd_ref[rs, :] = big[:, 2 * nd:3 * nd].astype(BF16)
        cq = big[:, o:o + MLA_Q_LORA]
        ckv = big[:, o + MLA_Q_LORA:o + MLA_Q_LORA + MLA_KV_LORA]
        kpe = _rope(big[:, o + MLA_Q_LORA + MLA_KV_LORA:], cos, sa, sb).astype(BF16)
        qm = _dot(_rms(cq, qng_ref[...]).astype(BF16), wqb_ref[...])
        kv = _dot(_rms(ckv, kvng_ref[...]).astype(BF16), wkvb_ref[...])
        for hd in range(MLA_HEADS):
            b0 = hd * MLA_QK_PAD
            qm_ref[rs, b0:b0 + 128] = qm[:, b0:b0 + 128].astype(BF16)
            qm_ref[rs, b0 + 128:b0 + 256] = _rope(qm[:, b0 + 128:b0 + 256], cos, sa, sb).astype(BF16)
            km_ref[rs, b0:b0 + 128] = kv[:, hd * 256:hd * 256 + 128].astype(BF16)
            km_ref[rs, b0 + 128:b0 + 256] = kpe
            vm_ref[rs, hd * 128:(hd + 1) * 128] = kv[:, hd * 256 + 128:(hd + 1) * 256].astype(BF16)


def _rope_tables():
    t = np.arange(SEQ)
    half = 16
    inv = (ROPE_BASE ** (-np.arange(half, dtype=np.float32) * 2.0 / (2 * half))).astype(np.float32)
    ang_r = (t // GRID_W).astype(np.float32)[:, None] * inv[None, :]
    ang_c = (t % GRID_W).astype(np.float32)[:, None] * inv[None, :]
    cr, sr, cc, sc = np.cos(ang_r), np.sin(ang_r), np.cos(ang_c), np.sin(ang_c)
    z = np.zeros_like(cr)
    cos = np.concatenate([cr, cr, cc, cc], axis=1)
    sa = np.concatenate([-sr, z, -sc, z], axis=1)
    sb = np.concatenate([z, sr, z, sc], axis=1)

    def full(tab, fill):
        tab = np.tile(tab, (1, 2))
        ident = np.full((CTX, 128), fill, np.float32)
        return jnp.asarray(np.concatenate([tab, ident], axis=0).astype(np.float32))

    return full(cos, 1.0), full(sa, 0.0), full(sb, 0.0)


def _proj0(x2, c2, mods, g, wa, qng, wqb, kvng, wkvb, nb):
    n_tiles = nb * SMP_TILES
    rows = nb * SMP_ROWS
    cos, sa, sb = _rope_tables()
    tile_map = lambda i: (i, 0)
    const = lambda i: (0, 0)
    tab_spec = pl.BlockSpec((TILE, 128), lambda i: (i % SMP_TILES, 0))
    widths = (512, 512, 512, 1024, 1024, 512)
    return pl.pallas_call(
        _proj0_kernel,
        grid=(n_tiles,),
        in_specs=[pl.BlockSpec((TILE, D), _x_lat_map),
                  pl.BlockSpec((TILE, D), _x_ctx_map),
                  pl.BlockSpec((1, 1, 6 * D), _mod_map_all(nb)),
                  pl.BlockSpec((1, D), const),
                  pl.BlockSpec(wa.shape, const),
                  pl.BlockSpec((1, MLA_Q_LORA), const),
                  pl.BlockSpec(wqb.shape, const),
                  pl.BlockSpec((1, MLA_KV_LORA), const),
                  pl.BlockSpec(wkvb.shape, const),
                  tab_spec, tab_spec, tab_spec],
        out_specs=[pl.BlockSpec((TILE, w), tile_map) for w in widths],
        out_shape=[jax.ShapeDtypeStruct((rows, w), BF16) for w in widths],
        compiler_params=_params(("arbitrary",)),
        name="proj_even",
    )(x2, c2, mods, g, wa, qng, wqb, kvng, wkvb, cos, sa, sb)


def _softmax_parts(s, scale):
    c = scale * LOG2_E
    m = jnp.max(s, axis=-1, keepdims=True)
    p = jnp.exp2(s * c - m * c)
    return p, jnp.sum(p, axis=-1, keepdims=True)


def _attn0_kernel(lam_init, qd_ref, kd_ref, vd_ref, qm_ref, km_ref, vm_ref, lam_ref, sg_ref, o_ref):
    lv = lam_ref[...]
    lam = (jnp.exp(jnp.sum(lv[0:1] * lv[1:2], axis=1, keepdims=True))
           - jnp.exp(jnp.sum(lv[2:3] * lv[3:4], axis=1, keepdims=True)) + lam_init)
    lane = lax.broadcasted_iota(jnp.int32, (TILE, 128), 1)

    def heads(k_lo, nk):
        def diff_scores(hd):
            sl = slice(hd * 128, (hd + 1) * 128)
            q = qd_ref[:, sl]
            zero = jnp.zeros_like(q)
            qq = jnp.concatenate([jnp.where(lane < DIFF_DH, q, zero),
                                  jnp.where(lane >= DIFF_DH, q, zero)], axis=0)
            return _dot_nt(qq, kd_ref[k_lo:k_lo + nk, sl])

        def diff_finish(hd, s):
            sl = slice(hd * 128, (hd + 1) * 128)
            p, l = _softmax_parts(s, DIFF_SCALE)
            pv = _dot(p.astype(BF16), vd_ref[k_lo:k_lo + nk, sl])
            o = pv[:TILE] / l[:TILE] - lam * (pv[TILE:] / l[TILE:])
            o_ref[:, sl] = (_rms(o, sg_ref[...]) * (1.0 - lam_init)).astype(BF16)

        def mla_scores(hd):
            ql = slice(hd * MLA_QK_PAD, (hd + 1) * MLA_QK_PAD)
            return _dot_nt(qm_ref[:, ql], km_ref[k_lo:k_lo + nk, ql])

        def mla_finish(hd, s):
            p, l = _softmax_parts(s, MLA_SCALE)
            pv = _dot(p.astype(BF16), vm_ref[k_lo:k_lo + nk, hd * 128:(hd + 1) * 128])
            o_ref[:, 512 + hd * 128:512 + (hd + 1) * 128] = (pv / l).astype(BF16)

        jobs = ([(diff_scores, diff_finish, hd) for hd in range(DIFF_HEADS)]
                + [(mla_scores, mla_finish, hd) for hd in range(MLA_HEADS)])
        s_next = jobs[0][0](jobs[0][2])
        for n, (_, finish, hd) in enumerate(jobs):
            s = s_next
            if n + 1 < len(jobs):
                s_next = jobs[n + 1][0](jobs[n + 1][2])
            finish(hd, s)

    is_ctx = pl.program_id(1) == LAT_TILES

    @pl.when(jnp.logical_not(is_ctx))
    def _():
        heads(0, SMP_ROWS)

    @pl.when(is_ctx)
    def _():
        heads(SEQ, CTX)


def _attn0(qd, kd, vd, qm, km, vm, lam_vec, subln_g, lam_init, nb):
    q_map = lambda b, p: (b * SMP_TILES + p, 0)
    k_map = lambda b, p: (b, 0)
    const = lambda b, p: (0, 0)
    return pl.pallas_call(
        functools.partial(_attn0_kernel, lam_init),
        grid=(nb, SMP_TILES),
        in_specs=[pl.BlockSpec((TILE, 512), q_map),
                  pl.BlockSpec((SMP_ROWS, 512), k_map),
                  pl.BlockSpec((SMP_ROWS, 512), k_map),
                  pl.BlockSpec((TILE, 1024), q_map),
                  pl.BlockSpec((SMP_ROWS, 1024), k_map),
                  pl.BlockSpec((SMP_ROWS, 512), k_map),
                  pl.BlockSpec(lam_vec.shape, const),
                  pl.BlockSpec((1, 128), const)],
        out_specs=pl.BlockSpec((TILE, D), q_map),
        out_shape=jax.ShapeDtypeStruct((nb * SMP_ROWS, D), BF16),
        compiler_params=_params(("arbitrary", "arbitrary")),
        name="attn_even",
    )(qd, kd, vd, qm, km, vm, lam_vec, subln_g)


def _oproj_kernel(dual, *refs):
    if dual:
        (o_ref, xl_ref, xc_ref, mod_ref, wo_ref, pg_ref, fg_ref, rw_ref, rb_ref,
         xn_ref, hf_ref, mi_ref, mw_ref, seg_ref) = refs
        is_ctx = pl.program_id(0) % SMP_TILES == LAT_TILES
        x = jnp.where(is_ctx, xc_ref[...], xl_ref[...])
    else:
        (o_ref, x_ref, mod_ref, wo_ref, pg_ref, fg_ref, rw_ref, rb_ref,
         xn_ref, hf_ref, mi_ref, mw_ref, seg_ref) = refs
        x = x_ref[...]

    his, los = [], []
    for rs in (slice(0, TILE // 2), slice(TILE // 2, TILE)):
        y = _dot(o_ref[rs, :], wo_ref[...])
        xn = x[rs] + _mod(mod_ref, 2) * _rms(y, pg_ref[...])
        xn_ref[rs, :] = xn
        hf = _rms(xn, fg_ref[...]) * (1.0 + _mod(mod_ref, 4)) + _mod(mod_ref, 3)
        hi = hf.astype(BF16)
        hf_ref[rs, :] = hi
        his.append(hi)
        los.append((hf - hi.astype(F32)).astype(BF16))

    lt = _dot_nt(rw_ref[...], jnp.concatenate(his + los, axis=0))
    ne = N_EXPERTS
    logits = (lt[:ne, :TILE] + lt[ne:, :TILE]) + (lt[:ne, TILE:] + lt[ne:, TILE:]) + rb_ref[...]

    eid = lax.broadcasted_iota(jnp.int32, (ne, TILE), 0).astype(F32)
    sels, vals = [], []
    for _ in range(TOP_K):
        m = jnp.max(logits, axis=0, keepdims=True)
        idx = jnp.min(jnp.where(logits == m, eid, float(ne)), axis=0, keepdims=True)
        sel = eid == idx
        sels.append(sel)
        vals.append(m)
        logits = jnp.where(sel, -jnp.inf, logits)
    ex = [jnp.exp(v - vals[0]) for v in vals]
    den = ex[0] + ex[1] + ex[2] + ex[3]

    onehot = (sels[0] | sels[1] | sels[2] | sels[3]).astype(F32)
    r_i = lax.broadcasted_iota(jnp.int32, (TILE, TILE), 0)
    c_i = lax.broadcasted_iota(jnp.int32, (TILE, TILE), 1)
    before = _dot(onehot.astype(BF16), (r_i < c_i).astype(BF16))
    cnt = jnp.sum(onehot, axis=1, keepdims=True)
    seg_len = jnp.floor((cnt + (SEG_ALIGN - 1)) * (1.0 / SEG_ALIGN)) * SEG_ALIGN
    e_r = lax.broadcasted_iota(jnp.int32, (ne, ne), 0)
    e_c = lax.broadcasted_iota(jnp.int32, (ne, ne), 1)
    seg_len_b = jnp.broadcast_to(seg_len, (ne, META_LANES))
    seg_off = _dot_f32((e_c < e_r).astype(F32), seg_len_b)[:, 0:1]
    pos = before + seg_off
    sub = lax.broadcasted_iota(jnp.int32, (8, TILE), 0)
    mi = jnp.zeros((8, TILE), jnp.int32)
    mw = jnp.zeros((8, TILE), F32)
    for k in range(TOP_K):
        row = jnp.sum(jnp.where(sels[k], pos, 0.0), axis=0, keepdims=True).astype(jnp.int32)
        mi = jnp.where(sub == k, row, mi)
        mw = jnp.where(sub == k, ex[k] / den, mw)
    mi_ref[0] = mi
    mw_ref[0] = mw
    seg_ref[0] = seg_len_b.astype(jnp.int32)


def _oproj(o, x_args, mods, wo, pg, fg, rw, rb, nb, all_tokens):
    const = lambda i: (0, 0)
    tile_map = lambda i: (i, 0)
    tile3_map = lambda i: (i, 0, 0)
    rw_hi = rw.astype(BF16)
    rw_lo = (rw - rw_hi.astype(F32)).astype(BF16)
    rw_split = jnp.concatenate([rw_hi.T, rw_lo.T], axis=0)
    if all_tokens:
        n_tiles = nb * SMP_TILES
        x_specs = [pl.BlockSpec((TILE, D), _x_lat_map), pl.BlockSpec((TILE, D), _x_ctx_map)]
        mod_map = _mod_map_all(nb)
    else:
        n_tiles = nb * LAT_TILES
        x_specs = [pl.BlockSpec((TILE, D), lambda i: (_lat_of_all(i), 0))]
        mod_map = lambda i: (i // LAT_TILES, 0, 0)
    rows = n_tiles * TILE
    return pl.pallas_call(
        functools.partial(_oproj_kernel, all_tokens),
        grid=(n_tiles,),
        in_specs=[pl.BlockSpec((TILE, D), tile_map)] + x_specs + [
            pl.BlockSpec((1, 1, 6 * D), mod_map),
            pl.BlockSpec((D, D), const),
            pl.BlockSpec((1, D), const),
            pl.BlockSpec((1, D), const),
            pl.BlockSpec((2 * N_EXPERTS, D), const),
            pl.BlockSpec((N_EXPERTS, 1), const)],
        out_specs=[pl.BlockSpec((TILE, D), tile_map),
                   pl.BlockSpec((TILE, D), tile_map),
                   pl.BlockSpec((1, 8, TILE), tile3_map),
                   pl.BlockSpec((1, 8, TILE), tile3_map),
                   pl.BlockSpec((1, N_EXPERTS, META_LANES), tile3_map)],
        out_shape=[jax.ShapeDtypeStruct((rows, D), F32),
                   jax.ShapeDtypeStruct((rows, D), BF16),
                   jax.ShapeDtypeStruct((n_tiles, 8, TILE), jnp.int32),
                   jax.ShapeDtypeStruct((n_tiles, 8, TILE), F32),
                   jax.ShapeDtypeStruct((n_tiles, N_EXPERTS, META_LANES), jnp.int32)],
        compiler_params=_params(("arbitrary",)),
        name="oproj_router",
    )(o, *x_args, mods, wo, pg, fg, rw_split, rb.reshape(N_EXPERTS, 1))


U32 = jnp.uint32
HALF = D // 2


def _pack_rows(x):
    bits = pltpu.bitcast(x.astype(BF16).astype(F32), U32)
    return (bits[:, :HALF] >> 16) | (bits[:, HALF:] & jnp.uint32(0xFFFF0000))


def _unpack_rows(w):
    lo = pltpu.bitcast(w << 16, F32).astype(BF16)
    hi = pltpu.bitcast(w & jnp.uint32(0xFFFF0000), F32).astype(BF16)
    return jnp.concatenate([lo, hi], axis=1)


def _for_pieces(n, largest, fn):
    piece = largest
    while piece >= SEG_ALIGN:
        done = (n // (2 * piece)) * (2 * piece)
        pl.when((n & piece) != 0)(functools.partial(fn, done, piece))
        piece //= 2


PIECES = tuple(TILE >> b for b in range((TILE // SEG_ALIGN).bit_length()))
N_PIECES = len(PIECES)
LIST_ROWS = 2 * N_PIECES + 1


def _piece_lists(seg_len, seg_local, seg_first):
    sizes = jnp.asarray(PIECES, jnp.int32)[None, :, None]
    n = seg_len[:, None, :]
    active = (n & sizes) != 0
    done = n // (2 * sizes) * (2 * sizes)
    slot = jnp.cumsum(active, axis=-1) - 1
    put = active[..., :, None] & (slot[..., :, None] == jnp.arange(N_EXPERTS))
    compact = lambda v: jnp.sum(jnp.where(put, v[..., :, None], 0), axis=-2)
    loc = compact(seg_local[:, None, :] + done)
    glb = compact(seg_first[:, None, :] + done)
    counts = jnp.sum(active, axis=-1)
    tail = jnp.concatenate([counts, jnp.sum(seg_len, axis=-1, keepdims=True)], axis=-1)
    tail = jnp.pad(tail, ((0, 0), (0, N_EXPERTS - tail.shape[-1])))[:, None, :]
    return jnp.concatenate([loc, glb, tail], axis=1).astype(jnp.int32)


STEP_TILES = 2


def _start_segment_copies(list_ref, u, local_ref, hbm_ref, sem, to_hbm):
    for b, rows in enumerate(PIECES):
        def start(j, carry, b=b, rows=rows):
            loc = local_ref.at[pl.ds(pl.multiple_of(list_ref[u, b, j], SEG_ALIGN), rows)]
            glb = hbm_ref.at[pl.ds(pl.multiple_of(list_ref[u, N_PIECES + b, j], SEG_ALIGN), rows)]
            (pltpu.make_async_copy(loc, glb, sem) if to_hbm else pltpu.make_async_copy(glb, loc, sem)).start()
            return carry

        lax.fori_loop(0, list_ref[u, 2 * N_PIECES, b], start, 0)


def _wait_segment_copies(list_ref, u, local_ref, hbm_ref, sem, to_hbm):
    def wait(done, rows):
        del done
        loc = local_ref.at[pl.ds(0, rows)]
        glb = hbm_ref.at[pl.ds(0, rows)]
        (pltpu.make_async_copy(loc, glb, sem) if to_hbm else pltpu.make_async_copy(glb, loc, sem)).wait()

    _for_pieces(list_ref[u, 2 * N_PIECES, N_PIECES], SORT_ROWS_POW2, wait)


def _zero_fill_copies(tail_ref, free_ref, zbuf, xs_ref, sem, action):
    def per_expert(e, carry):
        def piece_copy(done, rows):
            dst = xs_ref.at[pl.ds(pl.multiple_of(tail_ref[0, e] + done, SEG_ALIGN), rows)]
            action(pltpu.make_async_copy(zbuf.at[pl.ds(0, rows)], dst, sem))

        _for_pieces(tail_ref[1, e], EXPERT_BLOCK // 2, piece_copy)
        return carry

    def per_block(j, carry):
        first = pl.multiple_of(free_ref[0] + j * EXPERT_BLOCK, EXPERT_BLOCK)
        action(pltpu.make_async_copy(zbuf, xs_ref.at[pl.ds(first, EXPERT_BLOCK)], sem))
        return carry

    lax.fori_loop(0, N_EXPERTS, per_expert, 0)
    lax.fori_loop(0, free_ref[1], per_block, 0)


def _dispatch_kernel(seg_ref, seg_prev_ref, tail_ref, free_ref, lp_ref, h_ref, xs_ref, sbuf, zbuf, sem):
    i = pl.program_id(0)
    last = pl.num_programs(0) - 1
    slot = i % 2
    row = lax.broadcasted_iota(jnp.int32, (SORT_ROWS, TILE), 0)
    tiles = range(STEP_TILES)
    for u in tiles:
        lp = lp_ref[u]
        hit = row == lp[0:1]
        for k in range(1, TOP_K):
            hit = hit | (row == lp[k:k + 1])
        sbuf[slot, u] = _pack_rows(_dot(jnp.where(hit, 1.0, 0.0).astype(BF16),
                                        h_ref[u * TILE:(u + 1) * TILE, :]))
    for u in tiles:
        _start_segment_copies(seg_ref, u, sbuf.at[slot, u], xs_ref, sem.at[slot], True)

    @pl.when(i > 0)
    def _():
        for u in tiles:
            _wait_segment_copies(seg_prev_ref, u, sbuf.at[1 - slot, u], xs_ref, sem.at[1 - slot], True)

    @pl.when(i == last)
    def _():
        for u in tiles:
            _wait_segment_copies(seg_ref, u, sbuf.at[slot, u], xs_ref, sem.at[slot], True)
        zbuf[...] = jnp.zeros_like(zbuf)
        _zero_fill_copies(tail_ref, free_ref, zbuf, xs_ref, sem.at[slot], lambda cp: cp.start())
        _zero_fill_copies(tail_ref, free_ref, zbuf, xs_ref, sem.at[slot], lambda cp: cp.wait())


def _dispatch(segs, tails, free, lpos_t, hf, m_rows):
    n_steps = hf.shape[0] // (STEP_TILES * TILE)
    seg_block = (STEP_TILES, LIST_ROWS, N_EXPERTS)
    return pl.pallas_call(
        _dispatch_kernel,
        grid=(n_steps,),
        in_specs=[pl.BlockSpec(seg_block, lambda i: (i, 0, 0), memory_space=pltpu.SMEM),
                  pl.BlockSpec(seg_block, lambda i: (jnp.maximum(i - 1, 0), 0, 0), memory_space=pltpu.SMEM),
                  pl.BlockSpec(memory_space=pltpu.SMEM),
                  pl.BlockSpec(memory_space=pltpu.SMEM),
                  pl.BlockSpec((STEP_TILES, 8, TILE), lambda i: (i, 0, 0)),
                  pl.BlockSpec((STEP_TILES * TILE, D), lambda i: (i, 0))],
        out_specs=pl.BlockSpec(memory_space=pl.ANY),
        out_shape=jax.ShapeDtypeStruct((m_rows, HALF), U32),
        scratch_shapes=[pltpu.VMEM((2, STEP_TILES, SORT_ROWS, HALF), U32),
                        pltpu.VMEM((EXPERT_BLOCK, HALF), U32), pltpu.SemaphoreType.DMA((2,))],
        compiler_params=_params(("arbitrary",)),
        name="moe_dispatch",
    )(segs, segs, tails, free, lpos_t, hf)


def _expert_kernel(be_ref, nu_ref, xs_ref, w1_ref, b1_ref, w2_ref, b2_ref, ys_ref, w1b, w2b):
    i = pl.program_id(0)
    used = i < nu_ref[0]
    new_expert = jnp.logical_or(i == 0, be_ref[i] != be_ref[jnp.maximum(i - 1, 0)])

    @pl.when(jnp.logical_and(used, new_expert))
    def _():
        w1b[...] = w1_ref[0, 0].astype(BF16)
        w2b[...] = w2_ref[0, 0].astype(BF16)

    @pl.when(used)
    def _():
        gu = _dot(_unpack_rows(xs_ref[...]), w1b[...]) + b1_ref[0, 0]
        gate = jnp.minimum(gu[:, :D_FF], SWIGLU_LIMIT)
        lin = jnp.clip(gu[:, D_FF:], -SWIGLU_LIMIT, SWIGLU_LIMIT)
        act = gate * jax.nn.sigmoid(SWIGLU_ALPHA * gate) * (lin + 1.0)
        ys_ref[...] = _pack_rows(_dot(act.astype(BF16), w2b[...]) + b2_ref[0, 0])

    @pl.when(i >= nu_ref[0])
    def _():
        ys_ref[...] = jnp.zeros_like(ys_ref)


def _experts(blk_e, n_used, xs, layer, w1, b1, w2, b2):
    n_blocks = xs.shape[0] // EXPERT_BLOCK
    in_row_map = lambda i, be, nu: (jnp.minimum(i, nu[0] - 1), 0)
    e_map = lambda i, be, nu: (layer, be[i], 0, 0)
    grid_spec = pltpu.PrefetchScalarGridSpec(
        num_scalar_prefetch=2,
        grid=(n_blocks,),
        in_specs=[pl.BlockSpec((EXPERT_BLOCK, HALF), in_row_map),
                  pl.BlockSpec((1, 1, D, 2 * D_FF), e_map),
                  pl.BlockSpec((1, 1, 1, 2 * D_FF), e_map),
                  pl.BlockSpec((1, 1, D_FF, D), e_map),
                  pl.BlockSpec((1, 1, 1, D), e_map)],
        out_specs=pl.BlockSpec((EXPERT_BLOCK, HALF), lambda i, be, nu: (i, 0)),
        scratch_shapes=[pltpu.VMEM((D, 2 * D_FF), BF16), pltpu.VMEM((D_FF, D), BF16)])
    return pl.pallas_call(
        _expert_kernel,
        grid_spec=grid_spec,
        out_shape=jax.ShapeDtypeStruct(xs.shape, U32),
        compiler_params=_params(("arbitrary",)),
        name="moe_experts",
    )(blk_e, n_used, xs, w1, b1[:, :, None, :], w2, b2[:, :, None, :])


def _combine_kernel(seg_ref, seg_next_ref, mi_ref, mw_ref, x_ref, g_ref, ys_ref, *rest):
    mod_refs, (o_ref, ybuf, sem) = rest[:STEP_TILES], rest[STEP_TILES:]
    i = pl.program_id(0)
    slot = i % 2
    tiles = range(STEP_TILES)

    @pl.when(i == 0)
    def _():
        ybuf[...] = jnp.zeros_like(ybuf)
        for u in tiles:
            _start_segment_copies(seg_ref, u, ybuf.at[0, u], ys_ref, sem.at[0], False)

    @pl.when(i + 1 < pl.num_programs(0))
    def _():
        for u in tiles:
            _start_segment_copies(seg_next_ref, u, ybuf.at[1 - slot, u], ys_ref, sem.at[1 - slot], False)

    col = lax.broadcasted_iota(jnp.int32, (TILE, SORT_ROWS), 1)
    wms = []
    for u in tiles:
        rs = slice(u * TILE, (u + 1) * TILE)
        mi = mi_ref[rs, :]
        mw = mw_ref[rs, :]
        wm = jnp.zeros((TILE, SORT_ROWS), F32)
        for k in range(TOP_K):
            wm = jnp.where(col == mi[:, k:k + 1], mw[:, k:k + 1], wm)
        wms.append(wm.astype(BF16))
    for u in tiles:
        _wait_segment_copies(seg_ref, u, ybuf.at[slot, u], ys_ref, sem.at[slot], False)
    for u in tiles:
        rs = slice(u * TILE, (u + 1) * TILE)
        y = _dot(wms[u], _unpack_rows(ybuf[slot, u]))
        o_ref[rs, :] = x_ref[rs, :] + _mod(mod_refs[u], 5) * _rms(y, g_ref[...])


def _combine(segs, mi, mw, xn, mods, g, ys, nb, all_tokens):
    n_steps = xn.shape[0] // (STEP_TILES * TILE)
    step_map = lambda i: (i, 0)
    mod_map = _mod_map_all(nb) if all_tokens else (lambda i: (i // LAT_TILES, 0, 0))
    seg_block = (STEP_TILES, LIST_ROWS, N_EXPERTS)
    rows = STEP_TILES * TILE
    mod_specs = [pl.BlockSpec((1, 1, 6 * D), lambda i, u=u: mod_map(i * STEP_TILES + u))
                 for u in range(STEP_TILES)]
    return pl.pallas_call(
        _combine_kernel,
        grid=(n_steps,),
        in_specs=[pl.BlockSpec(seg_block, lambda i: (i, 0, 0), memory_space=pltpu.SMEM),
                  pl.BlockSpec(seg_block, lambda i: (jnp.minimum(i + 1, n_steps - 1), 0, 0),
                               memory_space=pltpu.SMEM),
                  pl.BlockSpec((rows, 8), step_map),
                  pl.BlockSpec((rows, 8), step_map),
                  pl.BlockSpec((rows, D), step_map),
                  pl.BlockSpec((1, D), lambda i: (0, 0)),
                  pl.BlockSpec(memory_space=pl.ANY)] + mod_specs,
        out_specs=pl.BlockSpec((rows, D), step_map),
        out_shape=jax.ShapeDtypeStruct(xn.shape, F32),
        scratch_shapes=[pltpu.VMEM((2, STEP_TILES, SORT_ROWS, HALF), U32), pltpu.SemaphoreType.DMA((2,))],
        compiler_params=_params(("arbitrary",)),
        name="moe_combine",
    )(segs, segs, mi, mw, xn, g, ys, *([mods] * STEP_TILES))


def _moe(hf, mi, mw, seg, xn, mods, g, layer, w1, b1, w2, b2, nb, all_tokens):
    t = hf.shape[0]
    n_tiles = t // TILE
    assert n_tiles % STEP_TILES == 0, "dispatch / combine take STEP_TILES token tiles per grid step"
    rows_max = t * TOP_K + n_tiles * N_EXPERTS * (SEG_ALIGN - 1)
    n_blocks = -(-rows_max // EXPERT_BLOCK) + N_EXPERTS
    seg_len = seg[:, :, 0]
    counts = jnp.sum(seg_len, axis=0)
    padded = (counts + EXPERT_BLOCK - 1) // EXPERT_BLOCK * EXPERT_BLOCK
    pad_end = jnp.cumsum(padded)
    pad_start = pad_end - padded
    seg_first = pad_start[None, :] + jnp.cumsum(seg_len, axis=0) - seg_len
    seg_local = jnp.cumsum(seg_len, axis=1) - seg_len
    segs = _piece_lists(seg_len, seg_local, seg_first)
    by_token = lambda a: a.transpose(0, 2, 1).reshape(t, 8)
    blk_row = jnp.arange(n_blocks, dtype=jnp.int32) * EXPERT_BLOCK
    blk_e = jnp.minimum(jnp.sum(pad_end[None, :] <= blk_row[:, None], axis=1), N_EXPERTS - 1).astype(jnp.int32)
    n_used = (pad_end[-1:] // EXPERT_BLOCK).astype(jnp.int32)
    tails = jnp.stack([pad_start + counts, padded - counts]).astype(jnp.int32)
    free = jnp.concatenate([pad_end[-1:], n_blocks - n_used]).astype(jnp.int32)
    xs = _dispatch(segs, tails, free, mi, hf, n_blocks * EXPERT_BLOCK)
    ys = _experts(blk_e, n_used, xs, layer, w1, b1, w2, b2)
    return _combine(segs, by_token(mi), by_token(mw), xn, mods, g, ys, nb, all_tokens)


def _proj1_kernel(x_ref, mod_ref, g_ref, w_ref, q_ref, k_ref, v_ref):
    h = (_rms(x_ref[...], g_ref[...]) * (1.0 + _mod(mod_ref, 1)) + _mod(mod_ref, 0)).astype(BF16)
    qkv = _dot(h, w_ref[...])
    q_ref[...] = (qkv[:, :D] * NA_SCALE).astype(BF16)
    k_ref[...] = qkv[:, D:2 * D].astype(BF16)
    v_ref[...] = qkv[:, 2 * D:].astype(BF16)


def _proj1(x1, mods, g, w, nb):
    n_tiles = nb * SMP_TILES
    tile_map = lambda i: (i, 0)
    return pl.pallas_call(
        _proj1_kernel,
        grid=(n_tiles,),
        in_specs=[pl.BlockSpec((TILE, D), tile_map),
                  pl.BlockSpec((1, 1, 6 * D), _mod_map_all(nb)),
                  pl.BlockSpec((1, D), lambda i: (0, 0)),
                  pl.BlockSpec((D, 3 * D), lambda i: (0, 0))],
        out_specs=[pl.BlockSpec((TILE, D), tile_map)] * 3,
        out_shape=[jax.ShapeDtypeStruct((n_tiles * TILE, D), BF16)] * 3,
        compiler_params=_params(("arbitrary",)),
        name="proj_odd",
    )(x1, mods, g, w)


def _na_row_start(r):
    return jnp.clip(r - WIN_ROWS // 2, 0, GRID_H - WIN_ROWS)


NA_STEP_ROWS = 4


def _na_kernel(q_ref, k_ref, v_ref, *rest):
    bias_refs, o_ref = rest[:NA_STEP_ROWS], rest[NA_STEP_ROWS]
    n_loc = WIN_ROWS * GRID_W
    lane = lax.broadcasted_iota(jnp.int32, (GRID_W, 128), 1)
    pairs = range(NA_HEADS // 2)
    sls = [slice(j * 128, (j + 1) * 128) for j in pairs]
    k0s, qqs = [], []
    for t in range(NA_STEP_ROWS):
        r = pl.program_id(1) * NA_STEP_ROWS + t
        k0s.append(pl.multiple_of(_na_row_start(r) * GRID_W, GRID_W))
        for j in pairs:
            q = q_ref[t * GRID_W:(t + 1) * GRID_W, sls[j]]
            zero = jnp.zeros_like(q)
            qqs.append(jnp.concatenate([jnp.where(lane < NA_DH, q, zero),
                                        jnp.where(lane >= NA_DH, q, zero)], axis=0))
    jobs = [(t, j) for t in range(NA_STEP_ROWS) for j in pairs]
    s_loc = jnp.concatenate([_dot_nt(qqs[n], k_ref[pl.ds(k0s[t], n_loc), sls[j]])
                             for n, (t, j) in enumerate(jobs)], axis=0)
    s_loc = s_loc + jnp.concatenate([b[0] for b in bias_refs], axis=0)
    s_ctx = jnp.concatenate([_dot_nt(qqs[n], k_ref[SEQ:, sls[j]]) for n, (t, j) in enumerate(jobs)], axis=0)
    m = jnp.maximum(jnp.max(s_loc, axis=-1, keepdims=True), jnp.max(s_ctx, axis=-1, keepdims=True))
    p_loc = jnp.exp(s_loc - m)
    p_ctx = jnp.exp(s_ctx - m)
    inv_l = 1.0 / (jnp.sum(p_loc, axis=-1, keepdims=True) + jnp.sum(p_ctx, axis=-1, keepdims=True))
    p_loc = p_loc.astype(BF16)
    p_ctx = p_ctx.astype(BF16)
    for n, (t, j) in enumerate(jobs):
        rows = slice(n * 2 * GRID_W, (n + 1) * 2 * GRID_W)
        pv = (_dot(p_loc[rows], v_ref[pl.ds(k0s[t], n_loc), sls[j]])
              + _dot(p_ctx[rows], v_ref[SEQ:, sls[j]])) * inv_l[rows]
        o_ref[t * GRID_W:(t + 1) * GRID_W, sls[j]] = jnp.where(lane < NA_DH, pv[:GRID_W],
                                                               pv[GRID_W:]).astype(BF16)


def _na_bias(rpb):
    mid = WIN_ROWS // 2
    pat_rows = list(range(mid)) + [mid] + list(range(GRID_H - mid + 1, GRID_H))
    r = np.array(pat_rows)
    rs = np.clip(r - mid, 0, GRID_H - WIN_ROWS)
    row_off = rs[:, None] + np.arange(WIN_ROWS)[None, :] - r[:, None] + WIN_ROWS - 1
    c = np.arange(GRID_W)
    q_start = np.clip(c - WIN_COLS // 2, 0, GRID_W - WIN_COLS)[:, None]
    kc = np.arange(GRID_W)[None, :]
    valid = (kc >= q_start) & (kc < q_start + WIN_COLS)
    col_off = np.clip(kc - c[:, None] + WIN_COLS - 1, 0, 2 * WIN_COLS - 2)
    sel_r = np.eye(2 * WIN_ROWS - 1, dtype=np.float32)[row_off]
    sel_c = np.eye(2 * WIN_COLS - 1, dtype=np.float32)[col_off]
    b = jnp.einsum('pia,hab,ckb->hpick', sel_r, rpb.astype(F32), sel_c, precision=lax.Precision.HIGHEST)
    b = jnp.where(valid[None, None, None], b, MASK_VALUE)
    b = b.transpose(1, 0, 3, 2, 4).reshape(len(pat_rows), NA_HEADS * GRID_W, WIN_ROWS * GRID_W)
    return b


def _na_pattern(r):
    mid = WIN_ROWS // 2
    return jnp.where(r < mid, r, jnp.where(r <= GRID_H - mid, mid, r - (GRID_H - 2 * mid)))


def _na(q, k, v, bias, nb):
    q_rows = NA_STEP_ROWS * GRID_W
    steps = GRID_H // NA_STEP_ROWS
    blocks_per_smp = SMP_ROWS // q_rows
    bias_specs = [pl.BlockSpec((1,) + bias.shape[1:],
                               lambda b, s, t=t: (_na_pattern(s * NA_STEP_ROWS + t), 0, 0))
                  for t in range(NA_STEP_ROWS)]
    return pl.pallas_call(
        _na_kernel,
        grid=(nb, steps),
        in_specs=[pl.BlockSpec((q_rows, D), lambda b, s: (b * blocks_per_smp + s, 0)),
                  pl.BlockSpec((SMP_ROWS, D), lambda b, s: (b, 0)),
                  pl.BlockSpec((SMP_ROWS, D), lambda b, s: (b, 0))] + bias_specs,
        out_specs=pl.BlockSpec((q_rows, D), lambda b, s: (b * steps + s, 0)),
        out_shape=jax.ShapeDtypeStruct((nb * SEQ, D), BF16),
        compiler_params=_params(("arbitrary", "arbitrary")),
        name="na_attn",
    )(q, k, v, *([bias] * NA_STEP_ROWS))


def kernel(x, c, ctx, c_ctx, ada_w, ada_b, mix_pre_g, mix_post_g, ffn_pre_g, ffn_post_g, even_w_in,
           diff_lambda, diff_subln_g, mla_q_norm_g, mla_w_qb, mla_kv_norm_g, mla_w_kvb, even_w_out,
           na_w_qkv, na_rpb, na_w_out, router_w, router_b, moe_w1, moe_b1, moe_w2, moe_b2):
    nb = x.shape[0]
    assert x.shape[1:] == (SEQ, D) and ctx.shape[1:] == (CTX, D)
    x2 = x.reshape(nb * SEQ, D)
    c2 = ctx.reshape(nb * CTX, D)
    row = lambda a: a.reshape(1, -1)

    mods = _ada(jnp.concatenate([c, c_ctx[None, :]], axis=0), ada_w, ada_b)
    mods0 = mods[0].reshape(nb + 1, 1, 6 * D)
    mods1 = mods[1].reshape(nb + 1, 1, 6 * D)

    lam_init = 0.8 - 0.6 * math.exp(-0.3 * 0)
    w_in = even_w_in[0]
    wa = jnp.pad(w_in, ((0, 0), (0, 2048 - w_in.shape[1]))).astype(BF16)
    wqb = mla_w_qb[0].reshape(MLA_Q_LORA, MLA_HEADS, MLA_NOPE + MLA_ROPE)
    wqb = jnp.pad(wqb, ((0, 0), (0, 0), (0, MLA_QK_PAD - MLA_NOPE - MLA_ROPE)))
    wqb = wqb.reshape(MLA_Q_LORA, MLA_HEADS * MLA_QK_PAD).astype(BF16)
    qd, kd, vd, qm, km, vm = _proj0(x2, c2, mods0, row(mix_pre_g[0]), wa, row(mla_q_norm_g[0]), wqb,
                                    row(mla_kv_norm_g[0]), mla_w_kvb[0].astype(BF16), nb)
    o0 = _attn0(qd, kd, vd, qm, km, vm, diff_lambda[0], row(diff_subln_g[0]), lam_init, nb)
    xn, hf, mi, mw, seg = _oproj(o0, (x2, c2), mods0, even_w_out[0].astype(BF16), row(mix_post_g[0]),
                                 row(ffn_pre_g[0]), router_w[0], row(router_b[0]), nb, True)
    x1 = _moe(hf, mi, mw, seg, xn, mods0, row(ffn_post_g[0]), 0, moe_w1, moe_b1, moe_w2, moe_b2, nb, True)

    q, k, v = _proj1(x1, mods1, row(mix_pre_g[1]), na_w_qkv[0].astype(BF16), nb)
    o1 = _na(q, k, v, _na_bias(na_rpb[0]), nb)
    xn, hf, mi, mw, seg = _oproj(o1, (x1,), mods1, na_w_out[0].astype(BF16), row(mix_post_g[1]),
                                 row(ffn_pre_g[1]), router_w[1], row(router_b[1]), nb, False)
    out = _moe(hf, mi, mw, seg, xn, mods1, row(ffn_post_g[1]), 1, moe_w1, moe_b1, moe_w2, moe_b2, nb, False)
    return out.reshape(nb, SEQ, D)
```

```python
import functools
import math

import numpy as np
import jax
import jax.numpy as jnp
from jax import lax
from jax.experimental import pallas as pl
from jax.experimental.pallas import tpu as pltpu

F32 = jnp.float32
BF16 = jnp.bfloat16

D = 1024
SEQ = 2048
CTX = 256
GRID_W = 64
GRID_H = SEQ // GRID_W
TILE = 256
LAT_TILES = SEQ // TILE
SMP_TILES = LAT_TILES + 1
SMP_ROWS = SEQ + CTX
EPS = 1e-6
ROPE_BASE = 10000.0

DIFF_HEADS = 4
DIFF_DH = 64
DIFF_SCALE = DIFF_DH ** -0.5
MLA_HEADS = 4
MLA_Q_LORA = 256
MLA_KV_LORA = 128
MLA_NOPE = 128
MLA_ROPE = 64
MLA_V = 128
MLA_SCALE = (MLA_NOPE + MLA_ROPE) ** -0.5
MLA_QK_PAD = 256
NA_HEADS = 16
NA_DH = 64
NA_SCALE = NA_DH ** -0.5
WIN_ROWS = 8
WIN_COLS = 16
N_EXPERTS = 32
TOP_K = 4
D_FF = 1024
SWIGLU_ALPHA = 1.702
SWIGLU_LIMIT = 7.0
EXPERT_BLOCK = 512
SEG_ALIGN = 8
SORT_ROWS = -(-(TILE * TOP_K + N_EXPERTS * (SEG_ALIGN - 1)) // 256) * 256
SORT_ROWS_POW2 = 1 << (SORT_ROWS.bit_length() - 1)
MASK_VALUE = -1e30
LOG2_E = math.log2(math.e)
META_LANES = 128
VMEM_LIMIT = 60 * 1024 * 1024

_NT = (((1,), (1,)), ((), ()))


def _dot(a, b):
    return jnp.dot(a, b, preferred_element_type=F32)


def _dot_nt(a, b):
    return lax.dot_general(a, b, _NT, preferred_element_type=F32)


def _dot_f32(a, b):
    return jnp.dot(a, b, preferred_element_type=F32, precision=lax.Precision.HIGHEST)


def _rms(x, g):
    return x * lax.rsqrt(jnp.mean(x * x, axis=-1, keepdims=True) + EPS) * g


def _mod(mod_ref, k):
    return mod_ref[0, :, k * D:(k + 1) * D]


def _params(sem):
    return pltpu.CompilerParams(dimension_semantics=sem, vmem_limit_bytes=VMEM_LIMIT)


def _ada_kernel(c_ref, w_ref, b_ref, o_ref):
    c = c_ref[...]
    s = c * jax.nn.sigmoid(c)
    o_ref[0] = _dot_f32(s, w_ref[0]) + b_ref[0]


def _ada(cc, ada_w, ada_b):
    depth = ada_w.shape[0]
    n = cc.shape[0]
    nt = 6 * D // D
    return pl.pallas_call(
        _ada_kernel,
        grid=(depth, nt),
        in_specs=[pl.BlockSpec((n, D), lambda l, j: (0, 0)),
                  pl.BlockSpec((1, D, D), lambda l, j: (l, 0, j)),
                  pl.BlockSpec((1, 1, D), lambda l, j: (l, 0, j))],
        out_specs=pl.BlockSpec((1, n, D), lambda l, j: (l, 0, j)),
        out_shape=jax.ShapeDtypeStruct((depth, n, 6 * D), F32),
        compiler_params=_params(("arbitrary", "arbitrary")),
        name="ada",
    )(cc, ada_w, ada_b.reshape(depth, 1, 6 * D))


def _x_lat_map(i):
    return ((i // SMP_TILES) * LAT_TILES + jnp.minimum(i % SMP_TILES, LAT_TILES - 1), 0)


def _x_ctx_map(i):
    return (i // SMP_TILES, 0)


def _mod_map_all(nb):
    return lambda i: (jnp.where(i % SMP_TILES == LAT_TILES, nb, i // SMP_TILES), 0, 0)


def _lat_of_all(i):
    return (i // LAT_TILES) * SMP_TILES + i % LAT_TILES


def _rope(x, cos, sa, sb):
    return x * cos + pltpu.roll(x, 112, 1) * sa + pltpu.roll(x, 16, 1) * sb


def _proj0_kernel(x_ref, c_ref, mod_ref, g_ref, wa_ref, qng_ref, wqb_ref, kvng_ref, wkvb_ref,
                  cos_ref, sa_ref, sb_ref, qd_ref, kd_ref, vd_ref, qm_ref, km_ref, vm_ref):
    is_ctx = pl.program_id(0) % SMP_TILES == LAT_TILES
    nd = DIFF_HEADS * 2 * DIFF_DH
    o = 3 * nd
    halves = (slice(0, TILE // 2), slice(TILE // 2, TILE))
    bigs = []
    for rs in halves:
        x = jnp.where(is_ctx, c_ref[rs, :], x_ref[rs, :])
        h = (_rms(x, g_ref[...]) * (1.0 + _mod(mod_ref, 1)) + _mod(mod_ref, 0)).astype(BF16)
        bigs.append(_dot(h, wa_ref[...]))
    for rs, big in zip(halves, bigs):
        cos, sa, sb = cos_ref[rs, :], sa_ref[rs, :], sb_ref[rs, :]
        for j in range(nd // 128):
            sl = slice(j * 128, (j + 1) * 128)
            qd_ref[rs, sl] = _rope(big[:, j * 128:(j + 1) * 128], cos, sa, sb).astype(BF16)
            kd_ref[rs, sl] = _rope(big[:, nd + j * 128:nd + (j + 1) * 128], cos, sa, sb).astype(BF16)
        vd_ref[rs, :] = big[:, 2 * nd:3 * nd].astype(BF16)
        cq = big[:, o:o + MLA_Q_LORA]
        ckv = big[:, o + MLA_Q_LORA:o + MLA_Q_LORA + MLA_KV_LORA]
        kpe = _rope(big[:, o + MLA_Q_LORA + MLA_KV_LORA:], cos, sa, sb).astype(BF16)
        qm = _dot(_rms(cq, qng_ref[...]).astype(BF16), wqb_ref[...])
        kv = _dot(_rms(ckv, kvng_ref[...]).astype(BF16), wkvb_ref[...])
        for hd in range(MLA_HEADS):
            b0 = hd * MLA_QK_PAD
            qm_ref[rs, b0:b0 + 128] = qm[:, b0:b0 + 128].astype(BF16)
            qm_ref[rs, b0 + 128:b0 + 256] = _rope(qm[:, b0 + 128:b0 + 256], cos, sa, sb).astype(BF16)
            km_ref[rs, b0:b0 + 128] = kv[:, hd * 256:hd * 256 + 128].astype(BF16)
            km_ref[rs, b0 + 128:b0 + 256] = kpe
            vm_ref[rs, hd * 128:(hd + 1) * 128] = kv[:, hd * 256 + 128:(hd + 1) * 256].astype(BF16)


def _rope_tables():
    t = np.arange(SEQ)
    half = 16
    inv = (ROPE_BASE ** (-np.arange(half, dtype=np.float32) * 2.0 / (2 * half))).astype(np.float32)
    ang_r = (t // GRID_W).astype(np.float32)[:, None] * inv[None, :]
    ang_c = (t % GRID_W).astype(np.float32)[:, None] * inv[None, :]
    cr, sr, cc, sc = np.cos(ang_r), np.sin(ang_r), np.cos(ang_c), np.sin(ang_c)
    z = np.zeros_like(cr)
    cos = np.concatenate([cr, cr, cc, cc], axis=1)
    sa = np.concatenate([-sr, z, -sc, z], axis=1)
    sb = np.concatenate([z, sr, z, sc], axis=1)

    def full(tab, fill):
        tab = np.tile(tab, (1, 2))
        ident = np.full((CTX, 128), fill, np.float32)
        return jnp.asarray(np.concatenate([tab, ident], axis=0).astype(np.float32))

    return full(cos, 1.0), full(sa, 0.0), full(sb, 0.0)


def _proj0(x2, c2, mods, g, wa, qng, wqb, kvng, wkvb, nb):
    n_tiles = nb * SMP_TILES
    rows = nb * SMP_ROWS
    cos, sa, sb = _rope_tables()
    tile_map = lambda i: (i, 0)
    const = lambda i: (0, 0)
    tab_spec = pl.BlockSpec((TILE, 128), lambda i: (i % SMP_TILES, 0))
    widths = (512, 512, 512, 1024, 1024, 512)
    return pl.pallas_call(
        _proj0_kernel,
        grid=(n_tiles,),
        in_specs=[pl.BlockSpec((TILE, D), _x_lat_map),
                  pl.BlockSpec((TILE, D), _x_ctx_map),
                  pl.BlockSpec((1, 1, 6 * D), _mod_map_all(nb)),
                  pl.BlockSpec((1, D), const),
                  pl.BlockSpec(wa.shape, const),
                  pl.BlockSpec((1, MLA_Q_LORA), const),
                  pl.BlockSpec(wqb.shape, const),
                  pl.BlockSpec((1, MLA_KV_LORA), const),
                  pl.BlockSpec(wkvb.shape, const),
                  tab_spec, tab_spec, tab_spec],
        out_specs=[pl.BlockSpec((TILE, w), tile_map) for w in widths],
        out_shape=[jax.ShapeDtypeStruct((rows, w), BF16) for w in widths],
        compiler_params=_params(("arbitrary",)),
        name="proj_even",
    )(x2, c2, mods, g, wa, qng, wqb, kvng, wkvb, cos, sa, sb)


def _softmax_parts(s, scale):
    c = scale * LOG2_E
    m = jnp.max(s, axis=-1, keepdims=True)
    p = jnp.exp2(s * c - m * c)
    return p, jnp.sum(p, axis=-1, keepdims=True)


def _attn0_kernel(lam_init, qd_ref, kd_ref, vd_ref, qm_ref, km_ref, vm_ref, lam_ref, sg_ref, o_ref):
    lv = lam_ref[...]
    lam = (jnp.exp(jnp.sum(lv[0:1] * lv[1:2], axis=1, keepdims=True))
           - jnp.exp(jnp.sum(lv[2:3] * lv[3:4], axis=1, keepdims=True)) + lam_init)
    lane = lax.broadcasted_iota(jnp.int32, (TILE, 128), 1)

    def heads(k_lo, nk):
        def diff_scores(hd):
            sl = slice(hd * 128, (hd + 1) * 128)
            q = qd_ref[:, sl]
            zero = jnp.zeros_like(q)
            qq = jnp.concatenate([jnp.where(lane < DIFF_DH, q, zero),
                                  jnp.where(lane >= DIFF_DH, q, zero)], axis=0)
            return _dot_nt(qq, kd_ref[k_lo:k_lo + nk, sl])

        def diff_finish(hd, s):
            sl = slice(hd * 128, (hd + 1) * 128)
            p, l = _softmax_parts(s, DIFF_SCALE)
            pv = _dot(p.astype(BF16), vd_ref[k_lo:k_lo + nk, sl])
            o = pv[:TILE] / l[:TILE] - lam * (pv[TILE:] / l[TILE:])
            o_ref[:, sl] = (_rms(o, sg_ref[...]) * (1.0 - lam_init)).astype(BF16)

        def mla_scores(hd):
            ql = slice(hd * MLA_QK_PAD, (hd + 1) * MLA_QK_PAD)
            return _dot_nt(qm_ref[:, ql], km_ref[k_lo:k_lo + nk, ql])

        def mla_finish(hd, s):
            p, l = _softmax_parts(s, MLA_SCALE)
            pv = _dot(p.astype(BF16), vm_ref[k_lo:k_lo + nk, hd * 128:(hd + 1) * 128])
            o_ref[:, 512 + hd * 128:512 + (hd + 1) * 128] = (pv / l).astype(BF16)

        jobs = ([(diff_scores, diff_finish, hd) for hd in range(DIFF_HEADS)]
                + [(mla_scores, mla_finish, hd) for hd in range(MLA_HEADS)])
        s_next = jobs[0][0](jobs[0][2])
        for n, (_, finish, hd) in enumerate(jobs):
            s = s_next
            if n + 1 < len(jobs):
                s_next = jobs[n + 1][0](jobs[n + 1][2])
            finish(hd, s)

    is_ctx = pl.program_id(1) == LAT_TILES

    @pl.when(jnp.logical_not(is_ctx))
    def _():
        heads(0, SMP_ROWS)

    @pl.when(is_ctx)
    def _():
        heads(SEQ, CTX)


def _attn0(qd, kd, vd, qm, km, vm, lam_vec, subln_g, lam_init, nb):
    q_map = lambda b, p: (b * SMP_TILES + p, 0)
    k_map = lambda b, p: (b, 0)
    const = lambda b, p: (0, 0)
    return pl.pallas_call(
        functools.partial(_attn0_kernel, lam_init),
        grid=(nb, SMP_TILES),
        in_specs=[pl.BlockSpec((TILE, 512), q_map),
                  pl.BlockSpec((SMP_ROWS, 512), k_map),
                  pl.BlockSpec((SMP_ROWS, 512), k_map),
                  pl.BlockSpec((TILE, 1024), q_map),
                  pl.BlockSpec((SMP_ROWS, 1024), k_map),
                  pl.BlockSpec((SMP_ROWS, 512), k_map),
                  pl.BlockSpec(lam_vec.shape, const),
                  pl.BlockSpec((1, 128), const)],
        out_specs=pl.BlockSpec((TILE, D), q_map),
        out_shape=jax.ShapeDtypeStruct((nb * SMP_ROWS, D), BF16),
        compiler_params=_params(("arbitrary", "arbitrary")),
        name="attn_even",
    )(qd, kd, vd, qm, km, vm, lam_vec, subln_g)


def _oproj_kernel(dual, *refs):
    if dual:
        (o_ref, xl_ref, xc_ref, mod_ref, wo_ref, pg_ref, fg_ref, rw_ref, rb_ref,
         xn_ref, hf_ref, mi_ref, mw_ref, seg_ref) = refs
        is_ctx = pl.program_id(0) % SMP_TILES == LAT_TILES
        x = jnp.where(is_ctx, xc_ref[...], xl_ref[...])
    else:
        (o_ref, x_ref, mod_ref, wo_ref, pg_ref, fg_ref, rw_ref, rb_ref,
         xn_ref, hf_ref, mi_ref, mw_ref, seg_ref) = refs
        x = x_ref[...]

    his, los = [], []
    for rs in (slice(0, TILE // 2), slice(TILE // 2, TILE)):
        y = _dot(o_ref[rs, :], wo_ref[...])
        xn = x[rs] + _mod(mod_ref, 2) * _rms(y, pg_ref[...])
        xn_ref[rs, :] = xn
        hf = _rms(xn, fg_ref[...]) * (1.0 + _mod(mod_ref, 4)) + _mod(mod_ref, 3)
        hi = hf.astype(BF16)
        hf_ref[rs, :] = hi
        his.append(hi)
        los.append((hf - hi.astype(F32)).astype(BF16))

    lt = _dot_nt(rw_ref[...], jnp.concatenate(his + los, axis=0))
    ne = N_EXPERTS
    logits = (lt[:ne, :TILE] + lt[ne:, :TILE]) + (lt[:ne, TILE:] + lt[ne:, TILE:]) + rb_ref[...]

    eid = lax.broadcasted_iota(jnp.int32, (ne, TILE), 0).astype(F32)
    sels, vals = [], []
    for _ in range(TOP_K):
        m = jnp.max(logits, axis=0, keepdims=True)
        idx = jnp.min(jnp.where(logits == m, eid, float(ne)), axis=0, keepdims=True)
        sel = eid == idx
        sels.append(sel)
        vals.append(m)
        logits = jnp.where(sel, -jnp.inf, logits)
    ex = [jnp.exp(v - vals[0]) for v in vals]
    den = ex[0] + ex[1] + ex[2] + ex[3]

    onehot = (sels[0] | sels[1] | sels[2] | sels[3]).astype(F32)
    r_i = lax.broadcasted_iota(jnp.int32, (TILE, TILE), 0)
    c_i = lax.broadcasted_iota(jnp.int32, (TILE, TILE), 1)
    before = _dot(onehot.astype(BF16), (r_i < c_i).astype(BF16))
    cnt = jnp.sum(onehot, axis=1, keepdims=True)
    seg_len = jnp.floor((cnt + (SEG_ALIGN - 1)) * (1.0 / SEG_ALIGN)) * SEG_ALIGN
    e_r = lax.broadcasted_iota(jnp.int32, (ne, ne), 0)
    e_c = lax.broadcasted_iota(jnp.int32, (ne, ne), 1)
    seg_len_b = jnp.broadcast_to(seg_len, (ne, META_LANES))
    seg_units = (seg_len_b * (1.0 / SEG_ALIGN)).astype(BF16)
    seg_off = _dot((e_c < e_r).astype(BF16), seg_units)[:, 0:1] * SEG_ALIGN
    pos = before + seg_off
    sub = lax.broadcasted_iota(jnp.int32, (8, TILE), 0)
    mi = jnp.zeros((8, TILE), jnp.int32)
    mw = jnp.zeros((8, TILE), F32)
    for k in range(TOP_K):
        row = jnp.sum(jnp.where(sels[k], pos, 0.0), axis=0, keepdims=True).astype(jnp.int32)
        mi = jnp.where(sub == k, row, mi)
        mw = jnp.where(sub == k, ex[k] / den, mw)
    mi_ref[0] = mi
    mw_ref[0] = mw
    seg_ref[0] = seg_len_b.astype(jnp.int32)


def _oproj(o, x_args, mods, wo, pg, fg, rw, rb, nb, all_tokens):
    const = lambda i: (0, 0)
    tile_map = lambda i: (i, 0)
    tile3_map = lambda i: (i, 0, 0)
    rw_hi = rw.astype(BF16)
    rw_lo = (rw - rw_hi.astype(F32)).astype(BF16)
    rw_split = jnp.concatenate([rw_hi.T, rw_lo.T], axis=0)
    if all_tokens:
        n_tiles = nb * SMP_TILES
        x_specs = [pl.BlockSpec((TILE, D), _x_lat_map), pl.BlockSpec((TILE, D), _x_ctx_map)]
        mod_map = _mod_map_all(nb)
    else:
        n_tiles = nb * LAT_TILES
        x_specs = [pl.BlockSpec((TILE, D), lambda i: (_lat_of_all(i), 0))]
        mod_map = lambda i: (i // LAT_TILES, 0, 0)
    rows = n_tiles * TILE
    return pl.pallas_call(
        functools.partial(_oproj_kernel, all_tokens),
        grid=(n_tiles,),
        in_specs=[pl.BlockSpec((TILE, D), tile_map)] + x_specs + [
            pl.BlockSpec((1, 1, 6 * D), mod_map),
            pl.BlockSpec((D, D), const),
            pl.BlockSpec((1, D), const),
            pl.BlockSpec((1, D), const),
            pl.BlockSpec((2 * N_EXPERTS, D), const),
            pl.BlockSpec((N_EXPERTS, 1), const)],
        out_specs=[pl.BlockSpec((TILE, D), tile_map),
                   pl.BlockSpec((TILE, D), tile_map),
                   pl.BlockSpec((1, 8, TILE), tile3_map),
                   pl.BlockSpec((1, 8, TILE), tile3_map),
                   pl.BlockSpec((1, N_EXPERTS, META_LANES), tile3_map)],
        out_shape=[jax.ShapeDtypeStruct((rows, D), F32),
                   jax.ShapeDtypeStruct((rows, D), BF16),
                   jax.ShapeDtypeStruct((n_tiles, 8, TILE), jnp.int32),
                   jax.ShapeDtypeStruct((n_tiles, 8, TILE), F32),
                   jax.ShapeDtypeStruct((n_tiles, N_EXPERTS, META_LANES), jnp.int32)],
        compiler_params=_params(("arbitrary",)),
        name="oproj_router",
    )(o, *x_args, mods, wo, pg, fg, rw_split, rb.reshape(N_EXPERTS, 1))


U32 = jnp.uint32
HALF = D // 2


def _pack_rows(x):
    bits = pltpu.bitcast(x.astype(BF16).astype(F32), U32)
    return (bits[:, :HALF] >> 16) | (bits[:, HALF:] & jnp.uint32(0xFFFF0000))


def _unpack_rows(w):
    lo = pltpu.bitcast(w << 16, F32).astype(BF16)
    hi = pltpu.bitcast(w & jnp.uint32(0xFFFF0000), F32).astype(BF16)
    return jnp.concatenate([lo, hi], axis=1)


def _for_pieces(n, largest, fn):
    piece = largest
    while piece >= SEG_ALIGN:
        done = (n // (2 * piece)) * (2 * piece)
        pl.when((n & piece) != 0)(functools.partial(fn, done, piece))
        piece //= 2


PIECES = tuple(TILE >> b for b in range((TILE // SEG_ALIGN).bit_length()))
N_PIECES = len(PIECES)
LIST_ROWS = 2 * N_PIECES + 1


def _piece_lists(seg_len, seg_local, seg_first):
    sizes = jnp.asarray(PIECES, jnp.int32)[None, :, None]
    n = seg_len[:, None, :]
    active = (n & sizes) != 0
    done = n // (2 * sizes) * (2 * sizes)
    slot = jnp.cumsum(active, axis=-1) - 1
    put = active[..., :, None] & (slot[..., :, None] == jnp.arange(N_EXPERTS))
    compact = lambda v: jnp.sum(jnp.where(put, v[..., :, None], 0), axis=-2)
    loc = compact(seg_local[:, None, :] + done)
    glb = compact(seg_first[:, None, :] + done)
    counts = jnp.sum(active, axis=-1)
    tail = jnp.concatenate([counts, jnp.sum(seg_len, axis=-1, keepdims=True)], axis=-1)
    tail = jnp.pad(tail, ((0, 0), (0, N_EXPERTS - tail.shape[-1])))[:, None, :]
    return jnp.concatenate([loc, glb, tail], axis=1).astype(jnp.int32)


STEP_TILES = 2


def _start_segment_copies(list_ref, u, local_ref, hbm_ref, sem, to_hbm):
    for b, rows in enumerate(PIECES):
        def start(j, carry, b=b, rows=rows):
            loc = local_ref.at[pl.ds(pl.multiple_of(list_ref[u, b, j], SEG_ALIGN), rows)]
            glb = hbm_ref.at[pl.ds(pl.multiple_of(list_ref[u, N_PIECES + b, j], SEG_ALIGN), rows)]
            (pltpu.make_async_copy(loc, glb, sem) if to_hbm else pltpu.make_async_copy(glb, loc, sem)).start()
            return carry

        lax.fori_loop(0, list_ref[u, 2 * N_PIECES, b], start, 0)


def _wait_segment_copies(list_ref, u, local_ref, hbm_ref, sem, to_hbm):
    def wait(done, rows):
        del done
        loc = local_ref.at[pl.ds(0, rows)]
        glb = hbm_ref.at[pl.ds(0, rows)]
        (pltpu.make_async_copy(loc, glb, sem) if to_hbm else pltpu.make_async_copy(glb, loc, sem)).wait()

    _for_pieces(list_ref[u, 2 * N_PIECES, N_PIECES], SORT_ROWS_POW2, wait)


def _zero_fill_copies(tail_ref, free_ref, zbuf, xs_ref, sem, action):
    def per_expert(e, carry):
        def piece_copy(done, rows):
            dst = xs_ref.at[pl.ds(pl.multiple_of(tail_ref[0, e] + done, SEG_ALIGN), rows)]
            action(pltpu.make_async_copy(zbuf.at[pl.ds(0, rows)], dst, sem))

        _for_pieces(tail_ref[1, e], EXPERT_BLOCK // 2, piece_copy)
        return carry

    def per_block(j, carry):
        first = pl.multiple_of(free_ref[0] + j * EXPERT_BLOCK, EXPERT_BLOCK)
        action(pltpu.make_async_copy(zbuf, xs_ref.at[pl.ds(first, EXPERT_BLOCK)], sem))
        return carry

    lax.fori_loop(0, N_EXPERTS, per_expert, 0)
    lax.fori_loop(0, free_ref[1], per_block, 0)


def _dispatch_kernel(seg_ref, seg_prev_ref, tail_ref, free_ref, lp_ref, h_ref, xs_ref, sbuf, zbuf, sem):
    i = pl.program_id(0)
    last = pl.num_programs(0) - 1
    slot = i % 2
    row = lax.broadcasted_iota(jnp.int32, (SORT_ROWS, TILE), 0)
    tiles = range(STEP_TILES)
    for u in tiles:
        lp = lp_ref[u]
        hit = row == lp[0:1]
        for k in range(1, TOP_K):
            hit = hit | (row == lp[k:k + 1])
        sbuf[slot, u] = _pack_rows(_dot(jnp.where(hit, 1.0, 0.0).astype(BF16),
                                        h_ref[u * TILE:(u + 1) * TILE, :]))
    for u in tiles:
        _start_segment_copies(seg_ref, u, sbuf.at[slot, u], xs_ref, sem.at[slot], True)

    @pl.when(i > 0)
    def _():
        for u in tiles:
            _wait_segment_copies(seg_prev_ref, u, sbuf.at[1 - slot, u], xs_ref, sem.at[1 - slot], True)

    @pl.when(i == last)
    def _():
        for u in tiles:
            _wait_segment_copies(seg_ref, u, sbuf.at[slot, u], xs_ref, sem.at[slot], True)
        zbuf[...] = jnp.zeros_like(zbuf)
        _zero_fill_copies(tail_ref, free_ref, zbuf, xs_ref, sem.at[slot], lambda cp: cp.start())
        _zero_fill_copies(tail_ref, free_ref, zbuf, xs_ref, sem.at[slot], lambda cp: cp.wait())


def _dispatch(segs, tails, free, lpos_t, hf, m_rows):
    n_steps = hf.shape[0] // (STEP_TILES * TILE)
    seg_block = (STEP_TILES, LIST_ROWS, N_EXPERTS)
    return pl.pallas_call(
        _dispatch_kernel,
        grid=(n_steps,),
        in_specs=[pl.BlockSpec(seg_block, lambda i: (i, 0, 0), memory_space=pltpu.SMEM),
                  pl.BlockSpec(seg_block, lambda i: (jnp.maximum(i - 1, 0), 0, 0), memory_space=pltpu.SMEM),
                  pl.BlockSpec(memory_space=pltpu.SMEM),
                  pl.BlockSpec(memory_space=pltpu.SMEM),
                  pl.BlockSpec((STEP_TILES, 8, TILE), lambda i: (i, 0, 0)),
                  pl.BlockSpec((STEP_TILES * TILE, D), lambda i: (i, 0))],
        out_specs=pl.BlockSpec(memory_space=pl.ANY),
        out_shape=jax.ShapeDtypeStruct((m_rows, HALF), U32),
        scratch_shapes=[pltpu.VMEM((2, STEP_TILES, SORT_ROWS, HALF), U32),
                        pltpu.VMEM((EXPERT_BLOCK, HALF), U32), pltpu.SemaphoreType.DMA((2,))],
        compiler_params=_params(("arbitrary",)),
        name="moe_dispatch",
    )(segs, segs, tails, free, lpos_t, hf)


def _expert_kernel(be_ref, nu_ref, xs_ref, w1_ref, b1_ref, w2_ref, b2_ref, ys_ref, w1b, w2b):
    i = pl.program_id(0)
    used = i < nu_ref[0]
    new_expert = jnp.logical_or(i == 0, be_ref[i] != be_ref[jnp.maximum(i - 1, 0)])

    @pl.when(jnp.logical_and(used, new_expert))
    def _():
        w1b[...] = w1_ref[0, 0].astype(BF16)
        w2b[...] = w2_ref[0, 0].astype(BF16)

    @pl.when(used)
    def _():
        gu = _dot(_unpack_rows(xs_ref[...]), w1b[...]) + b1_ref[0, 0]
        gate = jnp.minimum(gu[:, :D_FF], SWIGLU_LIMIT)
        lin = jnp.clip(gu[:, D_FF:], -SWIGLU_LIMIT, SWIGLU_LIMIT)
        act = gate * jax.nn.sigmoid(SWIGLU_ALPHA * gate) * (lin + 1.0)
        ys_ref[...] = _pack_rows(_dot(act.astype(BF16), w2b[...]) + b2_ref[0, 0])

    @pl.when(i >= nu_ref[0])
    def _():
        ys_ref[...] = jnp.zeros_like(ys_ref)


def _experts(blk_e, n_used, xs, layer, w1, b1, w2, b2):
    n_blocks = xs.shape[0] // EXPERT_BLOCK
    in_row_map = lambda i, be, nu: (jnp.minimum(i, nu[0] - 1), 0)
    e_map = lambda i, be, nu: (layer, be[i], 0, 0)
    grid_spec = pltpu.PrefetchScalarGridSpec(
        num_scalar_prefetch=2,
        grid=(n_blocks,),
        in_specs=[pl.BlockSpec((EXPERT_BLOCK, HALF), in_row_map),
                  pl.BlockSpec((1, 1, D, 2 * D_FF), e_map),
                  pl.BlockSpec((1, 1, 1, 2 * D_FF), e_map),
                  pl.BlockSpec((1, 1, D_FF, D), e_map),
                  pl.BlockSpec((1, 1, 1, D), e_map)],
        out_specs=pl.BlockSpec((EXPERT_BLOCK, HALF), lambda i, be, nu: (i, 0)),
        scratch_shapes=[pltpu.VMEM((D, 2 * D_FF), BF16), pltpu.VMEM((D_FF, D), BF16)])
    return pl.pallas_call(
        _expert_kernel,
        grid_spec=grid_spec,
        out_shape=jax.ShapeDtypeStruct(xs.shape, U32),
        compiler_params=_params(("arbitrary",)),
        name="moe_experts",
    )(blk_e, n_used, xs, w1, b1[:, :, None, :], w2, b2[:, :, None, :])


def _combine_kernel(seg_ref, seg_next_ref, mi_ref, mw_ref, x_ref, g_ref, ys_ref, *rest):
    mod_refs, (o_ref, ybuf, sem) = rest[:STEP_TILES], rest[STEP_TILES:]
    i = pl.program_id(0)
    slot = i % 2
    tiles = range(STEP_TILES)

    @pl.when(i == 0)
    def _():
        ybuf[...] = jnp.zeros_like(ybuf)
        for u in tiles:
            _start_segment_copies(seg_ref, u, ybuf.at[0, u], ys_ref, sem.at[0], False)

    @pl.when(i + 1 < pl.num_programs(0))
    def _():
        for u in tiles:
            _start_segment_copies(seg_next_ref, u, ybuf.at[1 - slot, u], ys_ref, sem.at[1 - slot], False)

    col = lax.broadcasted_iota(jnp.int32, (TILE, SORT_ROWS), 1)
    wms = []
    for u in tiles:
        rs = slice(u * TILE, (u + 1) * TILE)
        mi = mi_ref[rs, :]
        mw = mw_ref[rs, :]
        wm = jnp.zeros((TILE, SORT_ROWS), F32)
        for k in range(TOP_K):
            wm = jnp.where(col == mi[:, k:k + 1], mw[:, k:k + 1], wm)
        wms.append(wm.astype(BF16))
    for u in tiles:
        _wait_segment_copies(seg_ref, u, ybuf.at[slot, u], ys_ref, sem.at[slot], False)
    for u in tiles:
        rs = slice(u * TILE, (u + 1) * TILE)
        y = _dot(wms[u], _unpack_rows(ybuf[slot, u]))
        o_ref[rs, :] = x_ref[rs, :] + _mod(mod_refs[u], 5) * _rms(y, g_ref[...])


def _combine(segs, mi, mw, xn, mods, g, ys, nb, all_tokens):
    n_steps = xn.shape[0] // (STEP_TILES * TILE)
    step_map = lambda i: (i, 0)
    mod_map = _mod_map_all(nb) if all_tokens else (lambda i: (i // LAT_TILES, 0, 0))
    seg_block = (STEP_TILES, LIST_ROWS, N_EXPERTS)
    rows = STEP_TILES * TILE
    mod_specs = [pl.BlockSpec((1, 1, 6 * D), lambda i, u=u: mod_map(i * STEP_TILES + u))
                 for u in range(STEP_TILES)]
    return pl.pallas_call(
        _combine_kernel,
        grid=(n_steps,),
        in_specs=[pl.BlockSpec(seg_block, lambda i: (i, 0, 0), memory_space=pltpu.SMEM),
                  pl.BlockSpec(seg_block, lambda i: (jnp.minimum(i + 1, n_steps - 1), 0, 0),
                               memory_space=pltpu.SMEM),
                  pl.BlockSpec((rows, 8), step_map),
                  pl.BlockSpec((rows, 8), step_map),
                  pl.BlockSpec((rows, D), step_map),
                  pl.BlockSpec((1, D), lambda i: (0, 0)),
                  pl.BlockSpec(memory_space=pl.ANY)] + mod_specs,
        out_specs=pl.BlockSpec((rows, D), step_map),
        out_shape=jax.ShapeDtypeStruct(xn.shape, F32),
        scratch_shapes=[pltpu.VMEM((2, STEP_TILES, SORT_ROWS, HALF), U32), pltpu.SemaphoreType.DMA((2,))],
        compiler_params=_params(("arbitrary",)),
        name="moe_combine",
    )(segs, segs, mi, mw, xn, g, ys, *([mods] * STEP_TILES))


def _moe(hf, mi, mw, seg, xn, mods, g, layer, w1, b1, w2, b2, nb, all_tokens):
    t = hf.shape[0]
    n_tiles = t // TILE
    assert n_tiles % STEP_TILES == 0, "dispatch / combine take STEP_TILES token tiles per grid step"
    rows_max = t * TOP_K + n_tiles * N_EXPERTS * (SEG_ALIGN - 1)
    n_blocks = -(-rows_max // EXPERT_BLOCK) + N_EXPERTS
    seg_len = seg[:, :, 0]
    counts = jnp.sum(seg_len, axis=0)
    padded = (counts + EXPERT_BLOCK - 1) // EXPERT_BLOCK * EXPERT_BLOCK
    pad_end = jnp.cumsum(padded)
    pad_start = pad_end - padded
    seg_first = pad_start[None, :] + jnp.cumsum(seg_len, axis=0) - seg_len
    seg_local = jnp.cumsum(seg_len, axis=1) - seg_len
    segs = _piece_lists(seg_len, seg_local, seg_first)
    by_token = lambda a: a.transpose(0, 2, 1).reshape(t, 8)
    blk_row = jnp.arange(n_blocks, dtype=jnp.int32) * EXPERT_BLOCK
    blk_e = jnp.minimum(jnp.sum(pad_end[None, :] <= blk_row[:, None], axis=1), N_EXPERTS - 1).astype(jnp.int32)
    n_used = (pad_end[-1:] // EXPERT_BLOCK).astype(jnp.int32)
    tails = jnp.stack([pad_start + counts, padded - counts]).astype(jnp.int32)
    free = jnp.concatenate([pad_end[-1:], n_blocks - n_used]).astype(jnp.int32)
    xs = _dispatch(segs, tails, free, mi, hf, n_blocks * EXPERT_BLOCK)
    ys = _experts(blk_e, n_used, xs, layer, w1, b1, w2, b2)
    return _combine(segs, by_token(mi), by_token(mw), xn, mods, g, ys, nb, all_tokens)


def _proj1_kernel(x_ref, mod_ref, g_ref, w_ref, q_ref, k_ref, v_ref):
    h = (_rms(x_ref[...], g_ref[...]) * (1.0 + _mod(mod_ref, 1)) + _mod(mod_ref, 0)).astype(BF16)
    qkv = _dot(h, w_ref[...])
    q_ref[...] = (qkv[:, :D] * NA_SCALE).astype(BF16)
    k_ref[...] = qkv[:, D:2 * D].astype(BF16)
    v_ref[...] = qkv[:, 2 * D:].astype(BF16)


def _proj1(x1, mods, g, w, nb):
    n_tiles = nb * SMP_TILES
    tile_map = lambda i: (i, 0)
    return pl.pallas_call(
        _proj1_kernel,
        grid=(n_tiles,),
        in_specs=[pl.BlockSpec((TILE, D), tile_map),
                  pl.BlockSpec((1, 1, 6 * D), _mod_map_all(nb)),
                  pl.BlockSpec((1, D), lambda i: (0, 0)),
                  pl.BlockSpec((D, 3 * D), lambda i: (0, 0))],
        out_specs=[pl.BlockSpec((TILE, D), tile_map)] * 3,
        out_shape=[jax.ShapeDtypeStruct((n_tiles * TILE, D), BF16)] * 3,
        compiler_params=_params(("arbitrary",)),
        name="proj_odd",
    )(x1, mods, g, w)


def _na_row_start(r):
    return jnp.clip(r - WIN_ROWS // 2, 0, GRID_H - WIN_ROWS)


NA_STEP_ROWS = 4


def _na_kernel(q_ref, k_ref, v_ref, *rest):
    bias_refs, o_ref = rest[:NA_STEP_ROWS], rest[NA_STEP_ROWS]
    n_loc = WIN_ROWS * GRID_W
    lane = lax.broadcasted_iota(jnp.int32, (GRID_W, 128), 1)
    pairs = range(NA_HEADS // 2)
    sls = [slice(j * 128, (j + 1) * 128) for j in pairs]
    k0s, qqs = [], []
    for t in range(NA_STEP_ROWS):
        r = pl.program_id(1) * NA_STEP_ROWS + t
        k0s.append(pl.multiple_of(_na_row_start(r) * GRID_W, GRID_W))
        for j in pairs:
            q = q_ref[t * GRID_W:(t + 1) * GRID_W, sls[j]]
            zero = jnp.zeros_like(q)
            qqs.append(jnp.concatenate([jnp.where(lane < NA_DH, q, zero),
                                        jnp.where(lane >= NA_DH, q, zero)], axis=0))
    jobs = [(t, j) for t in range(NA_STEP_ROWS) for j in pairs]
    s_loc = jnp.concatenate([_dot_nt(qqs[n], k_ref[pl.ds(k0s[t], n_loc), sls[j]])
                             for n, (t, j) in enumerate(jobs)], axis=0)
    s_loc = s_loc + jnp.concatenate([b[0] for b in bias_refs], axis=0)
    s_ctx = jnp.concatenate([_dot_nt(qqs[n], k_ref[SEQ:, sls[j]]) for n, (t, j) in enumerate(jobs)], axis=0)
    m = jnp.maximum(jnp.max(s_loc, axis=-1, keepdims=True), jnp.max(s_ctx, axis=-1, keepdims=True))
    p_loc = jnp.exp(s_loc - m)
    p_ctx = jnp.exp(s_ctx - m)
    inv_l = 1.0 / (jnp.sum(p_loc, axis=-1, keepdims=True) + jnp.sum(p_ctx, axis=-1, keepdims=True))
    p_loc = p_loc.astype(BF16)
    p_ctx = p_ctx.astype(BF16)
    for n, (t, j) in enumerate(jobs):
        rows = slice(n * 2 * GRID_W, (n + 1) * 2 * GRID_W)
        pv = (_dot(p_loc[rows], v_ref[pl.ds(k0s[t], n_loc), sls[j]])
              + _dot(p_ctx[rows], v_ref[SEQ:, sls[j]])) * inv_l[rows]
        o_ref[t * GRID_W:(t + 1) * GRID_W, sls[j]] = jnp.where(lane < NA_DH, pv[:GRID_W],
                                                               pv[GRID_W:]).astype(BF16)


def _na_bias(rpb):
    mid = WIN_ROWS // 2
    pat_rows = list(range(mid)) + [mid] + list(range(GRID_H - mid + 1, GRID_H))
    r = np.array(pat_rows)
    rs = np.clip(r - mid, 0, GRID_H - WIN_ROWS)
    row_off = rs[:, None] + np.arange(WIN_ROWS)[None, :] - r[:, None] + WIN_ROWS - 1
    c = np.arange(GRID_W)
    q_start = np.clip(c - WIN_COLS // 2, 0, GRID_W - WIN_COLS)[:, None]
    kc = np.arange(GRID_W)[None, :]
    valid = (kc >= q_start) & (kc < q_start + WIN_COLS)
    col_off = np.clip(kc - c[:, None] + WIN_COLS - 1, 0, 2 * WIN_COLS - 2)
    sel_r = np.eye(2 * WIN_ROWS - 1, dtype=np.float32)[row_off]
    sel_c = np.eye(2 * WIN_COLS - 1, dtype=np.float32)[col_off]
    b = jnp.einsum('pia,hab,ckb->phcik', sel_r, rpb.astype(F32), sel_c, precision=lax.Precision.HIGHEST)
    b = jnp.where(valid[None, None, :, None, :], b, MASK_VALUE)
    return b.reshape(len(pat_rows), NA_HEADS * GRID_W, WIN_ROWS * GRID_W)


def _na_pattern(r):
    mid = WIN_ROWS // 2
    return jnp.where(r < mid, r, jnp.where(r <= GRID_H - mid, mid, r - (GRID_H - 2 * mid)))


def _na(q, k, v, bias, nb):
    q_rows = NA_STEP_ROWS * GRID_W
    steps = GRID_H // NA_STEP_ROWS
    blocks_per_smp = SMP_ROWS // q_rows
    bias_specs = [pl.BlockSpec((1,) + bias.shape[1:],
                               lambda b, s, t=t: (_na_pattern(s * NA_STEP_ROWS + t), 0, 0))
                  for t in range(NA_STEP_ROWS)]
    return pl.pallas_call(
        _na_kernel,
        grid=(nb, steps),
        in_specs=[pl.BlockSpec((q_rows, D), lambda b, s: (b * blocks_per_smp + s, 0)),
                  pl.BlockSpec((SMP_ROWS, D), lambda b, s: (b, 0)),
                  pl.BlockSpec((SMP_ROWS, D), lambda b, s: (b, 0))] + bias_specs,
        out_specs=pl.BlockSpec((q_rows, D), lambda b, s: (b * steps + s, 0)),
        out_shape=jax.ShapeDtypeStruct((nb * SEQ, D), BF16),
        compiler_params=_params(("arbitrary", "arbitrary")),
        name="na_attn",
    )(q, k, v, *([bias] * NA_STEP_ROWS))


def kernel(x, c, ctx, c_ctx, ada_w, ada_b, mix_pre_g, mix_post_g, ffn_pre_g, ffn_post_g, even_w_in,
           diff_lambda, diff_subln_g, mla_q_norm_g, mla_w_qb, mla_kv_norm_g, mla_w_kvb, even_w_out,
           na_w_qkv, na_rpb, na_w_out, router_w, router_b, moe_w1, moe_b1, moe_w2, moe_b2):
    nb = x.shape[0]
    assert x.shape[1:] == (SEQ, D) and ctx.shape[1:] == (CTX, D)
    x2 = x.reshape(nb * SEQ, D)
    c2 = ctx.reshape(nb * CTX, D)
    row = lambda a: a.reshape(1, -1)

    mods = _ada(jnp.concatenate([c, c_ctx[None, :]], axis=0), ada_w, ada_b)
    mods0 = mods[0].reshape(nb + 1, 1, 6 * D)
    mods1 = mods[1].reshape(nb + 1, 1, 6 * D)

    lam_init = 0.8 - 0.6 * math.exp(-0.3 * 0)
    w_in = even_w_in[0]
    wa = jnp.pad(w_in, ((0, 0), (0, 2048 - w_in.shape[1]))).astype(BF16)
    wqb = mla_w_qb[0].reshape(MLA_Q_LORA, MLA_HEADS, MLA_NOPE + MLA_ROPE)
    wqb = jnp.pad(wqb, ((0, 0), (0, 0), (0, MLA_QK_PAD - MLA_NOPE - MLA_ROPE)))
    wqb = wqb.reshape(MLA_Q_LORA, MLA_HEADS * MLA_QK_PAD).astype(BF16)
    qd, kd, vd, qm, km, vm = _proj0(x2, c2, mods0, row(mix_pre_g[0]), wa, row(mla_q_norm_g[0]), wqb,
                                    row(mla_kv_norm_g[0]), mla_w_kvb[0].astype(BF16), nb)
    o0 = _attn0(qd, kd, vd, qm, km, vm, diff_lambda[0], row(diff_subln_g[0]), lam_init, nb)
    xn, hf, mi, mw, seg = _oproj(o0, (x2, c2), mods0, even_w_out[0].astype(BF16), row(mix_post_g[0]),
                                 row(ffn_pre_g[0]), router_w[0], row(router_b[0]), nb, True)
    x1 = _moe(hf, mi, mw, seg, xn, mods0, row(ffn_post_g[0]), 0, moe_w1, moe_b1, moe_w2, moe_b2, nb, True)

    q, k, v = _proj1(x1, mods1, row(mix_pre_g[1]), na_w_qkv[0].astype(BF16), nb)
    o1 = _na(q, k, v, _na_bias(na_rpb[0]), nb)
    xn, hf, mi, mw, seg = _oproj(o1, (x1,), mods1, na_w_out[0].astype(BF16), row(mix_post_g[1]),
                                 row(ffn_pre_g[1]), router_w[1], row(router_b[1]), nb, False)
    out = _moe(hf, mi, mw, seg, xn, mods1, row(ffn_post_g[1]), 1, moe_w1, moe_b1, moe_w2, moe_b2, nb, False)
    return out.reshape(nb, SEQ, D)
```

```python
import functools
import math

import numpy as np
import jax
import jax.numpy as jnp
from jax import lax
from jax.experimental import pallas as pl
from jax.experimental.pallas import tpu as pltpu

F32 = jnp.float32
BF16 = jnp.bfloat16

D = 1024
SEQ = 2048
CTX = 256
GRID_W = 64
GRID_H = SEQ // GRID_W
TILE = 256
LAT_TILES = SEQ // TILE
SMP_TILES = LAT_TILES + 1
SMP_ROWS = SEQ + CTX
EPS = 1e-6
ROPE_BASE = 10000.0

DIFF_HEADS = 4
DIFF_DH = 64
DIFF_SCALE = DIFF_DH ** -0.5
MLA_HEADS = 4
MLA_Q_LORA = 256
MLA_KV_LORA = 128
MLA_NOPE = 128
MLA_ROPE = 64
MLA_V = 128
MLA_SCALE = (MLA_NOPE + MLA_ROPE) ** -0.5
MLA_QK_PAD = 256
NA_HEADS = 16
NA_DH = 64
NA_SCALE = NA_DH ** -0.5
WIN_ROWS = 8
WIN_COLS = 16
N_EXPERTS = 32
TOP_K = 4
D_FF = 1024
SWIGLU_ALPHA = 1.702
SWIGLU_LIMIT = 7.0
EXPERT_BLOCK = 1024
SEG_ALIGN = 8
SORT_ROWS = -(-(TILE * TOP_K + N_EXPERTS * (SEG_ALIGN - 1)) // 256) * 256
SORT_ROWS_POW2 = 1 << (SORT_ROWS.bit_length() - 1)
MASK_VALUE = -1e30
LOG2_E = math.log2(math.e)
META_LANES = 128
VMEM_LIMIT = 60 * 1024 * 1024

_NT = (((1,), (1,)), ((), ()))


def _dot(a, b):
    return jnp.dot(a, b, preferred_element_type=F32)


def _dot_nt(a, b):
    return lax.dot_general(a, b, _NT, preferred_element_type=F32)


def _dot_f32(a, b):
    return jnp.dot(a, b, preferred_element_type=F32, precision=lax.Precision.HIGHEST)


def _rms(x, g):
    return x * lax.rsqrt(jnp.mean(x * x, axis=-1, keepdims=True) + EPS) * g


def _mod(mod_ref, k):
    return mod_ref[0, :, k * D:(k + 1) * D]


def _params(sem):
    return pltpu.CompilerParams(dimension_semantics=sem, vmem_limit_bytes=VMEM_LIMIT)


def _ada_kernel(c_ref, w_ref, b_ref, o_ref):
    c = c_ref[...]
    s = c * jax.nn.sigmoid(c)
    o_ref[0] = _dot_f32(s, w_ref[0]) + b_ref[0]


def _ada(cc, ada_w, ada_b):
    depth = ada_w.shape[0]
    n = cc.shape[0]
    nt = 6 * D // D
    return pl.pallas_call(
        _ada_kernel,
        grid=(depth, nt),
        in_specs=[pl.BlockSpec((n, D), lambda l, j: (0, 0)),
                  pl.BlockSpec((1, D, D), lambda l, j: (l, 0, j)),
                  pl.BlockSpec((1, 1, D), lambda l, j: (l, 0, j))],
        out_specs=pl.BlockSpec((1, n, D), lambda l, j: (l, 0, j)),
        out_shape=jax.ShapeDtypeStruct((depth, n, 6 * D), F32),
        compiler_params=_params(("arbitrary", "arbitrary")),
        name="ada",
    )(cc, ada_w, ada_b.reshape(depth, 1, 6 * D))


def _x_lat_map(i):
    return ((i // SMP_TILES) * LAT_TILES + jnp.minimum(i % SMP_TILES, LAT_TILES - 1), 0)


def _x_ctx_map(i):
    return (i // SMP_TILES, 0)


def _mod_map_all(nb):
    return lambda i: (jnp.where(i % SMP_TILES == LAT_TILES, nb, i // SMP_TILES), 0, 0)


def _lat_of_all(i):
    return (i // LAT_TILES) * SMP_TILES + i % LAT_TILES


def _rope(x, cos, sa, sb):
    return x * cos + pltpu.roll(x, 112, 1) * sa + pltpu.roll(x, 16, 1) * sb


def _proj0_kernel(x_ref, c_ref, mod_ref, g_ref, wa_ref, qng_ref, wqb_ref, kvng_ref, wkvb_ref,
                  cos_ref, sa_ref, sb_ref, qd_ref, kd_ref, vd_ref, qm_ref, km_ref, vm_ref):
    is_ctx = pl.program_id(0) % SMP_TILES == LAT_TILES
    nd = DIFF_HEADS * 2 * DIFF_DH
    o = 3 * nd
    halves = (slice(0, TILE // 2), slice(TILE // 2, TILE))
    bigs = []
    for rs in halves:
        x = jnp.where(is_ctx, c_ref[rs, :], x_ref[rs, :])
        h = (_rms(x, g_ref[...]) * (1.0 + _mod(mod_ref, 1)) + _mod(mod_ref, 0)).astype(BF16)
        bigs.append(_dot(h, wa_ref[...]))
    for rs, big in zip(halves, bigs):
        cos, sa, sb = cos_ref[rs, :], sa_ref[rs, :], sb_ref[rs, :]
        for j in range(nd // 128):
            sl = slice(j * 128, (j + 1) * 128)
            qd_ref[rs, sl] = _rope(big[:, j * 128:(j + 1) * 128], cos, sa, sb).astype(BF16)
            kd_ref[rs, sl] = _rope(big[:, nd + j * 128:nd + (j + 1) * 128], cos, sa, sb).astype(BF16)
        vd_ref[rs, :] = big[:, 2 * nd:3 * nd].astype(BF16)
        cq = big[:, o:o + MLA_Q_LORA]
        ckv = big[:, o + MLA_Q_LORA:o + MLA_Q_LORA + MLA_KV_LORA]
        kpe = _rope(big[:, o + MLA_Q_LORA + MLA_KV_LORA:], cos, sa, sb).astype(BF16)
        qm = _dot(_rms(cq, qng_ref[...]).astype(BF16), wqb_ref[...])
        kv = _dot(_rms(ckv, kvng_ref[...]).astype(BF16), wkvb_ref[...])
        for hd in range(MLA_HEADS):
            b0 = hd * MLA_QK_PAD
            qm_ref[rs, b0:b0 + 128] = qm[:, b0:b0 + 128].astype(BF16)
            qm_ref[rs, b0 + 128:b0 + 256] = _rope(qm[:, b0 + 128:b0 + 256], cos, sa, sb).astype(BF16)
            km_ref[rs, b0:b0 + 128] = kv[:, hd * 256:hd * 256 + 128].astype(BF16)
            km_ref[rs, b0 + 128:b0 + 256] = kpe
            vm_ref[rs, hd * 128:(hd + 1) * 128] = kv[:, hd * 256 + 128:(hd + 1) * 256].astype(BF16)


def _rope_tables():
    t = np.arange(SEQ)
    half = 16
    inv = (ROPE_BASE ** (-np.arange(half, dtype=np.float32) * 2.0 / (2 * half))).astype(np.float32)
    ang_r = (t // GRID_W).astype(np.float32)[:, None] * inv[None, :]
    ang_c = (t % GRID_W).astype(np.float32)[:, None] * inv[None, :]
    cr, sr, cc, sc = np.cos(ang_r), np.sin(ang_r), np.cos(ang_c), np.sin(ang_c)
    z = np.zeros_like(cr)
    cos = np.concatenate([cr, cr, cc, cc], axis=1)
    sa = np.concatenate([-sr, z, -sc, z], axis=1)
    sb = np.concatenate([z, sr, z, sc], axis=1)

    def full(tab, fill):
        tab = np.tile(tab, (1, 2))
        ident = np.full((CTX, 128), fill, np.float32)
        return jnp.asarray(np.concatenate([tab, ident], axis=0).astype(np.float32))

    return full(cos, 1.0), full(sa, 0.0), full(sb, 0.0)


def _proj0(x2, c2, mods, g, wa, qng, wqb, kvng, wkvb, nb):
    n_tiles = nb * SMP_TILES
    rows = nb * SMP_ROWS
    cos, sa, sb = _rope_tables()
    tile_map = lambda i: (i, 0)
    const = lambda i: (0, 0)
    tab_spec = pl.BlockSpec((TILE, 128), lambda i: (i % SMP_TILES, 0))
    widths = (512, 512, 512, 1024, 1024, 512)
    return pl.pallas_call(
        _proj0_kernel,
        grid=(n_tiles,),
        in_specs=[pl.BlockSpec((TILE, D), _x_lat_map),
                  pl.BlockSpec((TILE, D), _x_ctx_map),
                  pl.BlockSpec((1, 1, 6 * D), _mod_map_all(nb)),
                  pl.BlockSpec((1, D), const),
                  pl.BlockSpec(wa.shape, const),
                  pl.BlockSpec((1, MLA_Q_LORA), const),
                  pl.BlockSpec(wqb.shape, const),
                  pl.BlockSpec((1, MLA_KV_LORA), const),
                  pl.BlockSpec(wkvb.shape, const),
                  tab_spec, tab_spec, tab_spec],
        out_specs=[pl.BlockSpec((TILE, w), tile_map) for w in widths],
        out_shape=[jax.ShapeDtypeStruct((rows, w), BF16) for w in widths],
        compiler_params=_params(("arbitrary",)),
        name="proj_even",
    )(x2, c2, mods, g, wa, qng, wqb, kvng, wkvb, cos, sa, sb)


def _softmax_parts(s, scale):
    c = scale * LOG2_E
    m = jnp.max(s, axis=-1, keepdims=True)
    p = jnp.exp2(s * c - m * c)
    return p, jnp.sum(p, axis=-1, keepdims=True)


def _attn0_kernel(lam_init, qd_ref, kd_ref, vd_ref, qm_ref, km_ref, vm_ref, lam_ref, sg_ref, o_ref):
    lv = lam_ref[...]
    lam = (jnp.exp(jnp.sum(lv[0:1] * lv[1:2], axis=1, keepdims=True))
           - jnp.exp(jnp.sum(lv[2:3] * lv[3:4], axis=1, keepdims=True)) + lam_init)
    lane = lax.broadcasted_iota(jnp.int32, (TILE, 128), 1)

    def heads(k_lo, nk):
        def diff_scores(hd):
            sl = slice(hd * 128, (hd + 1) * 128)
            q = qd_ref[:, sl]
            zero = jnp.zeros_like(q)
            qq = jnp.concatenate([jnp.where(lane < DIFF_DH, q, zero),
                                  jnp.where(lane >= DIFF_DH, q, zero)], axis=0)
            return _dot_nt(qq, kd_ref[k_lo:k_lo + nk, sl])

        def diff_finish(hd, s):
            sl = slice(hd * 128, (hd + 1) * 128)
            p, l = _softmax_parts(s, DIFF_SCALE)
            pv = _dot(p.astype(BF16), vd_ref[k_lo:k_lo + nk, sl])
            o = pv[:TILE] / l[:TILE] - lam * (pv[TILE:] / l[TILE:])
            o_ref[:, sl] = (_rms(o, sg_ref[...]) * (1.0 - lam_init)).astype(BF16)

        def mla_scores(hd):
            ql = slice(hd * MLA_QK_PAD, (hd + 1) * MLA_QK_PAD)
            return _dot_nt(qm_ref[:, ql], km_ref[k_lo:k_lo + nk, ql])

        def mla_finish(hd, s):
            p, l = _softmax_parts(s, MLA_SCALE)
            pv = _dot(p.astype(BF16), vm_ref[k_lo:k_lo + nk, hd * 128:(hd + 1) * 128])
            o_ref[:, 512 + hd * 128:512 + (hd + 1) * 128] = (pv / l).astype(BF16)

        jobs = ([(diff_scores, diff_finish, hd) for hd in range(DIFF_HEADS)]
                + [(mla_scores, mla_finish, hd) for hd in range(MLA_HEADS)])
        s_next = jobs[0][0](jobs[0][2])
        for n, (_, finish, hd) in enumerate(jobs):
            s = s_next
            if n + 1 < len(jobs):
                s_next = jobs[n + 1][0](jobs[n + 1][2])
            finish(hd, s)

    is_ctx = pl.program_id(1) == LAT_TILES

    @pl.when(jnp.logical_not(is_ctx))
    def _():
        heads(0, SMP_ROWS)

    @pl.when(is_ctx)
    def _():
        heads(SEQ, CTX)


def _attn0(qd, kd, vd, qm, km, vm, lam_vec, subln_g, lam_init, nb):
    q_map = lambda b, p: (b * SMP_TILES + p, 0)
    k_map = lambda b, p: (b, 0)
    const = lambda b, p: (0, 0)
    return pl.pallas_call(
        functools.partial(_attn0_kernel, lam_init),
        grid=(nb, SMP_TILES),
        in_specs=[pl.BlockSpec((TILE, 512), q_map),
                  pl.BlockSpec((SMP_ROWS, 512), k_map),
                  pl.BlockSpec((SMP_ROWS, 512), k_map),
                  pl.BlockSpec((TILE, 1024), q_map),
                  pl.BlockSpec((SMP_ROWS, 1024), k_map),
                  pl.BlockSpec((SMP_ROWS, 512), k_map),
                  pl.BlockSpec(lam_vec.shape, const),
                  pl.BlockSpec((1, 128), const)],
        out_specs=pl.BlockSpec((TILE, D), q_map),
        out_shape=jax.ShapeDtypeStruct((nb * SMP_ROWS, D), BF16),
        compiler_params=_params(("arbitrary", "arbitrary")),
        name="attn_even",
    )(qd, kd, vd, qm, km, vm, lam_vec, subln_g)


def _oproj_kernel(dual, *refs):
    if dual:
        (o_ref, xl_ref, xc_ref, mod_ref, wo_ref, pg_ref, fg_ref, rw_ref, rb_ref,
         xn_ref, hf_ref, mi_ref, mw_ref, seg_ref) = refs
        is_ctx = pl.program_id(0) % SMP_TILES == LAT_TILES
        x = jnp.where(is_ctx, xc_ref[...], xl_ref[...])
    else:
        (o_ref, x_ref, mod_ref, wo_ref, pg_ref, fg_ref, rw_ref, rb_ref,
         xn_ref, hf_ref, mi_ref, mw_ref, seg_ref) = refs
        x = x_ref[...]

    his, los = [], []
    for rs in (slice(0, TILE // 2), slice(TILE // 2, TILE)):
        y = _dot(o_ref[rs, :], wo_ref[...])
        xn = x[rs] + _mod(mod_ref, 2) * _rms(y, pg_ref[...])
        xn_ref[rs, :] = xn
        hf = _rms(xn, fg_ref[...]) * (1.0 + _mod(mod_ref, 4)) + _mod(mod_ref, 3)
        hi = hf.astype(BF16)
        hf_ref[rs, :] = hi
        his.append(hi)
        los.append((hf - hi.astype(F32)).astype(BF16))

    lt = _dot_nt(rw_ref[...], jnp.concatenate(his + los, axis=0))
    ne = N_EXPERTS
    logits = (lt[:ne, :TILE] + lt[ne:, :TILE]) + (lt[:ne, TILE:] + lt[ne:, TILE:]) + rb_ref[...]

    eid = lax.broadcasted_iota(jnp.int32, (ne, TILE), 0).astype(F32)
    sels, vals = [], []
    for _ in range(TOP_K):
        m = jnp.max(logits, axis=0, keepdims=True)
        idx = jnp.min(jnp.where(logits == m, eid, float(ne)), axis=0, keepdims=True)
        sel = eid == idx
        sels.append(sel)
        vals.append(m)
        logits = jnp.where(sel, -jnp.inf, logits)
    ex = [jnp.exp(v - vals[0]) for v in vals]
    den = ex[0] + ex[1] + ex[2] + ex[3]

    onehot = (sels[0] | sels[1] | sels[2] | sels[3]).astype(F32)
    r_i = lax.broadcasted_iota(jnp.int32, (TILE, TILE), 0)
    c_i = lax.broadcasted_iota(jnp.int32, (TILE, TILE), 1)
    before = _dot(onehot.astype(BF16), (r_i < c_i).astype(BF16))
    cnt = jnp.sum(onehot, axis=1, keepdims=True)
    seg_len = jnp.floor((cnt + (SEG_ALIGN - 1)) * (1.0 / SEG_ALIGN)) * SEG_ALIGN
    e_r = lax.broadcasted_iota(jnp.int32, (ne, ne), 0)
    e_c = lax.broadcasted_iota(jnp.int32, (ne, ne), 1)
    seg_len_b = jnp.broadcast_to(seg_len, (ne, META_LANES))
    seg_units = (seg_len_b * (1.0 / SEG_ALIGN)).astype(BF16)
    seg_off = _dot((e_c < e_r).astype(BF16), seg_units)[:, 0:1] * SEG_ALIGN
    pos = before + seg_off
    sub = lax.broadcasted_iota(jnp.int32, (8, TILE), 0)
    mi = jnp.zeros((8, TILE), jnp.int32)
    mw = jnp.zeros((8, TILE), F32)
    for k in range(TOP_K):
        row = jnp.sum(jnp.where(sels[k], pos, 0.0), axis=0, keepdims=True).astype(jnp.int32)
        mi = jnp.where(sub == k, row, mi)
        mw = jnp.where(sub == k, ex[k] / den, mw)
    mi_ref[0] = mi
    mw_ref[0] = mw
    seg_ref[0] = seg_len_b.astype(jnp.int32)


def _oproj(o, x_args, mods, wo, pg, fg, rw, rb, nb, all_tokens):
    const = lambda i: (0, 0)
    tile_map = lambda i: (i, 0)
    tile3_map = lambda i: (i, 0, 0)
    rw_hi = rw.astype(BF16)
    rw_lo = (rw - rw_hi.astype(F32)).astype(BF16)
    rw_split = jnp.concatenate([rw_hi.T, rw_lo.T], axis=0)
    if all_tokens:
        n_tiles = nb * SMP_TILES
        x_specs = [pl.BlockSpec((TILE, D), _x_lat_map), pl.BlockSpec((TILE, D), _x_ctx_map)]
        mod_map = _mod_map_all(nb)
    else:
        n_tiles = nb * LAT_TILES
        x_specs = [pl.BlockSpec((TILE, D), lambda i: (_lat_of_all(i), 0))]
        mod_map = lambda i: (i // LAT_TILES, 0, 0)
    rows = n_tiles * TILE
    return pl.pallas_call(
        functools.partial(_oproj_kernel, all_tokens),
        grid=(n_tiles,),
        in_specs=[pl.BlockSpec((TILE, D), tile_map)] + x_specs + [
            pl.BlockSpec((1, 1, 6 * D), mod_map),
            pl.BlockSpec((D, D), const),
            pl.BlockSpec((1, D), const),
            pl.BlockSpec((1, D), const),
            pl.BlockSpec((2 * N_EXPERTS, D), const),
            pl.BlockSpec((N_EXPERTS, 1), const)],
        out_specs=[pl.BlockSpec((TILE, D), tile_map),
                   pl.BlockSpec((TILE, D), tile_map),
                   pl.BlockSpec((1, 8, TILE), tile3_map),
                   pl.BlockSpec((1, 8, TILE), tile3_map),
                   pl.BlockSpec((1, N_EXPERTS, META_LANES), tile3_map)],
        out_shape=[jax.ShapeDtypeStruct((rows, D), F32),
                   jax.ShapeDtypeStruct((rows, D), BF16),
                   jax.ShapeDtypeStruct((n_tiles, 8, TILE), jnp.int32),
                   jax.ShapeDtypeStruct((n_tiles, 8, TILE), F32),
                   jax.ShapeDtypeStruct((n_tiles, N_EXPERTS, META_LANES), jnp.int32)],
        compiler_params=_params(("arbitrary",)),
        name="oproj_router",
    )(o, *x_args, mods, wo, pg, fg, rw_split, rb.reshape(N_EXPERTS, 1))


U32 = jnp.uint32
HALF = D // 2


def _pack_rows(x):
    bits = pltpu.bitcast(x.astype(BF16).astype(F32), U32)
    return (bits[:, :HALF] >> 16) | (bits[:, HALF:] & jnp.uint32(0xFFFF0000))


def _unpack_rows(w):
    lo = pltpu.bitcast(w << 16, F32).astype(BF16)
    hi = pltpu.bitcast(w & jnp.uint32(0xFFFF0000), F32).astype(BF16)
    return jnp.concatenate([lo, hi], axis=1)


def _for_pieces(n, largest, fn):
    piece = largest
    while piece >= SEG_ALIGN:
        done = (n // (2 * piece)) * (2 * piece)
        pl.when((n & piece) != 0)(functools.partial(fn, done, piece))
        piece //= 2


PIECES = tuple(TILE >> b for b in range((TILE // SEG_ALIGN).bit_length()))
N_PIECES = len(PIECES)
LIST_ROWS = 2 * N_PIECES + 1


def _piece_lists(seg_len, seg_local, seg_first):
    sizes = jnp.asarray(PIECES, jnp.int32)[None, :, None]
    n = seg_len[:, None, :]
    active = (n & sizes) != 0
    done = n // (2 * sizes) * (2 * sizes)
    slot = jnp.cumsum(active, axis=-1) - 1
    put = active[..., :, None] & (slot[..., :, None] == jnp.arange(N_EXPERTS))
    compact = lambda v: jnp.sum(jnp.where(put, v[..., :, None], 0), axis=-2)
    loc = compact(seg_local[:, None, :] + done)
    glb = compact(seg_first[:, None, :] + done)
    counts = jnp.sum(active, axis=-1)
    tail = jnp.concatenate([counts, jnp.sum(seg_len, axis=-1, keepdims=True)], axis=-1)
    tail = jnp.pad(tail, ((0, 0), (0, N_EXPERTS - tail.shape[-1])))[:, None, :]
    return jnp.concatenate([loc, glb, tail], axis=1).astype(jnp.int32)


STEP_TILES = 2


def _start_segment_copies(list_ref, u, local_ref, hbm_ref, sem, to_hbm):
    for b, rows in enumerate(PIECES):
        def start(j, carry, b=b, rows=rows):
            loc = local_ref.at[pl.ds(pl.multiple_of(list_ref[u, b, j], SEG_ALIGN), rows)]
            glb = hbm_ref.at[pl.ds(pl.multiple_of(list_ref[u, N_PIECES + b, j], SEG_ALIGN), rows)]
            (pltpu.make_async_copy(loc, glb, sem) if to_hbm else pltpu.make_async_copy(glb, loc, sem)).start()
            return carry

        lax.fori_loop(0, list_ref[u, 2 * N_PIECES, b], start, 0)


def _wait_segment_copies(list_ref, u, local_ref, hbm_ref, sem, to_hbm):
    def wait(done, rows):
        del done
        loc = local_ref.at[pl.ds(0, rows)]
        glb = hbm_ref.at[pl.ds(0, rows)]
        (pltpu.make_async_copy(loc, glb, sem) if to_hbm else pltpu.make_async_copy(glb, loc, sem)).wait()

    _for_pieces(list_ref[u, 2 * N_PIECES, N_PIECES], SORT_ROWS_POW2, wait)


def _zero_fill_copies(tail_ref, free_ref, zbuf, xs_ref, sem, action):
    def per_expert(e, carry):
        def piece_copy(done, rows):
            dst = xs_ref.at[pl.ds(pl.multiple_of(tail_ref[0, e] + done, SEG_ALIGN), rows)]
            action(pltpu.make_async_copy(zbuf.at[pl.ds(0, rows)], dst, sem))

        _for_pieces(tail_ref[1, e], EXPERT_BLOCK // 2, piece_copy)
        return carry

    def per_block(j, carry):
        first = pl.multiple_of(free_ref[0] + j * EXPERT_BLOCK, EXPERT_BLOCK)
        action(pltpu.make_async_copy(zbuf, xs_ref.at[pl.ds(first, EXPERT_BLOCK)], sem))
        return carry

    lax.fori_loop(0, N_EXPERTS, per_expert, 0)
    lax.fori_loop(0, free_ref[1], per_block, 0)


def _dispatch_kernel(seg_ref, seg_prev_ref, tail_ref, free_ref, lp_ref, h_ref, xs_ref, sbuf, zbuf, sem):
    i = pl.program_id(0)
    last = pl.num_programs(0) - 1
    slot = i % 2
    row = lax.broadcasted_iota(jnp.int32, (SORT_ROWS, TILE), 0)
    tiles = range(STEP_TILES)
    for u in tiles:
        lp = lp_ref[u]
        hit = row == lp[0:1]
        for k in range(1, TOP_K):
            hit = hit | (row == lp[k:k + 1])
        sbuf[slot, u] = _pack_rows(_dot(jnp.where(hit, 1.0, 0.0).astype(BF16),
                                        h_ref[u * TILE:(u + 1) * TILE, :]))
    for u in tiles:
        _start_segment_copies(seg_ref, u, sbuf.at[slot, u], xs_ref, sem.at[slot], True)

    @pl.when(i > 0)
    def _():
        for u in tiles:
            _wait_segment_copies(seg_prev_ref, u, sbuf.at[1 - slot, u], xs_ref, sem.at[1 - slot], True)

    @pl.when(i == last)
    def _():
        for u in tiles:
            _wait_segment_copies(seg_ref, u, sbuf.at[slot, u], xs_ref, sem.at[slot], True)
        zbuf[...] = jnp.zeros_like(zbuf)
        _zero_fill_copies(tail_ref, free_ref, zbuf, xs_ref, sem.at[slot], lambda cp: cp.start())
        _zero_fill_copies(tail_ref, free_ref, zbuf, xs_ref, sem.at[slot], lambda cp: cp.wait())


def _dispatch(segs, tails, free, lpos_t, hf, m_rows):
    n_steps = hf.shape[0] // (STEP_TILES * TILE)
    seg_block = (STEP_TILES, LIST_ROWS, N_EXPERTS)
    return pl.pallas_call(
        _dispatch_kernel,
        grid=(n_steps,),
        in_specs=[pl.BlockSpec(seg_block, lambda i: (i, 0, 0), memory_space=pltpu.SMEM),
                  pl.BlockSpec(seg_block, lambda i: (jnp.maximum(i - 1, 0), 0, 0), memory_space=pltpu.SMEM),
                  pl.BlockSpec(memory_space=pltpu.SMEM),
                  pl.BlockSpec(memory_space=pltpu.SMEM),
                  pl.BlockSpec((STEP_TILES, 8, TILE), lambda i: (i, 0, 0)),
                  pl.BlockSpec((STEP_TILES * TILE, D), lambda i: (i, 0))],
        out_specs=pl.BlockSpec(memory_space=pl.ANY),
        out_shape=jax.ShapeDtypeStruct((m_rows, HALF), U32),
        scratch_shapes=[pltpu.VMEM((2, STEP_TILES, SORT_ROWS, HALF), U32),
                        pltpu.VMEM((EXPERT_BLOCK, HALF), U32), pltpu.SemaphoreType.DMA((2,))],
        compiler_params=_params(("arbitrary",)),
        name="moe_dispatch",
    )(segs, segs, tails, free, lpos_t, hf)


def _expert_kernel(be_ref, nu_ref, xs_ref, w1_ref, b1_ref, w2_ref, b2_ref, ys_ref, w1b, w2b):
    i = pl.program_id(0)
    used = i < nu_ref[0]
    new_expert = jnp.logical_or(i == 0, be_ref[i] != be_ref[jnp.maximum(i - 1, 0)])

    @pl.when(jnp.logical_and(used, new_expert))
    def _():
        w1b[...] = w1_ref[0, 0].astype(BF16)
        w2b[...] = w2_ref[0, 0].astype(BF16)

    @pl.when(used)
    def _():
        gu = _dot(_unpack_rows(xs_ref[...]), w1b[...]) + b1_ref[0, 0]
        gate = jnp.minimum(gu[:, :D_FF], SWIGLU_LIMIT)
        lin = jnp.clip(gu[:, D_FF:], -SWIGLU_LIMIT, SWIGLU_LIMIT)
        act = gate * jax.nn.sigmoid(SWIGLU_ALPHA * gate) * (lin + 1.0)
        ys_ref[...] = _pack_rows(_dot(act.astype(BF16), w2b[...]) + b2_ref[0, 0])

    @pl.when(i >= nu_ref[0])
    def _():
        ys_ref[...] = jnp.zeros_like(ys_ref)


def _experts(blk_e, n_used, xs, layer, w1, b1, w2, b2):
    n_blocks = xs.shape[0] // EXPERT_BLOCK
    in_row_map = lambda i, be, nu: (jnp.minimum(i, nu[0] - 1), 0)
    e_map = lambda i, be, nu: (layer, be[i], 0, 0)
    grid_spec = pltpu.PrefetchScalarGridSpec(
        num_scalar_prefetch=2,
        grid=(n_blocks,),
        in_specs=[pl.BlockSpec((EXPERT_BLOCK, HALF), in_row_map),
                  pl.BlockSpec((1, 1, D, 2 * D_FF), e_map),
                  pl.BlockSpec((1, 1, 1, 2 * D_FF), e_map),
                  pl.BlockSpec((1, 1, D_FF, D), e_map),
                  pl.BlockSpec((1, 1, 1, D), e_map)],
        out_specs=pl.BlockSpec((EXPERT_BLOCK, HALF), lambda i, be, nu: (i, 0)),
        scratch_shapes=[pltpu.VMEM((D, 2 * D_FF), BF16), pltpu.VMEM((D_FF, D), BF16)])
    return pl.pallas_call(
        _expert_kernel,
        grid_spec=grid_spec,
        out_shape=jax.ShapeDtypeStruct(xs.shape, U32),
        compiler_params=_params(("arbitrary",)),
        name="moe_experts",
    )(blk_e, n_used, xs, w1, b1[:, :, None, :], w2, b2[:, :, None, :])


def _combine_kernel(seg_ref, seg_next_ref, mi_ref, mw_ref, x_ref, g_ref, ys_ref, *rest):
    mod_refs, (o_ref, ybuf, sem) = rest[:STEP_TILES], rest[STEP_TILES:]
    i = pl.program_id(0)
    slot = i % 2
    tiles = range(STEP_TILES)

    @pl.when(i == 0)
    def _():
        ybuf[...] = jnp.zeros_like(ybuf)
        for u in tiles:
            _start_segment_copies(seg_ref, u, ybuf.at[0, u], ys_ref, sem.at[0], False)

    @pl.when(i + 1 < pl.num_programs(0))
    def _():
        for u in tiles:
            _start_segment_copies(seg_next_ref, u, ybuf.at[1 - slot, u], ys_ref, sem.at[1 - slot], False)

    col = lax.broadcasted_iota(jnp.int32, (TILE, SORT_ROWS), 1)
    wms = []
    for u in tiles:
        rs = slice(u * TILE, (u + 1) * TILE)
        mi = mi_ref[rs, :]
        mw = mw_ref[rs, :]
        wm = jnp.zeros((TILE, SORT_ROWS), F32)
        for k in range(TOP_K):
            wm = jnp.where(col == mi[:, k:k + 1], mw[:, k:k + 1], wm)
        wms.append(wm.astype(BF16))
    for u in tiles:
        _wait_segment_copies(seg_ref, u, ybuf.at[slot, u], ys_ref, sem.at[slot], False)
    for u in tiles:
        rs = slice(u * TILE, (u + 1) * TILE)
        y = _dot(wms[u], _unpack_rows(ybuf[slot, u]))
        o_ref[rs, :] = x_ref[rs, :] + _mod(mod_refs[u], 5) * _rms(y, g_ref[...])


def _combine(segs, mi, mw, xn, mods, g, ys, nb, all_tokens):
    n_steps = xn.shape[0] // (STEP_TILES * TILE)
    step_map = lambda i: (i, 0)
    mod_map = _mod_map_all(nb) if all_tokens else (lambda i: (i // LAT_TILES, 0, 0))
    seg_block = (STEP_TILES, LIST_ROWS, N_EXPERTS)
    rows = STEP_TILES * TILE
    mod_specs = [pl.BlockSpec((1, 1, 6 * D), lambda i, u=u: mod_map(i * STEP_TILES + u))
                 for u in range(STEP_TILES)]
    return pl.pallas_call(
        _combine_kernel,
        grid=(n_steps,),
        in_specs=[pl.BlockSpec(seg_block, lambda i: (i, 0, 0), memory_space=pltpu.SMEM),
                  pl.BlockSpec(seg_block, lambda i: (jnp.minimum(i + 1, n_steps - 1), 0, 0),
                               memory_space=pltpu.SMEM),
                  pl.BlockSpec((rows, 8), step_map),
                  pl.BlockSpec((rows, 8), step_map),
                  pl.BlockSpec((rows, D), step_map),
                  pl.BlockSpec((1, D), lambda i: (0, 0)),
                  pl.BlockSpec(memory_space=pl.ANY)] + mod_specs,
        out_specs=pl.BlockSpec((rows, D), step_map),
        out_shape=jax.ShapeDtypeStruct(xn.shape, F32),
        scratch_shapes=[pltpu.VMEM((2, STEP_TILES, SORT_ROWS, HALF), U32), pltpu.SemaphoreType.DMA((2,))],
        compiler_params=_params(("arbitrary",)),
        name="moe_combine",
    )(segs, segs, mi, mw, xn, g, ys, *([mods] * STEP_TILES))


def _moe(hf, mi, mw, seg, xn, mods, g, layer, w1, b1, w2, b2, nb, all_tokens):
    t = hf.shape[0]
    n_tiles = t // TILE
    assert n_tiles % STEP_TILES == 0, "dispatch / combine take STEP_TILES token tiles per grid step"
    rows_max = t * TOP_K + n_tiles * N_EXPERTS * (SEG_ALIGN - 1)
    n_blocks = -(-rows_max // EXPERT_BLOCK) + N_EXPERTS
    seg_len = seg[:, :, 0]
    counts = jnp.sum(seg_len, axis=0)
    padded = (counts + EXPERT_BLOCK - 1) // EXPERT_BLOCK * EXPERT_BLOCK
    pad_end = jnp.cumsum(padded)
    pad_start = pad_end - padded
    seg_first = pad_start[None, :] + jnp.cumsum(seg_len, axis=0) - seg_len
    seg_local = jnp.cumsum(seg_len, axis=1) - seg_len
    segs = _piece_lists(seg_len, seg_local, seg_first)
    by_token = lambda a: a.transpose(0, 2, 1).reshape(t, 8)
    blk_row = jnp.arange(n_blocks, dtype=jnp.int32) * EXPERT_BLOCK
    blk_e = jnp.minimum(jnp.sum(pad_end[None, :] <= blk_row[:, None], axis=1), N_EXPERTS - 1).astype(jnp.int32)
    n_used = (pad_end[-1:] // EXPERT_BLOCK).astype(jnp.int32)
    tails = jnp.stack([pad_start + counts, padded - counts]).astype(jnp.int32)
    free = jnp.concatenate([pad_end[-1:], n_blocks - n_used]).astype(jnp.int32)
    xs = _dispatch(segs, tails, free, mi, hf, n_blocks * EXPERT_BLOCK)
    ys = _experts(blk_e, n_used, xs, layer, w1, b1, w2, b2)
    return _combine(segs, by_token(mi), by_token(mw), xn, mods, g, ys, nb, all_tokens)


def _proj1_kernel(x_ref, mod_ref, g_ref, w_ref, q_ref, k_ref, v_ref):
    h = (_rms(x_ref[...], g_ref[...]) * (1.0 + _mod(mod_ref, 1)) + _mod(mod_ref, 0)).astype(BF16)
    qkv = _dot(h, w_ref[...])
    q_ref[...] = (qkv[:, :D] * NA_SCALE).astype(BF16)
    k_ref[...] = qkv[:, D:2 * D].astype(BF16)
    v_ref[...] = qkv[:, 2 * D:].astype(BF16)


def _proj1(x1, mods, g, w, nb):
    n_tiles = nb * SMP_TILES
    tile_map = lambda i: (i, 0)
    return pl.pallas_call(
        _proj1_kernel,
        grid=(n_tiles,),
        in_specs=[pl.BlockSpec((TILE, D), tile_map),
                  pl.BlockSpec((1, 1, 6 * D), _mod_map_all(nb)),
                  pl.BlockSpec((1, D), lambda i: (0, 0)),
                  pl.BlockSpec((D, 3 * D), lambda i: (0, 0))],
        out_specs=[pl.BlockSpec((TILE, D), tile_map)] * 3,
        out_shape=[jax.ShapeDtypeStruct((n_tiles * TILE, D), BF16)] * 3,
        compiler_params=_params(("arbitrary",)),
        name="proj_odd",
    )(x1, mods, g, w)


def _na_row_start(r):
    return jnp.clip(r - WIN_ROWS // 2, 0, GRID_H - WIN_ROWS)


NA_STEP_ROWS = 4


def _na_kernel(q_ref, k_ref, v_ref, *rest):
    bias_refs, o_ref = rest[:NA_STEP_ROWS], rest[NA_STEP_ROWS]
    n_loc = WIN_ROWS * GRID_W
    lane = lax.broadcasted_iota(jnp.int32, (GRID_W, 128), 1)
    pairs = range(NA_HEADS // 2)
    sls = [slice(j * 128, (j + 1) * 128) for j in pairs]
    k0s, qqs = [], []
    for t in range(NA_STEP_ROWS):
        r = pl.program_id(1) * NA_STEP_ROWS + t
        k0s.append(pl.multiple_of(_na_row_start(r) * GRID_W, GRID_W))
        for j in pairs:
            q = q_ref[t * GRID_W:(t + 1) * GRID_W, sls[j]]
            zero = jnp.zeros_like(q)
            qqs.append(jnp.concatenate([jnp.where(lane < NA_DH, q, zero),
                                        jnp.where(lane >= NA_DH, q, zero)], axis=0))
    jobs = [(t, j) for t in range(NA_STEP_ROWS) for j in pairs]
    s_loc = jnp.concatenate([_dot_nt(qqs[n], k_ref[pl.ds(k0s[t], n_loc), sls[j]])
                             for n, (t, j) in enumerate(jobs)], axis=0)
    s_loc = s_loc + jnp.concatenate([b[0] for b in bias_refs], axis=0)
    s_ctx = jnp.concatenate([_dot_nt(qqs[n], k_ref[SEQ:, sls[j]]) for n, (t, j) in enumerate(jobs)], axis=0)
    m = jnp.maximum(jnp.max(s_loc, axis=-1, keepdims=True), jnp.max(s_ctx, axis=-1, keepdims=True))
    p_loc = jnp.exp(s_loc - m)
    p_ctx = jnp.exp(s_ctx - m)
    inv_l = 1.0 / (jnp.sum(p_loc, axis=-1, keepdims=True) + jnp.sum(p_ctx, axis=-1, keepdims=True))
    p_loc = p_loc.astype(BF16)
    p_ctx = p_ctx.astype(BF16)
    for n, (t, j) in enumerate(jobs):
        rows = slice(n * 2 * GRID_W, (n + 1) * 2 * GRID_W)
        pv = (_dot(p_loc[rows], v_ref[pl.ds(k0s[t], n_loc), sls[j]])
              + _dot(p_ctx[rows], v_ref[SEQ:, sls[j]])) * inv_l[rows]
        o_ref[t * GRID_W:(t + 1) * GRID_W, sls[j]] = jnp.where(lane < NA_DH, pv[:GRID_W],
                                                               pv[GRID_W:]).astype(BF16)


def _na_bias(rpb):
    mid = WIN_ROWS // 2
    pat_rows = list(range(mid)) + [mid] + list(range(GRID_H - mid + 1, GRID_H))
    r = np.array(pat_rows)
    rs = np.clip(r - mid, 0, GRID_H - WIN_ROWS)
    row_off = rs[:, None] + np.arange(WIN_ROWS)[None, :] - r[:, None] + WIN_ROWS - 1
    c = np.arange(GRID_W)
    q_start = np.clip(c - WIN_COLS // 2, 0, GRID_W - WIN_COLS)[:, None]
    kc = np.arange(GRID_W)[None, :]
    valid = (kc >= q_start) & (kc < q_start + WIN_COLS)
    col_off = np.clip(kc - c[:, None] + WIN_COLS - 1, 0, 2 * WIN_COLS - 2)
    sel_r = np.eye(2 * WIN_ROWS - 1, dtype=np.float32)[row_off]
    sel_c = np.eye(2 * WIN_COLS - 1, dtype=np.float32)[col_off]
    b = jnp.einsum('pia,hab,ckb->phcik', sel_r, rpb.astype(F32), sel_c, precision=lax.Precision.HIGHEST)
    b = jnp.where(valid[None, None, :, None, :], b, MASK_VALUE)
    return b.reshape(len(pat_rows), NA_HEADS * GRID_W, WIN_ROWS * GRID_W)


def _na_pattern(r):
    mid = WIN_ROWS // 2
    return jnp.where(r < mid, r, jnp.where(r <= GRID_H - mid, mid, r - (GRID_H - 2 * mid)))


def _na(q, k, v, bias, nb):
    q_rows = NA_STEP_ROWS * GRID_W
    steps = GRID_H // NA_STEP_ROWS
    blocks_per_smp = SMP_ROWS // q_rows
    bias_specs = [pl.BlockSpec((1,) + bias.shape[1:],
                               lambda b, s, t=t: (_na_pattern(s * NA_STEP_ROWS + t), 0, 0))
                  for t in range(NA_STEP_ROWS)]
    return pl.pallas_call(
        _na_kernel,
        grid=(nb, steps),
        in_specs=[pl.BlockSpec((q_rows, D), lambda b, s: (b * blocks_per_smp + s, 0)),
                  pl.BlockSpec((SMP_ROWS, D), lambda b, s: (b, 0)),
                  pl.BlockSpec((SMP_ROWS, D), lambda b, s: (b, 0))] + bias_specs,
        out_specs=pl.BlockSpec((q_rows, D), lambda b, s: (b * steps + s, 0)),
        out_shape=jax.ShapeDtypeStruct((nb * SEQ, D), BF16),
        compiler_params=_params(("arbitrary", "arbitrary")),
        name="na_attn",
    )(q, k, v, *([bias] * NA_STEP_ROWS))


def kernel(x, c, ctx, c_ctx, ada_w, ada_b, mix_pre_g, mix_post_g, ffn_pre_g, ffn_post_g, even_w_in,
           diff_lambda, diff_subln_g, mla_q_norm_g, mla_w_qb, mla_kv_norm_g, mla_w_kvb, even_w_out,
           na_w_qkv, na_rpb, na_w_out, router_w, router_b, moe_w1, moe_b1, moe_w2, moe_b2):
    nb = x.shape[0]
    assert x.shape[1:] == (SEQ, D) and ctx.shape[1:] == (CTX, D)
    x2 = x.reshape(nb * SEQ, D)
    c2 = ctx.reshape(nb * CTX, D)
    row = lambda a: a.reshape(1, -1)

    mods = _ada(jnp.concatenate([c, c_ctx[None, :]], axis=0), ada_w, ada_b)
    mods0 = mods[0].reshape(nb + 1, 1, 6 * D)
    mods1 = mods[1].reshape(nb + 1, 1, 6 * D)

    lam_init = 0.8 - 0.6 * math.exp(-0.3 * 0)
    w_in = even_w_in[0]
    wa = jnp.pad(w_in, ((0, 0), (0, 2048 - w_in.shape[1]))).astype(BF16)
    wqb = mla_w_qb[0].reshape(MLA_Q_LORA, MLA_HEADS, MLA_NOPE + MLA_ROPE)
    wqb = jnp.pad(wqb, ((0, 0), (0, 0), (0, MLA_QK_PAD - MLA_NOPE - MLA_ROPE)))
    wqb = wqb.reshape(MLA_Q_LORA, MLA_HEADS * MLA_QK_PAD).astype(BF16)
    qd, kd, vd, qm, km, vm = _proj0(x2, c2, mods0, row(mix_pre_g[0]), wa, row(mla_q_norm_g[0]), wqb,
                                    row(mla_kv_norm_g[0]), mla_w_kvb[0].astype(BF16), nb)
    o0 = _attn0(qd, kd, vd, qm, km, vm, diff_lambda[0], row(diff_subln_g[0]), lam_init, nb)
    xn, hf, mi, mw, seg = _oproj(o0, (x2, c2), mods0, even_w_out[0].astype(BF16), row(mix_post_g[0]),
                                 row(ffn_pre_g[0]), router_w[0], row(router_b[0]), nb, True)
    x1 = _moe(hf, mi, mw, seg, xn, mods0, row(ffn_post_g[0]), 0, moe_w1, moe_b1, moe_w2, moe_b2, nb, True)

    q, k, v = _proj1(x1, mods1, row(mix_pre_g[1]), na_w_qkv[0].astype(BF16), nb)
    o1 = _na(q, k, v, _na_bias(na_rpb[0]), nb)
    xn, hf, mi, mw, seg = _oproj(o1, (x1,), mods1, na_w_out[0].astype(BF16), row(mix_post_g[1]),
                                 row(ffn_pre_g[1]), router_w[1], row(router_b[1]), nb, False)
    out = _moe(hf, mi, mw, seg, xn, mods1, row(ffn_post_g[1]), 1, moe_w1, moe_b1, moe_w2, moe_b2, nb, False)
    return out.reshape(nb, SEQ, D)
```

```python
import functools
import math

import numpy as np
import jax
import jax.numpy as jnp
from jax import lax
from jax.experimental import pallas as pl
from jax.experimental.pallas import tpu as pltpu

F32 = jnp.float32
BF16 = jnp.bfloat16

D = 1024
SEQ = 2048
CTX = 256
GRID_W = 64
GRID_H = SEQ // GRID_W
TILE = 256
LAT_TILES = SEQ // TILE
SMP_TILES = LAT_TILES + 1
SMP_ROWS = SEQ + CTX
EPS = 1e-6
ROPE_BASE = 10000.0

DIFF_HEADS = 4
DIFF_DH = 64
DIFF_SCALE = DIFF_DH ** -0.5
MLA_HEADS = 4
MLA_Q_LORA = 256
MLA_KV_LORA = 128
MLA_NOPE = 128
MLA_ROPE = 64
MLA_V = 128
MLA_SCALE = (MLA_NOPE + MLA_ROPE) ** -0.5
MLA_QK_PAD = 256
NA_HEADS = 16
NA_DH = 64
NA_SCALE = NA_DH ** -0.5
WIN_ROWS = 8
WIN_COLS = 16
N_EXPERTS = 32
TOP_K = 4
D_FF = 1024
SWIGLU_ALPHA = 1.702
SWIGLU_LIMIT = 7.0
EXPERT_BLOCK = 1024
SEG_ALIGN = 8
SORT_ROWS = -(-(TILE * TOP_K + N_EXPERTS * (SEG_ALIGN - 1)) // 256) * 256
SORT_ROWS_POW2 = 1 << (SORT_ROWS.bit_length() - 1)
MASK_VALUE = -1e30
LOG2_E = math.log2(math.e)
META_LANES = 128
VMEM_LIMIT = 60 * 1024 * 1024

_NT = (((1,), (1,)), ((), ()))


def _dot(a, b):
    return jnp.dot(a, b, preferred_element_type=F32)


def _dot_nt(a, b):
    return lax.dot_general(a, b, _NT, preferred_element_type=F32)


def _dot_f32(a, b):
    return jnp.dot(a, b, preferred_element_type=F32, precision=lax.Precision.HIGHEST)


def _rms(x, g):
    return x * lax.rsqrt(jnp.mean(x * x, axis=-1, keepdims=True) + EPS) * g


def _mod(mod_ref, k):
    return mod_ref[0, :, k * D:(k + 1) * D]


def _params(sem):
    return pltpu.CompilerParams(dimension_semantics=sem, vmem_limit_bytes=VMEM_LIMIT)


def _ada_kernel(c_ref, w_ref, b_ref, o_ref):
    c = c_ref[...]
    s = c * jax.nn.sigmoid(c)
    o_ref[0] = _dot_f32(s, w_ref[0]) + b_ref[0]


def _ada(cc, ada_w, ada_b):
    depth = ada_w.shape[0]
    n = cc.shape[0]
    nt = 6 * D // D
    return pl.pallas_call(
        _ada_kernel,
        grid=(depth, nt),
        in_specs=[pl.BlockSpec((n, D), lambda l, j: (0, 0)),
                  pl.BlockSpec((1, D, D), lambda l, j: (l, 0, j)),
                  pl.BlockSpec((1, 1, D), lambda l, j: (l, 0, j))],
        out_specs=pl.BlockSpec((1, n, D), lambda l, j: (l, 0, j)),
        out_shape=jax.ShapeDtypeStruct((depth, n, 6 * D), F32),
        compiler_params=_params(("arbitrary", "arbitrary")),
        name="ada",
    )(cc, ada_w, ada_b.reshape(depth, 1, 6 * D))


def _x_lat_map(i):
    return ((i // SMP_TILES) * LAT_TILES + jnp.minimum(i % SMP_TILES, LAT_TILES - 1), 0)


def _x_ctx_map(i):
    return (i // SMP_TILES, 0)


def _mod_map_all(nb):
    return lambda i: (jnp.where(i % SMP_TILES == LAT_TILES, nb, i // SMP_TILES), 0, 0)


def _lat_of_all(i):
    return (i // LAT_TILES) * SMP_TILES + i % LAT_TILES


def _rope(x, cos, sa, sb):
    return x * cos + pltpu.roll(x, 112, 1) * sa + pltpu.roll(x, 16, 1) * sb


def _proj0_kernel(x_ref, c_ref, mod_ref, g_ref, wa_ref, qng_ref, wqb_ref, kvng_ref, wkvb_ref,
                  cos_ref, sa_ref, sb_ref, qd_ref, kd_ref, vd_ref, qm_ref, km_ref, vm_ref):
    is_ctx = pl.program_id(0) % SMP_TILES == LAT_TILES
    nd = DIFF_HEADS * 2 * DIFF_DH
    o = 3 * nd
    halves = (slice(0, TILE // 2), slice(TILE // 2, TILE))
    bigs = []
    for rs in halves:
        x = jnp.where(is_ctx, c_ref[rs, :], x_ref[rs, :])
        h = (_rms(x, g_ref[...]) * (1.0 + _mod(mod_ref, 1)) + _mod(mod_ref, 0)).astype(BF16)
        bigs.append(_dot(h, wa_ref[...]))
    for rs, big in zip(halves, bigs):
        cos, sa, sb = cos_ref[rs, :], sa_ref[rs, :], sb_ref[rs, :]
        for j in range(nd // 128):
            sl = slice(j * 128, (j + 1) * 128)
            qd_ref[rs, sl] = _rope(big[:, j * 128:(j + 1) * 128], cos, sa, sb).astype(BF16)
            kd_ref[rs, sl] = _rope(big[:, nd + j * 128:nd + (j + 1) * 128], cos, sa, sb).astype(BF16)
        vd_ref[rs, :] = big[:, 2 * nd:3 * nd].astype(BF16)
        cq = big[:, o:o + MLA_Q_LORA]
        ckv = big[:, o + MLA_Q_LORA:o + MLA_Q_LORA + MLA_KV_LORA]
        kpe = _rope(big[:, o + MLA_Q_LORA + MLA_KV_LORA:], cos, sa, sb).astype(BF16)
        qm = _dot(_rms(cq, qng_ref[...]).astype(BF16), wqb_ref[...])
        kv = _dot(_rms(ckv, kvng_ref[...]).astype(BF16), wkvb_ref[...])
        for hd in range(MLA_HEADS):
            b0 = hd * MLA_QK_PAD
            qm_ref[rs, b0:b0 + 128] = qm[:, b0:b0 + 128].astype(BF16)
            qm_ref[rs, b0 + 128:b0 + 256] = _rope(qm[:, b0 + 128:b0 + 256], cos, sa, sb).astype(BF16)
            km_ref[rs, b0:b0 + 128] = kv[:, hd * 256:hd * 256 + 128].astype(BF16)
            km_ref[rs, b0 + 128:b0 + 256] = kpe
            vm_ref[rs, hd * 128:(hd + 1) * 128] = kv[:, hd * 256 + 128:(hd + 1) * 256].astype(BF16)


def _rope_tables():
    t = np.arange(SEQ)
    half = 16
    inv = (ROPE_BASE ** (-np.arange(half, dtype=np.float32) * 2.0 / (2 * half))).astype(np.float32)
    ang_r = (t // GRID_W).astype(np.float32)[:, None] * inv[None, :]
    ang_c = (t % GRID_W).astype(np.float32)[:, None] * inv[None, :]
    cr, sr, cc, sc = np.cos(ang_r), np.sin(ang_r), np.cos(ang_c), np.sin(ang_c)
    z = np.zeros_like(cr)
    cos = np.concatenate([cr, cr, cc, cc], axis=1)
    sa = np.concatenate([-sr, z, -sc, z], axis=1)
    sb = np.concatenate([z, sr, z, sc], axis=1)

    def full(tab, fill):
        tab = np.tile(tab, (1, 2))
        ident = np.full((CTX, 128), fill, np.float32)
        return jnp.asarray(np.concatenate([tab, ident], axis=0).astype(np.float32))

    return full(cos, 1.0), full(sa, 0.0), full(sb, 0.0)


def _proj0(x2, c2, mods, g, wa, qng, wqb, kvng, wkvb, nb):
    n_tiles = nb * SMP_TILES
    rows = nb * SMP_ROWS
    cos, sa, sb = _rope_tables()
    tile_map = lambda i: (i, 0)
    const = lambda i: (0, 0)
    tab_spec = pl.BlockSpec((TILE, 128), lambda i: (i % SMP_TILES, 0))
    widths = (512, 512, 512, 1024, 1024, 512)
    return pl.pallas_call(
        _proj0_kernel,
        grid=(n_tiles,),
        in_specs=[pl.BlockSpec((TILE, D), _x_lat_map),
                  pl.BlockSpec((TILE, D), _x_ctx_map),
                  pl.BlockSpec((1, 1, 6 * D), _mod_map_all(nb)),
                  pl.BlockSpec((1, D), const),
                  pl.BlockSpec(wa.shape, const),
                  pl.BlockSpec((1, MLA_Q_LORA), const),
                  pl.BlockSpec(wqb.shape, const),
                  pl.BlockSpec((1, MLA_KV_LORA), const),
                  pl.BlockSpec(wkvb.shape, const),
                  tab_spec, tab_spec, tab_spec],
        out_specs=[pl.BlockSpec((TILE, w), tile_map) for w in widths],
        out_shape=[jax.ShapeDtypeStruct((rows, w), BF16) for w in widths],
        compiler_params=_params(("arbitrary",)),
        name="proj_even",
    )(x2, c2, mods, g, wa, qng, wqb, kvng, wkvb, cos, sa, sb)


def _softmax_parts(s, scale):
    c = scale * LOG2_E
    m = jnp.max(s, axis=-1, keepdims=True)
    p = jnp.exp2(s * c - m * c)
    return p, jnp.sum(p, axis=-1, keepdims=True)


def _attn0_kernel(lam_init, qd_ref, kd_ref, vd_ref, qm_ref, km_ref, vm_ref, lam_ref, sg_ref, o_ref):
    lv = lam_ref[...]
    lam = (jnp.exp(jnp.sum(lv[0:1] * lv[1:2], axis=1, keepdims=True))
           - jnp.exp(jnp.sum(lv[2:3] * lv[3:4], axis=1, keepdims=True)) + lam_init)
    lane = lax.broadcasted_iota(jnp.int32, (TILE, 128), 1)

    def heads(k_lo, nk):
        def diff_scores(hd):
            sl = slice(hd * 128, (hd + 1) * 128)
            q = qd_ref[:, sl]
            zero = jnp.zeros_like(q)
            qq = jnp.concatenate([jnp.where(lane < DIFF_DH, q, zero),
                                  jnp.where(lane >= DIFF_DH, q, zero)], axis=0)
            return _dot_nt(qq, kd_ref[k_lo:k_lo + nk, sl])

        def diff_finish(hd, s):
            sl = slice(hd * 128, (hd + 1) * 128)
            p, l = _softmax_parts(s, DIFF_SCALE)
            pv = _dot(p.astype(BF16), vd_ref[k_lo:k_lo + nk, sl])
            o = pv[:TILE] / l[:TILE] - lam * (pv[TILE:] / l[TILE:])
            o_ref[:, sl] = (_rms(o, sg_ref[...]) * (1.0 - lam_init)).astype(BF16)

        def mla_scores(hd):
            ql = slice(hd * MLA_QK_PAD, (hd + 1) * MLA_QK_PAD)
            return _dot_nt(qm_ref[:, ql], km_ref[k_lo:k_lo + nk, ql])

        def mla_finish(hd, s):
            p, l = _softmax_parts(s, MLA_SCALE)
            pv = _dot(p.astype(BF16), vm_ref[k_lo:k_lo + nk, hd * 128:(hd + 1) * 128])
            o_ref[:, 512 + hd * 128:512 + (hd + 1) * 128] = (pv / l).astype(BF16)

        jobs = ([(diff_scores, diff_finish, hd) for hd in range(DIFF_HEADS)]
                + [(mla_scores, mla_finish, hd) for hd in range(MLA_HEADS)])
        s_next = jobs[0][0](jobs[0][2])
        for n, (_, finish, hd) in enumerate(jobs):
            s = s_next
            if n + 1 < len(jobs):
                s_next = jobs[n + 1][0](jobs[n + 1][2])
            finish(hd, s)

    is_ctx = pl.program_id(1) == LAT_TILES

    @pl.when(jnp.logical_not(is_ctx))
    def _():
        heads(0, SMP_ROWS)

    @pl.when(is_ctx)
    def _():
        heads(SEQ, CTX)


def _attn0(qd, kd, vd, qm, km, vm, lam_vec, subln_g, lam_init, nb):
    q_map = lambda b, p: (b * SMP_TILES + p, 0)
    k_map = lambda b, p: (b, 0)
    const = lambda b, p: (0, 0)
    return pl.pallas_call(
        functools.partial(_attn0_kernel, lam_init),
        grid=(nb, SMP_TILES),
        in_specs=[pl.BlockSpec((TILE, 512), q_map),
                  pl.BlockSpec((SMP_ROWS, 512), k_map),
                  pl.BlockSpec((SMP_ROWS, 512), k_map),
                  pl.BlockSpec((TILE, 1024), q_map),
                  pl.BlockSpec((SMP_ROWS, 1024), k_map),
                  pl.BlockSpec((SMP_ROWS, 512), k_map),
                  pl.BlockSpec(lam_vec.shape, const),
                  pl.BlockSpec((1, 128), const)],
        out_specs=pl.BlockSpec((TILE, D), q_map),
        out_shape=jax.ShapeDtypeStruct((nb * SMP_ROWS, D), BF16),
        compiler_params=_params(("arbitrary", "arbitrary")),
        name="attn_even",
    )(qd, kd, vd, qm, km, vm, lam_vec, subln_g)


def _oproj_kernel(dual, *refs):
    if dual:
        (o_ref, xl_ref, xc_ref, mod_ref, wo_ref, pg_ref, fg_ref, rw_ref, rb_ref,
         xn_ref, hf_ref, mi_ref, mw_ref, seg_ref) = refs
        is_ctx = pl.program_id(0) % SMP_TILES == LAT_TILES
        x = jnp.where(is_ctx, xc_ref[...], xl_ref[...])
    else:
        (o_ref, x_ref, mod_ref, wo_ref, pg_ref, fg_ref, rw_ref, rb_ref,
         xn_ref, hf_ref, mi_ref, mw_ref, seg_ref) = refs
        x = x_ref[...]

    his, los = [], []
    for rs in (slice(0, TILE // 2), slice(TILE // 2, TILE)):
        y = _dot(o_ref[rs, :], wo_ref[...])
        xn = x[rs] + _mod(mod_ref, 2) * _rms(y, pg_ref[...])
        xn_ref[rs, :] = xn
        hf = _rms(xn, fg_ref[...]) * (1.0 + _mod(mod_ref, 4)) + _mod(mod_ref, 3)
        hi = hf.astype(BF16)
        hf_ref[rs, :] = hi
        his.append(hi)
        los.append((hf - hi.astype(F32)).astype(BF16))

    lt = _dot_nt(rw_ref[...], jnp.concatenate(his + los, axis=0))
    ne = N_EXPERTS
    logits = (lt[:ne, :TILE] + lt[ne:, :TILE]) + (lt[:ne, TILE:] + lt[ne:, TILE:]) + rb_ref[...]

    eid = lax.broadcasted_iota(jnp.int32, (ne, TILE), 0).astype(F32)
    sels, vals = [], []
    for _ in range(TOP_K):
        m = jnp.max(logits, axis=0, keepdims=True)
        idx = jnp.min(jnp.where(logits == m, eid, float(ne)), axis=0, keepdims=True)
        sel = eid == idx
        sels.append(sel)
        vals.append(m)
        logits = jnp.where(sel, -jnp.inf, logits)
    ex = [jnp.exp(v - vals[0]) for v in vals]
    den = ex[0] + ex[1] + ex[2] + ex[3]

    onehot = (sels[0] | sels[1] | sels[2] | sels[3]).astype(F32)
    r_i = lax.broadcasted_iota(jnp.int32, (TILE, TILE), 0)
    c_i = lax.broadcasted_iota(jnp.int32, (TILE, TILE), 1)
    before = _dot(onehot.astype(BF16), (r_i < c_i).astype(BF16))
    cnt = jnp.sum(onehot, axis=1, keepdims=True)
    seg_len = jnp.floor((cnt + (SEG_ALIGN - 1)) * (1.0 / SEG_ALIGN)) * SEG_ALIGN
    e_r = lax.broadcasted_iota(jnp.int32, (ne, ne), 0)
    e_c = lax.broadcasted_iota(jnp.int32, (ne, ne), 1)
    seg_len_b = jnp.broadcast_to(seg_len, (ne, META_LANES))
    seg_units = (seg_len_b * (1.0 / SEG_ALIGN)).astype(BF16)
    seg_off = _dot((e_c < e_r).astype(BF16), seg_units)[:, 0:1] * SEG_ALIGN
    pos = before + seg_off
    sub = lax.broadcasted_iota(jnp.int32, (8, TILE), 0)
    mi = jnp.zeros((8, TILE), jnp.int32)
    mw = jnp.zeros((8, TILE), F32)
    for k in range(TOP_K):
        row = jnp.sum(jnp.where(sels[k], pos, 0.0), axis=0, keepdims=True).astype(jnp.int32)
        mi = jnp.where(sub == k, row, mi)
        mw = jnp.where(sub == k, ex[k] / den, mw)
    mi_ref[0] = mi
    mw_ref[0] = mw
    seg_ref[0] = seg_len_b.astype(jnp.int32)


def _oproj(o, x_args, mods, wo, pg, fg, rw, rb, nb, all_tokens):
    const = lambda i: (0, 0)
    tile_map = lambda i: (i, 0)
    tile3_map = lambda i: (i, 0, 0)
    rw_hi = rw.astype(BF16)
    rw_lo = (rw - rw_hi.astype(F32)).astype(BF16)
    rw_split = jnp.concatenate([rw_hi.T, rw_lo.T], axis=0)
    if all_tokens:
        n_tiles = nb * SMP_TILES
        x_specs = [pl.BlockSpec((TILE, D), _x_lat_map), pl.BlockSpec((TILE, D), _x_ctx_map)]
        mod_map = _mod_map_all(nb)
    else:
        n_tiles = nb * LAT_TILES
        x_specs = [pl.BlockSpec((TILE, D), lambda i: (_lat_of_all(i), 0))]
        mod_map = lambda i: (i // LAT_TILES, 0, 0)
    rows = n_tiles * TILE
    return pl.pallas_call(
        functools.partial(_oproj_kernel, all_tokens),
        grid=(n_tiles,),
        in_specs=[pl.BlockSpec((TILE, D), tile_map)] + x_specs + [
            pl.BlockSpec((1, 1, 6 * D), mod_map),
            pl.BlockSpec((D, D), const),
            pl.BlockSpec((1, D), const),
            pl.BlockSpec((1, D), const),
            pl.BlockSpec((2 * N_EXPERTS, D), const),
            pl.BlockSpec((N_EXPERTS, 1), const)],
        out_specs=[pl.BlockSpec((TILE, D), tile_map),
                   pl.BlockSpec((TILE, D), tile_map),
                   pl.BlockSpec((1, 8, TILE), tile3_map),
                   pl.BlockSpec((1, 8, TILE), tile3_map),
                   pl.BlockSpec((1, N_EXPERTS, META_LANES), tile3_map)],
        out_shape=[jax.ShapeDtypeStruct((rows, D), F32),
                   jax.ShapeDtypeStruct((rows, D), BF16),
                   jax.ShapeDtypeStruct((n_tiles, 8, TILE), jnp.int32),
                   jax.ShapeDtypeStruct((n_tiles, 8, TILE), F32),
                   jax.ShapeDtypeStruct((n_tiles, N_EXPERTS, META_LANES), jnp.int32)],
        compiler_params=_params(("arbitrary",)),
        name="oproj_router",
    )(o, *x_args, mods, wo, pg, fg, rw_split, rb.reshape(N_EXPERTS, 1))


U32 = jnp.uint32
HALF = D // 2


def _pack_rows(x):
    bits = pltpu.bitcast(x.astype(BF16).astype(F32), U32)
    return (bits[:, :HALF] >> 16) | (bits[:, HALF:] & jnp.uint32(0xFFFF0000))


def _unpack_rows(w):
    lo = pltpu.bitcast(w << 16, F32).astype(BF16)
    hi = pltpu.bitcast(w & jnp.uint32(0xFFFF0000), F32).astype(BF16)
    return jnp.concatenate([lo, hi], axis=1)


def _for_pieces(n, largest, fn):
    piece = largest
    while piece >= SEG_ALIGN:
        done = (n // (2 * piece)) * (2 * piece)
        pl.when((n & piece) != 0)(functools.partial(fn, done, piece))
        piece //= 2


PIECES = tuple(TILE >> b for b in range((TILE // SEG_ALIGN).bit_length()))
N_PIECES = len(PIECES)
LIST_ROWS = 2 * N_PIECES + 1


def _piece_lists(seg_len, seg_local, seg_first):
    sizes = jnp.asarray(PIECES, jnp.int32)[None, :, None]
    n = seg_len[:, None, :]
    active = (n & sizes) != 0
    done = n // (2 * sizes) * (2 * sizes)
    slot = jnp.cumsum(active, axis=-1) - 1
    put = active[..., :, None] & (slot[..., :, None] == jnp.arange(N_EXPERTS))
    compact = lambda v: jnp.sum(jnp.where(put, v[..., :, None], 0), axis=-2)
    loc = compact(seg_local[:, None, :] + done)
    glb = compact(seg_first[:, None, :] + done)
    counts = jnp.sum(active, axis=-1)
    tail = jnp.concatenate([counts, jnp.sum(seg_len, axis=-1, keepdims=True)], axis=-1)
    tail = jnp.pad(tail, ((0, 0), (0, N_EXPERTS - tail.shape[-1])))[:, None, :]
    return jnp.concatenate([loc, glb, tail], axis=1).astype(jnp.int32)


STEP_TILES = 4


def _start_segment_copies(list_ref, u, local_ref, hbm_ref, sem, to_hbm):
    for b, rows in enumerate(PIECES):
        def start(j, carry, b=b, rows=rows):
            loc = local_ref.at[pl.ds(pl.multiple_of(list_ref[u, b, j], SEG_ALIGN), rows)]
            glb = hbm_ref.at[pl.ds(pl.multiple_of(list_ref[u, N_PIECES + b, j], SEG_ALIGN), rows)]
            (pltpu.make_async_copy(loc, glb, sem) if to_hbm else pltpu.make_async_copy(glb, loc, sem)).start()
            return carry

        lax.fori_loop(0, list_ref[u, 2 * N_PIECES, b], start, 0)


def _wait_segment_copies(list_ref, u, local_ref, hbm_ref, sem, to_hbm):
    def wait(done, rows):
        del done
        loc = local_ref.at[pl.ds(0, rows)]
        glb = hbm_ref.at[pl.ds(0, rows)]
        (pltpu.make_async_copy(loc, glb, sem) if to_hbm else pltpu.make_async_copy(glb, loc, sem)).wait()

    _for_pieces(list_ref[u, 2 * N_PIECES, N_PIECES], SORT_ROWS_POW2, wait)


def _zero_fill_copies(tail_ref, free_ref, zbuf, xs_ref, sem, action):
    def per_expert(e, carry):
        def piece_copy(done, rows):
            dst = xs_ref.at[pl.ds(pl.multiple_of(tail_ref[0, e] + done, SEG_ALIGN), rows)]
            action(pltpu.make_async_copy(zbuf.at[pl.ds(0, rows)], dst, sem))

        _for_pieces(tail_ref[1, e], EXPERT_BLOCK // 2, piece_copy)
        return carry

    def per_block(j, carry):
        first = pl.multiple_of(free_ref[0] + j * EXPERT_BLOCK, EXPERT_BLOCK)
        action(pltpu.make_async_copy(zbuf, xs_ref.at[pl.ds(first, EXPERT_BLOCK)], sem))
        return carry

    lax.fori_loop(0, N_EXPERTS, per_expert, 0)
    lax.fori_loop(0, free_ref[1], per_block, 0)


def _dispatch_kernel(seg_ref, seg_prev_ref, tail_ref, free_ref, lp_ref, h_ref, xs_ref, sbuf, zbuf, sem):
    i = pl.program_id(0)
    last = pl.num_programs(0) - 1
    slot = i % 2
    row = lax.broadcasted_iota(jnp.int32, (SORT_ROWS, TILE), 0)
    tiles = range(STEP_TILES)
    for u in tiles:
        lp = lp_ref[u]
        hit = row == lp[0:1]
        for k in range(1, TOP_K):
            hit = hit | (row == lp[k:k + 1])
        sbuf[slot, u] = _pack_rows(_dot(jnp.where(hit, 1.0, 0.0).astype(BF16),
                                        h_ref[u * TILE:(u + 1) * TILE, :]))
    for u in tiles:
        _start_segment_copies(seg_ref, u, sbuf.at[slot, u], xs_ref, sem.at[slot], True)

    @pl.when(i > 0)
    def _():
        for u in tiles:
            _wait_segment_copies(seg_prev_ref, u, sbuf.at[1 - slot, u], xs_ref, sem.at[1 - slot], True)

    @pl.when(i == last)
    def _():
        for u in tiles:
            _wait_segment_copies(seg_ref, u, sbuf.at[slot, u], xs_ref, sem.at[slot], True)
        zbuf[...] = jnp.zeros_like(zbuf)
        _zero_fill_copies(tail_ref, free_ref, zbuf, xs_ref, sem.at[slot], lambda cp: cp.start())
        _zero_fill_copies(tail_ref, free_ref, zbuf, xs_ref, sem.at[slot], lambda cp: cp.wait())


def _dispatch(segs, tails, free, lpos_t, hf, m_rows):
    n_steps = hf.shape[0] // (STEP_TILES * TILE)
    seg_block = (STEP_TILES, LIST_ROWS, N_EXPERTS)
    return pl.pallas_call(
        _dispatch_kernel,
        grid=(n_steps,),
        in_specs=[pl.BlockSpec(seg_block, lambda i: (i, 0, 0), memory_space=pltpu.SMEM),
                  pl.BlockSpec(seg_block, lambda i: (jnp.maximum(i - 1, 0), 0, 0), memory_space=pltpu.SMEM),
                  pl.BlockSpec(memory_space=pltpu.SMEM),
                  pl.BlockSpec(memory_space=pltpu.SMEM),
                  pl.BlockSpec((STEP_TILES, 8, TILE), lambda i: (i, 0, 0)),
                  pl.BlockSpec((STEP_TILES * TILE, D), lambda i: (i, 0))],
        out_specs=pl.BlockSpec(memory_space=pl.ANY),
        out_shape=jax.ShapeDtypeStruct((m_rows, HALF), U32),
        scratch_shapes=[pltpu.VMEM((2, STEP_TILES, SORT_ROWS, HALF), U32),
                        pltpu.VMEM((EXPERT_BLOCK, HALF), U32), pltpu.SemaphoreType.DMA((2,))],
        compiler_params=_params(("arbitrary",)),
        name="moe_dispatch",
    )(segs, segs, tails, free, lpos_t, hf)


def _expert_kernel(be_ref, nu_ref, xs_ref, w1_ref, b1_ref, w2_ref, b2_ref, ys_ref, w1b, w2b):
    i = pl.program_id(0)
    used = i < nu_ref[0]
    new_expert = jnp.logical_or(i == 0, be_ref[i] != be_ref[jnp.maximum(i - 1, 0)])

    @pl.when(jnp.logical_and(used, new_expert))
    def _():
        w1b[...] = w1_ref[0, 0].astype(BF16)
        w2b[...] = w2_ref[0, 0].astype(BF16)

    @pl.when(used)
    def _():
        gu = _dot(_unpack_rows(xs_ref[...]), w1b[...]) + b1_ref[0, 0]
        gate = jnp.minimum(gu[:, :D_FF], SWIGLU_LIMIT)
        lin = jnp.clip(gu[:, D_FF:], -SWIGLU_LIMIT, SWIGLU_LIMIT)
        act = gate * jax.nn.sigmoid(SWIGLU_ALPHA * gate) * (lin + 1.0)
        ys_ref[...] = _pack_rows(_dot(act.astype(BF16), w2b[...]) + b2_ref[0, 0])

    @pl.when(i >= nu_ref[0])
    def _():
        ys_ref[...] = jnp.zeros_like(ys_ref)


def _experts(blk_e, n_used, xs, layer, w1, b1, w2, b2):
    n_blocks = xs.shape[0] // EXPERT_BLOCK
    in_row_map = lambda i, be, nu: (jnp.minimum(i, nu[0] - 1), 0)
    e_map = lambda i, be, nu: (layer, be[i], 0, 0)
    grid_spec = pltpu.PrefetchScalarGridSpec(
        num_scalar_prefetch=2,
        grid=(n_blocks,),
        in_specs=[pl.BlockSpec((EXPERT_BLOCK, HALF), in_row_map),
                  pl.BlockSpec((1, 1, D, 2 * D_FF), e_map),
                  pl.BlockSpec((1, 1, 1, 2 * D_FF), e_map),
                  pl.BlockSpec((1, 1, D_FF, D), e_map),
                  pl.BlockSpec((1, 1, 1, D), e_map)],
        out_specs=pl.BlockSpec((EXPERT_BLOCK, HALF), lambda i, be, nu: (i, 0)),
        scratch_shapes=[pltpu.VMEM((D, 2 * D_FF), BF16), pltpu.VMEM((D_FF, D), BF16)])
    return pl.pallas_call(
        _expert_kernel,
        grid_spec=grid_spec,
        out_shape=jax.ShapeDtypeStruct(xs.shape, U32),
        compiler_params=_params(("arbitrary",)),
        name="moe_experts",
    )(blk_e, n_used, xs, w1, b1[:, :, None, :], w2, b2[:, :, None, :])


def _combine_kernel(seg_ref, seg_next_ref, mi_ref, mw_ref, x_ref, g_ref, ys_ref, *rest):
    mod_refs, (o_ref, ybuf, sem) = rest[:STEP_TILES], rest[STEP_TILES:]
    i = pl.program_id(0)
    slot = i % 2
    tiles = range(STEP_TILES)

    @pl.when(i == 0)
    def _():
        ybuf[...] = jnp.zeros_like(ybuf)
        for u in tiles:
            _start_segment_copies(seg_ref, u, ybuf.at[0, u], ys_ref, sem.at[0], False)

    @pl.when(i + 1 < pl.num_programs(0))
    def _():
        for u in tiles:
            _start_segment_copies(seg_next_ref, u, ybuf.at[1 - slot, u], ys_ref, sem.at[1 - slot], False)

    col = lax.broadcasted_iota(jnp.int32, (TILE, SORT_ROWS), 1)
    wms = []
    for u in tiles:
        rs = slice(u * TILE, (u + 1) * TILE)
        mi = mi_ref[rs, :]
        mw = mw_ref[rs, :]
        wm = jnp.zeros((TILE, SORT_ROWS), F32)
        for k in range(TOP_K):
            wm = jnp.where(col == mi[:, k:k + 1], mw[:, k:k + 1], wm)
        wms.append(wm.astype(BF16))
    for u in tiles:
        _wait_segment_copies(seg_ref, u, ybuf.at[slot, u], ys_ref, sem.at[slot], False)
    for u in tiles:
        rs = slice(u * TILE, (u + 1) * TILE)
        y = _dot(wms[u], _unpack_rows(ybuf[slot, u]))
        o_ref[rs, :] = x_ref[rs, :] + _mod(mod_refs[u], 5) * _rms(y, g_ref[...])


def _combine(segs, mi, mw, xn, mods, g, ys, nb, all_tokens):
    n_steps = xn.shape[0] // (STEP_TILES * TILE)
    step_map = lambda i: (i, 0)
    mod_map = _mod_map_all(nb) if all_tokens else (lambda i: (i // LAT_TILES, 0, 0))
    seg_block = (STEP_TILES, LIST_ROWS, N_EXPERTS)
    rows = STEP_TILES * TILE
    mod_specs = [pl.BlockSpec((1, 1, 6 * D), lambda i, u=u: mod_map(i * STEP_TILES + u))
                 for u in range(STEP_TILES)]
    return pl.pallas_call(
        _combine_kernel,
        grid=(n_steps,),
        in_specs=[pl.BlockSpec(seg_block, lambda i: (i, 0, 0), memory_space=pltpu.SMEM),
                  pl.BlockSpec(seg_block, lambda i: (jnp.minimum(i + 1, n_steps - 1), 0, 0),
                               memory_space=pltpu.SMEM),
                  pl.BlockSpec((rows, 8), step_map),
                  pl.BlockSpec((rows, 8), step_map),
                  pl.BlockSpec((rows, D), step_map),
                  pl.BlockSpec((1, D), lambda i: (0, 0)),
                  pl.BlockSpec(memory_space=pl.ANY)] + mod_specs,
        out_specs=pl.BlockSpec((rows, D), step_map),
        out_shape=jax.ShapeDtypeStruct(xn.shape, F32),
        scratch_shapes=[pltpu.VMEM((2, STEP_TILES, SORT_ROWS, HALF), U32), pltpu.SemaphoreType.DMA((2,))],
        compiler_params=_params(("arbitrary",)),
        name="moe_combine",
    )(segs, segs, mi, mw, xn, g, ys, *([mods] * STEP_TILES))


def _moe(hf, mi, mw, seg, xn, mods, g, layer, w1, b1, w2, b2, nb, all_tokens):
    t = hf.shape[0]
    n_tiles = t // TILE
    assert n_tiles % STEP_TILES == 0, "dispatch / combine take STEP_TILES token tiles per grid step"
    rows_max = t * TOP_K + n_tiles * N_EXPERTS * (SEG_ALIGN - 1)
    n_blocks = -(-rows_max // EXPERT_BLOCK) + N_EXPERTS
    seg_len = seg[:, :, 0]
    counts = jnp.sum(seg_len, axis=0)
    padded = (counts + EXPERT_BLOCK - 1) // EXPERT_BLOCK * EXPERT_BLOCK
    pad_end = jnp.cumsum(padded)
    pad_start = pad_end - padded
    seg_first = pad_start[None, :] + jnp.cumsum(seg_len, axis=0) - seg_len
    seg_local = jnp.cumsum(seg_len, axis=1) - seg_len
    segs = _piece_lists(seg_len, seg_local, seg_first)
    by_token = lambda a: a.transpose(0, 2, 1).reshape(t, 8)
    blk_row = jnp.arange(n_blocks, dtype=jnp.int32) * EXPERT_BLOCK
    blk_e = jnp.minimum(jnp.sum(pad_end[None, :] <= blk_row[:, None], axis=1), N_EXPERTS - 1).astype(jnp.int32)
    n_used = (pad_end[-1:] // EXPERT_BLOCK).astype(jnp.int32)
    tails = jnp.stack([pad_start + counts, padded - counts]).astype(jnp.int32)
    free = jnp.concatenate([pad_end[-1:], n_blocks - n_used]).astype(jnp.int32)
    xs = _dispatch(segs, tails, free, mi, hf, n_blocks * EXPERT_BLOCK)
    ys = _experts(blk_e, n_used, xs, layer, w1, b1, w2, b2)
    return _combine(segs, by_token(mi), by_token(mw), xn, mods, g, ys, nb, all_tokens)


def _proj1_kernel(x_ref, mod_ref, g_ref, w_ref, q_ref, k_ref, v_ref):
    h = (_rms(x_ref[...], g_ref[...]) * (1.0 + _mod(mod_ref, 1)) + _mod(mod_ref, 0)).astype(BF16)
    qkv = _dot(h, w_ref[...])
    q_ref[...] = (qkv[:, :D] * NA_SCALE).astype(BF16)
    k_ref[...] = qkv[:, D:2 * D].astype(BF16)
    v_ref[...] = qkv[:, 2 * D:].astype(BF16)


def _proj1(x1, mods, g, w, nb):
    n_tiles = nb * SMP_TILES
    tile_map = lambda i: (i, 0)
    return pl.pallas_call(
        _proj1_kernel,
        grid=(n_tiles,),
        in_specs=[pl.BlockSpec((TILE, D), tile_map),
                  pl.BlockSpec((1, 1, 6 * D), _mod_map_all(nb)),
                  pl.BlockSpec((1, D), lambda i: (0, 0)),
                  pl.BlockSpec((D, 3 * D), lambda i: (0, 0))],
        out_specs=[pl.BlockSpec((TILE, D), tile_map)] * 3,
        out_shape=[jax.ShapeDtypeStruct((n_tiles * TILE, D), BF16)] * 3,
        compiler_params=_params(("arbitrary",)),
        name="proj_odd",
    )(x1, mods, g, w)


def _na_row_start(r):
    return jnp.clip(r - WIN_ROWS // 2, 0, GRID_H - WIN_ROWS)


NA_STEP_ROWS = 4


def _na_kernel(q_ref, k_ref, v_ref, *rest):
    bias_refs, o_ref = rest[:NA_STEP_ROWS], rest[NA_STEP_ROWS]
    n_loc = WIN_ROWS * GRID_W
    lane = lax.broadcasted_iota(jnp.int32, (GRID_W, 128), 1)
    pairs = range(NA_HEADS // 2)
    sls = [slice(j * 128, (j + 1) * 128) for j in pairs]
    k0s, qqs = [], []
    for t in range(NA_STEP_ROWS):
        r = pl.program_id(1) * NA_STEP_ROWS + t
        k0s.append(pl.multiple_of(_na_row_start(r) * GRID_W, GRID_W))
        for j in pairs:
            q = q_ref[t * GRID_W:(t + 1) * GRID_W, sls[j]]
            zero = jnp.zeros_like(q)
            qqs.append(jnp.concatenate([jnp.where(lane < NA_DH, q, zero),
                                        jnp.where(lane >= NA_DH, q, zero)], axis=0))
    jobs = [(t, j) for t in range(NA_STEP_ROWS) for j in pairs]
    s_loc = jnp.concatenate([_dot_nt(qqs[n], k_ref[pl.ds(k0s[t], n_loc), sls[j]])
                             for n, (t, j) in enumerate(jobs)], axis=0)
    s_loc = s_loc + jnp.concatenate([b[0] for b in bias_refs], axis=0)
    s_ctx = jnp.concatenate([_dot_nt(qqs[n], k_ref[SEQ:, sls[j]]) for n, (t, j) in enumerate(jobs)], axis=0)
    m = jnp.maximum(jnp.max(s_loc, axis=-1, keepdims=True), jnp.max(s_ctx, axis=-1, keepdims=True))
    p_loc = jnp.exp(s_loc - m)
    p_ctx = jnp.exp(s_ctx - m)
    inv_l = 1.0 / (jnp.sum(p_loc, axis=-1, keepdims=True) + jnp.sum(p_ctx, axis=-1, keepdims=True))
    p_loc = p_loc.astype(BF16)
    p_ctx = p_ctx.astype(BF16)
    for n, (t, j) in enumerate(jobs):
        rows = slice(n * 2 * GRID_W, (n + 1) * 2 * GRID_W)
        pv = (_dot(p_loc[rows], v_ref[pl.ds(k0s[t], n_loc), sls[j]])
              + _dot(p_ctx[rows], v_ref[SEQ:, sls[j]])) * inv_l[rows]
        o_ref[t * GRID_W:(t + 1) * GRID_W, sls[j]] = jnp.where(lane < NA_DH, pv[:GRID_W],
                                                               pv[GRID_W:]).astype(BF16)


def _na_bias(rpb):
    mid = WIN_ROWS // 2
    pat_rows = list(range(mid)) + [mid] + list(range(GRID_H - mid + 1, GRID_H))
    r = np.array(pat_rows)
    rs = np.clip(r - mid, 0, GRID_H - WIN_ROWS)
    row_off = rs[:, None] + np.arange(WIN_ROWS)[None, :] - r[:, None] + WIN_ROWS - 1
    c = np.arange(GRID_W)
    q_start = np.clip(c - WIN_COLS // 2, 0, GRID_W - WIN_COLS)[:, None]
    kc = np.arange(GRID_W)[None, :]
    valid = (kc >= q_start) & (kc < q_start + WIN_COLS)
    col_off = np.clip(kc - c[:, None] + WIN_COLS - 1, 0, 2 * WIN_COLS - 2)
    sel_r = np.eye(2 * WIN_ROWS - 1, dtype=np.float32)[row_off]
    sel_c = np.eye(2 * WIN_COLS - 1, dtype=np.float32)[col_off]
    b = jnp.einsum('pia,hab,ckb->phcik', sel_r, rpb.astype(F32), sel_c, precision=lax.Precision.HIGHEST)
    b = jnp.where(valid[None, None, :, None, :], b, MASK_VALUE)
    return b.reshape(len(pat_rows), NA_HEADS * GRID_W, WIN_ROWS * GRID_W)


def _na_pattern(r):
    mid = WIN_ROWS // 2
    return jnp.where(r < mid, r, jnp.where(r <= GRID_H - mid, mid, r - (GRID_H - 2 * mid)))


def _na(q, k, v, bias, nb):
    q_rows = NA_STEP_ROWS * GRID_W
    steps = GRID_H // NA_STEP_ROWS
    blocks_per_smp = SMP_ROWS // q_rows
    bias_specs = [pl.BlockSpec((1,) + bias.shape[1:],
                               lambda b, s, t=t: (_na_pattern(s * NA_STEP_ROWS + t), 0, 0))
                  for t in range(NA_STEP_ROWS)]
    return pl.pallas_call(
        _na_kernel,
        grid=(nb, steps),
        in_specs=[pl.BlockSpec((q_rows, D), lambda b, s: (b * blocks_per_smp + s, 0)),
                  pl.BlockSpec((SMP_ROWS, D), lambda b, s: (b, 0)),
                  pl.BlockSpec((SMP_ROWS, D), lambda b, s: (b, 0))] + bias_specs,
        out_specs=pl.BlockSpec((q_rows, D), lambda b, s: (b * steps + s, 0)),
        out_shape=jax.ShapeDtypeStruct((nb * SEQ, D), BF16),
        compiler_params=_params(("arbitrary", "arbitrary")),
        name="na_attn",
    )(q, k, v, *([bias] * NA_STEP_ROWS))


def kernel(x, c, ctx, c_ctx, ada_w, ada_b, mix_pre_g, mix_post_g, ffn_pre_g, ffn_post_g, even_w_in,
           diff_lambda, diff_subln_g, mla_q_norm_g, mla_w_qb, mla_kv_norm_g, mla_w_kvb, even_w_out,
           na_w_qkv, na_rpb, na_w_out, router_w, router_b, moe_w1, moe_b1, moe_w2, moe_b2):
    nb = x.shape[0]
    assert x.shape[1:] == (SEQ, D) and ctx.shape[1:] == (CTX, D)
    x2 = x.reshape(nb * SEQ, D)
    c2 = ctx.reshape(nb * CTX, D)
    row = lambda a: a.reshape(1, -1)

    mods = _ada(jnp.concatenate([c, c_ctx[None, :]], axis=0), ada_w, ada_b)
    mods0 = mods[0].reshape(nb + 1, 1, 6 * D)
    mods1 = mods[1].reshape(nb + 1, 1, 6 * D)

    lam_init = 0.8 - 0.6 * math.exp(-0.3 * 0)
    w_in = even_w_in[0]
    wa = jnp.pad(w_in, ((0, 0), (0, 2048 - w_in.shape[1]))).astype(BF16)
    wqb = mla_w_qb[0].reshape(MLA_Q_LORA, MLA_HEADS, MLA_NOPE + MLA_ROPE)
    wqb = jnp.pad(wqb, ((0, 0), (0, 0), (0, MLA_QK_PAD - MLA_NOPE - MLA_ROPE)))
    wqb = wqb.reshape(MLA_Q_LORA, MLA_HEADS * MLA_QK_PAD).astype(BF16)
    qd, kd, vd, qm, km, vm = _proj0(x2, c2, mods0, row(mix_pre_g[0]), wa, row(mla_q_norm_g[0]), wqb,
                                    row(mla_kv_norm_g[0]), mla_w_kvb[0].astype(BF16), nb)
    o0 = _attn0(qd, kd, vd, qm, km, vm, diff_lambda[0], row(diff_subln_g[0]), lam_init, nb)
    xn, hf, mi, mw, seg = _oproj(o0, (x2, c2), mods0, even_w_out[0].astype(BF16), row(mix_post_g[0]),
                                 row(ffn_pre_g[0]), router_w[0], row(router_b[0]), nb, True)
    x1 = _moe(hf, mi, mw, seg, xn, mods0, row(ffn_post_g[0]), 0, moe_w1, moe_b1, moe_w2, moe_b2, nb, True)

    q, k, v = _proj1(x1, mods1, row(mix_pre_g[1]), na_w_qkv[0].astype(BF16), nb)
    o1 = _na(q, k, v, _na_bias(na_rpb[0]), nb)
    xn, hf, mi, mw, seg = _oproj(o1, (x1,), mods1, na_w_out[0].astype(BF16), row(mix_post_g[1]),
                                 row(ffn_pre_g[1]), router_w[1], row(router_b[1]), nb, False)
    out = _moe(hf, mi, mw, seg, xn, mods1, row(ffn_post_g[1]), 1, moe_w1, moe_b1, moe_w2, moe_b2, nb, False)
    return out.reshape(nb, SEQ, D)
```

```python
import functools
import math

import numpy as np
import jax
import jax.numpy as jnp
from jax import lax
from jax.experimental import pallas as pl
from jax.experimental.pallas import tpu as pltpu

F32 = jnp.float32
BF16 = jnp.bfloat16

D = 1024
SEQ = 2048
CTX = 256
GRID_W = 64
GRID_H = SEQ // GRID_W
TILE = 256
LAT_TILES = SEQ // TILE
SMP_TILES = LAT_TILES + 1
SMP_ROWS = SEQ + CTX
EPS = 1e-6
ROPE_BASE = 10000.0

DIFF_HEADS = 4
DIFF_DH = 64
DIFF_SCALE = DIFF_DH ** -0.5
MLA_HEADS = 4
MLA_Q_LORA = 256
MLA_KV_LORA = 128
MLA_NOPE = 128
MLA_ROPE = 64
MLA_V = 128
MLA_SCALE = (MLA_NOPE + MLA_ROPE) ** -0.5
MLA_QK_PAD = 256
NA_HEADS = 16
NA_DH = 64
NA_SCALE = NA_DH ** -0.5
WIN_ROWS = 8
WIN_COLS = 16
N_EXPERTS = 32
TOP_K = 4
D_FF = 1024
SWIGLU_ALPHA = 1.702
SWIGLU_LIMIT = 7.0
EXPERT_BLOCK = 1024
SEG_ALIGN = 8
SORT_ROWS = -(-(TILE * TOP_K + N_EXPERTS * (SEG_ALIGN - 1)) // 256) * 256
SORT_ROWS_POW2 = 1 << (SORT_ROWS.bit_length() - 1)
MASK_VALUE = -1e30
LOG2_E = math.log2(math.e)
KEY_CHUNK = 768
META_LANES = 128
VMEM_LIMIT = 60 * 1024 * 1024

_NT = (((1,), (1,)), ((), ()))


def _dot(a, b):
    return jnp.dot(a, b, preferred_element_type=F32)


def _dot_nt(a, b):
    return lax.dot_general(a, b, _NT, preferred_element_type=F32)


def _dot_f32(a, b):
    return jnp.dot(a, b, preferred_element_type=F32, precision=lax.Precision.HIGHEST)


def _rms(x, g):
    return x * lax.rsqrt(jnp.mean(x * x, axis=-1, keepdims=True) + EPS) * g


def _mod(mod_ref, k):
    return mod_ref[0, :, k * D:(k + 1) * D]


def _params(sem):
    return pltpu.CompilerParams(dimension_semantics=sem, vmem_limit_bytes=VMEM_LIMIT)


def _ada_kernel(c_ref, w_ref, b_ref, o_ref):
    c = c_ref[...]
    s = c * jax.nn.sigmoid(c)
    o_ref[0] = _dot_f32(s, w_ref[0]) + b_ref[0]


def _ada(cc, ada_w, ada_b):
    depth = ada_w.shape[0]
    n = cc.shape[0]
    nt = 6 * D // D
    return pl.pallas_call(
        _ada_kernel,
        grid=(depth, nt),
        in_specs=[pl.BlockSpec((n, D), lambda l, j: (0, 0)),
                  pl.BlockSpec((1, D, D), lambda l, j: (l, 0, j)),
                  pl.BlockSpec((1, 1, D), lambda l, j: (l, 0, j))],
        out_specs=pl.BlockSpec((1, n, D), lambda l, j: (l, 0, j)),
        out_shape=jax.ShapeDtypeStruct((depth, n, 6 * D), F32),
        compiler_params=_params(("arbitrary", "arbitrary")),
        name="ada",
    )(cc, ada_w, ada_b.reshape(depth, 1, 6 * D))


def _x_lat_map(i):
    return ((i // SMP_TILES) * LAT_TILES + jnp.minimum(i % SMP_TILES, LAT_TILES - 1), 0)


def _x_ctx_map(i):
    return (i // SMP_TILES, 0)


def _mod_map_all(nb):
    return lambda i: (jnp.where(i % SMP_TILES == LAT_TILES, nb, i // SMP_TILES), 0, 0)


def _lat_of_all(i):
    return (i // LAT_TILES) * SMP_TILES + i % LAT_TILES


def _rope(x, cos, sa, sb):
    return x * cos + pltpu.roll(x, 112, 1) * sa + pltpu.roll(x, 16, 1) * sb


def _proj0_kernel(x_ref, c_ref, mod_ref, g_ref, wa_ref, qng_ref, wqb_ref, kvng_ref, wkvb_ref,
                  cos_ref, sa_ref, sb_ref, qd_ref, kd_ref, vd_ref, qm_ref, km_ref, vm_ref):
    is_ctx = pl.program_id(0) % SMP_TILES == LAT_TILES
    nd = DIFF_HEADS * 2 * DIFF_DH
    o = 3 * nd
    halves = (slice(0, TILE // 2), slice(TILE // 2, TILE))
    bigs = []
    for rs in halves:
        x = jnp.where(is_ctx, c_ref[rs, :], x_ref[rs, :])
        h = (_rms(x, g_ref[...]) * (1.0 + _mod(mod_ref, 1)) + _mod(mod_ref, 0)).astype(BF16)
        bigs.append(_dot(h, wa_ref[...]))
    for rs, big in zip(halves, bigs):
        cos, sa, sb = cos_ref[rs, :], sa_ref[rs, :], sb_ref[rs, :]
        for j in range(nd // 128):
            sl = slice(j * 128, (j + 1) * 128)
            qd_ref[rs, sl] = _rope(big[:, j * 128:(j + 1) * 128], cos, sa, sb).astype(BF16)
            kd_ref[rs, sl] = _rope(big[:, nd + j * 128:nd + (j + 1) * 128], cos, sa, sb).astype(BF16)
        vd_ref[rs, :] = big[:, 2 * nd:3 * nd].astype(BF16)
        cq = big[:, o:o + MLA_Q_LORA]
        ckv = big[:, o + MLA_Q_LORA:o + MLA_Q_LORA + MLA_KV_LORA]
        kpe = _rope(big[:, o + MLA_Q_LORA + MLA_KV_LORA:], cos, sa, sb).astype(BF16)
        qm = _dot(_rms(cq, qng_ref[...]).astype(BF16), wqb_ref[...])
        kv = _dot(_rms(ckv, kvng_ref[...]).astype(BF16), wkvb_ref[...])
        for hd in range(MLA_HEADS):
            b0 = hd * MLA_QK_PAD
            qm_ref[rs, b0:b0 + 128] = qm[:, b0:b0 + 128].astype(BF16)
            qm_ref[rs, b0 + 128:b0 + 256] = _rope(qm[:, b0 + 128:b0 + 256], cos, sa, sb).astype(BF16)
            km_ref[rs, b0:b0 + 128] = kv[:, hd * 256:hd * 256 + 128].astype(BF16)
            km_ref[rs, b0 + 128:b0 + 256] = kpe
            vm_ref[rs, hd * 128:(hd + 1) * 128] = kv[:, hd * 256 + 128:(hd + 1) * 256].astype(BF16)


def _rope_tables():
    t = np.arange(SEQ)
    half = 16
    inv = (ROPE_BASE ** (-np.arange(half, dtype=np.float32) * 2.0 / (2 * half))).astype(np.float32)
    ang_r = (t // GRID_W).astype(np.float32)[:, None] * inv[None, :]
    ang_c = (t % GRID_W).astype(np.float32)[:, None] * inv[None, :]
    cr, sr, cc, sc = np.cos(ang_r), np.sin(ang_r), np.cos(ang_c), np.sin(ang_c)
    z = np.zeros_like(cr)
    cos = np.concatenate([cr, cr, cc, cc], axis=1)
    sa = np.concatenate([-sr, z, -sc, z], axis=1)
    sb = np.concatenate([z, sr, z, sc], axis=1)

    def full(tab, fill):
        tab = np.tile(tab, (1, 2))
        ident = np.full((CTX, 128), fill, np.float32)
        return jnp.asarray(np.concatenate([tab, ident], axis=0).astype(np.float32))

    return full(cos, 1.0), full(sa, 0.0), full(sb, 0.0)


def _proj0(x2, c2, mods, g, wa, qng, wqb, kvng, wkvb, nb):
    n_tiles = nb * SMP_TILES
    rows = nb * SMP_ROWS
    cos, sa, sb = _rope_tables()
    tile_map = lambda i: (i, 0)
    const = lambda i: (0, 0)
    tab_spec = pl.BlockSpec((TILE, 128), lambda i: (i % SMP_TILES, 0))
    widths = (512, 512, 512, 1024, 1024, 512)
    return pl.pallas_call(
        _proj0_kernel,
        grid=(n_tiles,),
        in_specs=[pl.BlockSpec((TILE, D), _x_lat_map),
                  pl.BlockSpec((TILE, D), _x_ctx_map),
                  pl.BlockSpec((1, 1, 6 * D), _mod_map_all(nb)),
                  pl.BlockSpec((1, D), const),
                  pl.BlockSpec(wa.shape, const),
                  pl.BlockSpec((1, MLA_Q_LORA), const),
                  pl.BlockSpec(wqb.shape, const),
                  pl.BlockSpec((1, MLA_KV_LORA), const),
                  pl.BlockSpec(wkvb.shape, const),
                  tab_spec, tab_spec, tab_spec],
        out_specs=[pl.BlockSpec((TILE, w), tile_map) for w in widths],
        out_shape=[jax.ShapeDtypeStruct((rows, w), BF16) for w in widths],
        compiler_params=_params(("arbitrary",)),
        name="proj_even",
    )(x2, c2, mods, g, wa, qng, wqb, kvng, wkvb, cos, sa, sb)


def _softmax_parts(s, scale):
    c = scale * LOG2_E
    m = jnp.max(s, axis=-1, keepdims=True)
    p = jnp.exp2(s * c - m * c)
    return p, jnp.sum(p, axis=-1, keepdims=True)


def _attn0_kernel(lam_init, qd_ref, kd_ref, vd_ref, qm_ref, km_ref, vm_ref, lam_ref, sg_ref, o_ref):
    lv = lam_ref[...]
    lam = (jnp.exp(jnp.sum(lv[0:1] * lv[1:2], axis=1, keepdims=True))
           - jnp.exp(jnp.sum(lv[2:3] * lv[3:4], axis=1, keepdims=True)) + lam_init)
    lane = lax.broadcasted_iota(jnp.int32, (TILE, 128), 1)

    def heads(k_lo, nk):
        chunk = min(nk, KEY_CHUNK)
        chunks = [(k_lo + c0, chunk) for c0 in range(0, nk, chunk)]

        def diff_q(hd):
            q = qd_ref[:, hd * 128:(hd + 1) * 128]
            zero = jnp.zeros_like(q)
            return jnp.concatenate([jnp.where(lane < DIFF_DH, q, zero),
                                    jnp.where(lane >= DIFF_DH, q, zero)], axis=0)

        def diff_store(hd, pv, l):
            sl = slice(hd * 128, (hd + 1) * 128)
            o = pv[:TILE] / l[:TILE] - lam * (pv[TILE:] / l[TILE:])
            o_ref[:, sl] = (_rms(o, sg_ref[...]) * (1.0 - lam_init)).astype(BF16)

        def mla_store(hd, pv, l):
            o_ref[:, 512 + hd * 128:512 + (hd + 1) * 128] = (pv / l).astype(BF16)

        heads_ = ([(diff_q(hd), kd_ref, slice(hd * 128, (hd + 1) * 128), vd_ref, slice(hd * 128, (hd + 1) * 128),
                   DIFF_SCALE, diff_store, hd) for hd in range(DIFF_HEADS)]
                  + [(qm_ref[:, hd * MLA_QK_PAD:(hd + 1) * MLA_QK_PAD], km_ref,
                      slice(hd * MLA_QK_PAD, (hd + 1) * MLA_QK_PAD), vm_ref, slice(hd * 128, (hd + 1) * 128),
                      MLA_SCALE, mla_store, hd) for hd in range(MLA_HEADS)])
        jobs = [(h, c) for h in range(len(heads_)) for c in range(len(chunks))]
        score = lambda h, c: _dot_nt(heads_[h][0], heads_[h][1][chunks[c][0]:chunks[c][0] + chunks[c][1], heads_[h][2]])
        s_next = score(*jobs[0])
        m = l = acc = None
        for n, (h, c) in enumerate(jobs):
            s = s_next
            if n + 1 < len(jobs):
                s_next = score(*jobs[n + 1])
            _, _, _, v_ref_, vsl, scale, store, hd = heads_[h]
            cs = scale * LOG2_E
            m_c = jnp.max(s, axis=-1, keepdims=True)
            m_new = m_c if c == 0 else jnp.maximum(m, m_c)
            p = jnp.exp2(s * cs - m_new * cs)
            pv = _dot(p.astype(BF16), v_ref_[chunks[c][0]:chunks[c][0] + chunks[c][1], vsl])
            l_c = jnp.sum(p, axis=-1, keepdims=True)
            if c == 0:
                l, acc = l_c, pv
            else:
                alpha = jnp.exp2((m - m_new) * cs)
                l, acc = l * alpha + l_c, acc * alpha + pv
            m = m_new
            if c == len(chunks) - 1:
                store(hd, acc, l)

    is_ctx = pl.program_id(1) == LAT_TILES

    @pl.when(jnp.logical_not(is_ctx))
    def _():
        heads(0, SMP_ROWS)

    @pl.when(is_ctx)
    def _():
        heads(SEQ, CTX)


def _attn0(qd, kd, vd, qm, km, vm, lam_vec, subln_g, lam_init, nb):
    q_map = lambda b, p: (b * SMP_TILES + p, 0)
    k_map = lambda b, p: (b, 0)
    const = lambda b, p: (0, 0)
    return pl.pallas_call(
        functools.partial(_attn0_kernel, lam_init),
        grid=(nb, SMP_TILES),
        in_specs=[pl.BlockSpec((TILE, 512), q_map),
                  pl.BlockSpec((SMP_ROWS, 512), k_map),
                  pl.BlockSpec((SMP_ROWS, 512), k_map),
                  pl.BlockSpec((TILE, 1024), q_map),
                  pl.BlockSpec((SMP_ROWS, 1024), k_map),
                  pl.BlockSpec((SMP_ROWS, 512), k_map),
                  pl.BlockSpec(lam_vec.shape, const),
                  pl.BlockSpec((1, 128), const)],
        out_specs=pl.BlockSpec((TILE, D), q_map),
        out_shape=jax.ShapeDtypeStruct((nb * SMP_ROWS, D), BF16),
        compiler_params=_params(("arbitrary", "arbitrary")),
        name="attn_even",
    )(qd, kd, vd, qm, km, vm, lam_vec, subln_g)


def _oproj_kernel(dual, *refs):
    if dual:
        (o_ref, xl_ref, xc_ref, mod_ref, wo_ref, pg_ref, fg_ref, rw_ref, rb_ref,
         xn_ref, hf_ref, mi_ref, mw_ref, seg_ref) = refs
        is_ctx = pl.program_id(0) % SMP_TILES == LAT_TILES
        x = jnp.where(is_ctx, xc_ref[...], xl_ref[...])
    else:
        (o_ref, x_ref, mod_ref, wo_ref, pg_ref, fg_ref, rw_ref, rb_ref,
         xn_ref, hf_ref, mi_ref, mw_ref, seg_ref) = refs
        x = x_ref[...]

    his, los = [], []
    for rs in (slice(0, TILE // 2), slice(TILE // 2, TILE)):
        y = _dot(o_ref[rs, :], wo_ref[...])
        xn = x[rs] + _mod(mod_ref, 2) * _rms(y, pg_ref[...])
        xn_ref[rs, :] = xn
        hf = _rms(xn, fg_ref[...]) * (1.0 + _mod(mod_ref, 4)) + _mod(mod_ref, 3)
        hi = hf.astype(BF16)
        hf_ref[rs, :] = hi
        his.append(hi)
        los.append((hf - hi.astype(F32)).astype(BF16))

    lt = _dot_nt(rw_ref[...], jnp.concatenate(his + los, axis=0))
    ne = N_EXPERTS
    logits = (lt[:ne, :TILE] + lt[ne:, :TILE]) + (lt[:ne, TILE:] + lt[ne:, TILE:]) + rb_ref[...]

    eid = lax.broadcasted_iota(jnp.int32, (ne, TILE), 0).astype(F32)
    sels, vals = [], []
    for _ in range(TOP_K):
        m = jnp.max(logits, axis=0, keepdims=True)
        idx = jnp.min(jnp.where(logits == m, eid, float(ne)), axis=0, keepdims=True)
        sel = eid == idx
        sels.append(sel)
        vals.append(m)
        logits = jnp.where(sel, -jnp.inf, logits)
    ex = [jnp.exp(v - vals[0]) for v in vals]
    den = ex[0] + ex[1] + ex[2] + ex[3]

    onehot = (sels[0] | sels[1] | sels[2] | sels[3]).astype(F32)
    r_i = lax.broadcasted_iota(jnp.int32, (TILE, TILE), 0)
    c_i = lax.broadcasted_iota(jnp.int32, (TILE, TILE), 1)
    before = _dot(onehot.astype(BF16), (r_i < c_i).astype(BF16))
    cnt = jnp.sum(onehot, axis=1, keepdims=True)
    seg_len = jnp.floor((cnt + (SEG_ALIGN - 1)) * (1.0 / SEG_ALIGN)) * SEG_ALIGN
    e_r = lax.broadcasted_iota(jnp.int32, (ne, ne), 0)
    e_c = lax.broadcasted_iota(jnp.int32, (ne, ne), 1)
    seg_len_b = jnp.broadcast_to(seg_len, (ne, META_LANES))
    seg_units = (seg_len_b * (1.0 / SEG_ALIGN)).astype(BF16)
    seg_off = _dot((e_c < e_r).astype(BF16), seg_units)[:, 0:1] * SEG_ALIGN
    pos = before + seg_off
    sub = lax.broadcasted_iota(jnp.int32, (8, TILE), 0)
    mi = jnp.zeros((8, TILE), jnp.int32)
    mw = jnp.zeros((8, TILE), F32)
    for k in range(TOP_K):
        row = jnp.sum(jnp.where(sels[k], pos, 0.0), axis=0, keepdims=True).astype(jnp.int32)
        mi = jnp.where(sub == k, row, mi)
        mw = jnp.where(sub == k, ex[k] / den, mw)
    mi_ref[0] = mi
    mw_ref[0] = mw
    seg_ref[0] = seg_len_b.astype(jnp.int32)


def _oproj(o, x_args, mods, wo, pg, fg, rw, rb, nb, all_tokens):
    const = lambda i: (0, 0)
    tile_map = lambda i: (i, 0)
    tile3_map = lambda i: (i, 0, 0)
    rw_hi = rw.astype(BF16)
    rw_lo = (rw - rw_hi.astype(F32)).astype(BF16)
    rw_split = jnp.concatenate([rw_hi.T, rw_lo.T], axis=0)
    if all_tokens:
        n_tiles = nb * SMP_TILES
        x_specs = [pl.BlockSpec((TILE, D), _x_lat_map), pl.BlockSpec((TILE, D), _x_ctx_map)]
        mod_map = _mod_map_all(nb)
    else:
        n_tiles = nb * LAT_TILES
        x_specs = [pl.BlockSpec((TILE, D), lambda i: (_lat_of_all(i), 0))]
        mod_map = lambda i: (i // LAT_TILES, 0, 0)
    rows = n_tiles * TILE
    return pl.pallas_call(
        functools.partial(_oproj_kernel, all_tokens),
        grid=(n_tiles,),
        in_specs=[pl.BlockSpec((TILE, D), tile_map)] + x_specs + [
            pl.BlockSpec((1, 1, 6 * D), mod_map),
            pl.BlockSpec((D, D), const),
            pl.BlockSpec((1, D), const),
            pl.BlockSpec((1, D), const),
            pl.BlockSpec((2 * N_EXPERTS, D), const),
            pl.BlockSpec((N_EXPERTS, 1), const)],
        out_specs=[pl.BlockSpec((TILE, D), tile_map),
                   pl.BlockSpec((TILE, D), tile_map),
                   pl.BlockSpec((1, 8, TILE), tile3_map),
                   pl.BlockSpec((1, 8, TILE), tile3_map),
                   pl.BlockSpec((1, N_EXPERTS, META_LANES), tile3_map)],
        out_shape=[jax.ShapeDtypeStruct((rows, D), F32),
                   jax.ShapeDtypeStruct((rows, D), BF16),
                   jax.ShapeDtypeStruct((n_tiles, 8, TILE), jnp.int32),
                   jax.ShapeDtypeStruct((n_tiles, 8, TILE), F32),
                   jax.ShapeDtypeStruct((n_tiles, N_EXPERTS, META_LANES), jnp.int32)],
        compiler_params=_params(("arbitrary",)),
        name="oproj_router",
    )(o, *x_args, mods, wo, pg, fg, rw_split, rb.reshape(N_EXPERTS, 1))


U32 = jnp.uint32
HALF = D // 2


def _pack_rows(x):
    bits = pltpu.bitcast(x.astype(BF16).astype(F32), U32)
    return (bits[:, :HALF] >> 16) | (bits[:, HALF:] & jnp.uint32(0xFFFF0000))


def _unpack_rows(w):
    lo = pltpu.bitcast(w << 16, F32).astype(BF16)
    hi = pltpu.bitcast(w & jnp.uint32(0xFFFF0000), F32).astype(BF16)
    return jnp.concatenate([lo, hi], axis=1)


def _for_pieces(n, largest, fn):
    piece = largest
    while piece >= SEG_ALIGN:
        done = (n // (2 * piece)) * (2 * piece)
        pl.when((n & piece) != 0)(functools.partial(fn, done, piece))
        piece //= 2


PIECES = tuple(TILE >> b for b in range((TILE // SEG_ALIGN).bit_length()))
N_PIECES = len(PIECES)
LIST_ROWS = 2 * N_PIECES + 1


def _piece_lists(seg_len, seg_local, seg_first):
    sizes = jnp.asarray(PIECES, jnp.int32)[None, :, None]
    n = seg_len[:, None, :]
    active = (n & sizes) != 0
    done = n // (2 * sizes) * (2 * sizes)
    slot = jnp.cumsum(active, axis=-1) - 1
    put = active[..., :, None] & (slot[..., :, None] == jnp.arange(N_EXPERTS))
    compact = lambda v: jnp.sum(jnp.where(put, v[..., :, None], 0), axis=-2)
    loc = compact(seg_local[:, None, :] + done)
    glb = compact(seg_first[:, None, :] + done)
    counts = jnp.sum(active, axis=-1)
    tail = jnp.concatenate([counts, jnp.sum(seg_len, axis=-1, keepdims=True)], axis=-1)
    tail = jnp.pad(tail, ((0, 0), (0, N_EXPERTS - tail.shape[-1])))[:, None, :]
    return jnp.concatenate([loc, glb, tail], axis=1).astype(jnp.int32)


STEP_TILES = 4


def _start_segment_copies(list_ref, u, local_ref, hbm_ref, sem, to_hbm):
    for b, rows in enumerate(PIECES):
        def start(j, carry, b=b, rows=rows):
            loc = local_ref.at[pl.ds(pl.multiple_of(list_ref[u, b, j], SEG_ALIGN), rows)]
            glb = hbm_ref.at[pl.ds(pl.multiple_of(list_ref[u, N_PIECES + b, j], SEG_ALIGN), rows)]
            (pltpu.make_async_copy(loc, glb, sem) if to_hbm else pltpu.make_async_copy(glb, loc, sem)).start()
            return carry

        lax.fori_loop(0, list_ref[u, 2 * N_PIECES, b], start, 0)


def _wait_segment_copies(list_ref, u, local_ref, hbm_ref, sem, to_hbm):
    def wait(done, rows):
        del done
        loc = local_ref.at[pl.ds(0, rows)]
        glb = hbm_ref.at[pl.ds(0, rows)]
        (pltpu.make_async_copy(loc, glb, sem) if to_hbm else pltpu.make_async_copy(glb, loc, sem)).wait()

    _for_pieces(list_ref[u, 2 * N_PIECES, N_PIECES], SORT_ROWS_POW2, wait)


def _zero_fill_copies(tail_ref, free_ref, zbuf, xs_ref, sem, action):
    def per_expert(e, carry):
        def piece_copy(done, rows):
            dst = xs_ref.at[pl.ds(pl.multiple_of(tail_ref[0, e] + done, SEG_ALIGN), rows)]
            action(pltpu.make_async_copy(zbuf.at[pl.ds(0, rows)], dst, sem))

        _for_pieces(tail_ref[1, e], EXPERT_BLOCK // 2, piece_copy)
        return carry

    def per_block(j, carry):
        first = pl.multiple_of(free_ref[0] + j * EXPERT_BLOCK, EXPERT_BLOCK)
        action(pltpu.make_async_copy(zbuf, xs_ref.at[pl.ds(first, EXPERT_BLOCK)], sem))
        return carry

    lax.fori_loop(0, N_EXPERTS, per_expert, 0)
    lax.fori_loop(0, free_ref[1], per_block, 0)


def _dispatch_kernel(seg_ref, seg_prev_ref, tail_ref, free_ref, lp_ref, h_ref, xs_ref, sbuf, zbuf, sem):
    i = pl.program_id(0)
    last = pl.num_programs(0) - 1
    slot = i % 2
    row = lax.broadcasted_iota(jnp.int32, (SORT_ROWS, TILE), 0)
    tiles = range(STEP_TILES)
    for u in tiles:
        lp = lp_ref[u]
        hit = row == lp[0:1]
        for k in range(1, TOP_K):
            hit = hit | (row == lp[k:k + 1])
        sbuf[slot, u] = _pack_rows(_dot(jnp.where(hit, 1.0, 0.0).astype(BF16),
                                        h_ref[u * TILE:(u + 1) * TILE, :]))
    for u in tiles:
        _start_segment_copies(seg_ref, u, sbuf.at[slot, u], xs_ref, sem.at[slot], True)

    @pl.when(i > 0)
    def _():
        for u in tiles:
            _wait_segment_copies(seg_prev_ref, u, sbuf.at[1 - slot, u], xs_ref, sem.at[1 - slot], True)

    @pl.when(i == last)
    def _():
        for u in tiles:
            _wait_segment_copies(seg_ref, u, sbuf.at[slot, u], xs_ref, sem.at[slot], True)
        zbuf[...] = jnp.zeros_like(zbuf)
        _zero_fill_copies(tail_ref, free_ref, zbuf, xs_ref, sem.at[slot], lambda cp: cp.start())
        _zero_fill_copies(tail_ref, free_ref, zbuf, xs_ref, sem.at[slot], lambda cp: cp.wait())


def _dispatch(segs, tails, free, lpos_t, hf, m_rows):
    n_steps = hf.shape[0] // (STEP_TILES * TILE)
    seg_block = (STEP_TILES, LIST_ROWS, N_EXPERTS)
    return pl.pallas_call(
        _dispatch_kernel,
        grid=(n_steps,),
        in_specs=[pl.BlockSpec(seg_block, lambda i: (i, 0, 0), memory_space=pltpu.SMEM),
                  pl.BlockSpec(seg_block, lambda i: (jnp.maximum(i - 1, 0), 0, 0), memory_space=pltpu.SMEM),
                  pl.BlockSpec(memory_space=pltpu.SMEM),
                  pl.BlockSpec(memory_space=pltpu.SMEM),
                  pl.BlockSpec((STEP_TILES, 8, TILE), lambda i: (i, 0, 0)),
                  pl.BlockSpec((STEP_TILES * TILE, D), lambda i: (i, 0))],
        out_specs=pl.BlockSpec(memory_space=pl.ANY),
        out_shape=jax.ShapeDtypeStruct((m_rows, HALF), U32),
        scratch_shapes=[pltpu.VMEM((2, STEP_TILES, SORT_ROWS, HALF), U32),
                        pltpu.VMEM((EXPERT_BLOCK, HALF), U32), pltpu.SemaphoreType.DMA((2,))],
        compiler_params=_params(("arbitrary",)),
        name="moe_dispatch",
    )(segs, segs, tails, free, lpos_t, hf)


def _expert_kernel(be_ref, nu_ref, xs_ref, w1_ref, b1_ref, w2_ref, b2_ref, ys_ref, w1b, w2b):
    i = pl.program_id(0)
    used = i < nu_ref[0]
    new_expert = jnp.logical_or(i == 0, be_ref[i] != be_ref[jnp.maximum(i - 1, 0)])

    @pl.when(jnp.logical_and(used, new_expert))
    def _():
        w1b[...] = w1_ref[0, 0].astype(BF16)
        w2b[...] = w2_ref[0, 0].astype(BF16)

    @pl.when(used)
    def _():
        gu = _dot(_unpack_rows(xs_ref[...]), w1b[...]) + b1_ref[0, 0]
        gate = jnp.minimum(gu[:, :D_FF], SWIGLU_LIMIT)
        lin = jnp.clip(gu[:, D_FF:], -SWIGLU_LIMIT, SWIGLU_LIMIT)
        act = gate * jax.nn.sigmoid(SWIGLU_ALPHA * gate) * (lin + 1.0)
        ys_ref[...] = _pack_rows(_dot(act.astype(BF16), w2b[...]) + b2_ref[0, 0])

    @pl.when(i >= nu_ref[0])
    def _():
        ys_ref[...] = jnp.zeros_like(ys_ref)


def _experts(blk_e, n_used, xs, layer, w1, b1, w2, b2):
    n_blocks = xs.shape[0] // EXPERT_BLOCK
    in_row_map = lambda i, be, nu: (jnp.minimum(i, nu[0] - 1), 0)
    e_map = lambda i, be, nu: (layer, be[i], 0, 0)
    grid_spec = pltpu.PrefetchScalarGridSpec(
        num_scalar_prefetch=2,
        grid=(n_blocks,),
        in_specs=[pl.BlockSpec((EXPERT_BLOCK, HALF), in_row_map),
                  pl.BlockSpec((1, 1, D, 2 * D_FF), e_map),
                  pl.BlockSpec((1, 1, 1, 2 * D_FF), e_map),
                  pl.BlockSpec((1, 1, D_FF, D), e_map),
                  pl.BlockSpec((1, 1, 1, D), e_map)],
        out_specs=pl.BlockSpec((EXPERT_BLOCK, HALF), lambda i, be, nu: (i, 0)),
        scratch_shapes=[pltpu.VMEM((D, 2 * D_FF), BF16), pltpu.VMEM((D_FF, D), BF16)])
    return pl.pallas_call(
        _expert_kernel,
        grid_spec=grid_spec,
        out_shape=jax.ShapeDtypeStruct(xs.shape, U32),
        compiler_params=_params(("arbitrary",)),
        name="moe_experts",
    )(blk_e, n_used, xs, w1, b1[:, :, None, :], w2, b2[:, :, None, :])


def _combine_kernel(seg_ref, seg_next_ref, mi_ref, mw_ref, x_ref, g_ref, ys_ref, *rest):
    mod_refs, (o_ref, ybuf, sem) = rest[:STEP_TILES], rest[STEP_TILES:]
    i = pl.program_id(0)
    slot = i % 2
    tiles = range(STEP_TILES)

    @pl.when(i == 0)
    def _():
        ybuf[...] = jnp.zeros_like(ybuf)
        for u in tiles:
            _start_segment_copies(seg_ref, u, ybuf.at[0, u], ys_ref, sem.at[0], False)

    @pl.when(i + 1 < pl.num_programs(0))
    def _():
        for u in tiles:
            _start_segment_copies(seg_next_ref, u, ybuf.at[1 - slot, u], ys_ref, sem.at[1 - slot], False)

    col = lax.broadcasted_iota(jnp.int32, (TILE, SORT_ROWS), 1)
    wms = []
    for u in tiles:
        rs = slice(u * TILE, (u + 1) * TILE)
        mi = mi_ref[rs, :]
        mw = mw_ref[rs, :]
        wm = jnp.zeros((TILE, SORT_ROWS), F32)
        for k in range(TOP_K):
            wm = jnp.where(col == mi[:, k:k + 1], mw[:, k:k + 1], wm)
        wms.append(wm.astype(BF16))
    for u in tiles:
        _wait_segment_copies(seg_ref, u, ybuf.at[slot, u], ys_ref, sem.at[slot], False)
    for u in tiles:
        rs = slice(u * TILE, (u + 1) * TILE)
        y = _dot(wms[u], _unpack_rows(ybuf[slot, u]))
        o_ref[rs, :] = x_ref[rs, :] + _mod(mod_refs[u], 5) * _rms(y, g_ref[...])


def _combine(segs, mi, mw, xn, mods, g, ys, nb, all_tokens):
    n_steps = xn.shape[0] // (STEP_TILES * TILE)
    step_map = lambda i: (i, 0)
    mod_map = _mod_map_all(nb) if all_tokens else (lambda i: (i // LAT_TILES, 0, 0))
    seg_block = (STEP_TILES, LIST_ROWS, N_EXPERTS)
    rows = STEP_TILES * TILE
    mod_specs = [pl.BlockSpec((1, 1, 6 * D), lambda i, u=u: mod_map(i * STEP_TILES + u))
                 for u in range(STEP_TILES)]
    return pl.pallas_call(
        _combine_kernel,
        grid=(n_steps,),
        in_specs=[pl.BlockSpec(seg_block, lambda i: (i, 0, 0), memory_space=pltpu.SMEM),
                  pl.BlockSpec(seg_block, lambda i: (jnp.minimum(i + 1, n_steps - 1), 0, 0),
                               memory_space=pltpu.SMEM),
                  pl.BlockSpec((rows, 8), step_map),
                  pl.BlockSpec((rows, 8), step_map),
                  pl.BlockSpec((rows, D), step_map),
                  pl.BlockSpec((1, D), lambda i: (0, 0)),
                  pl.BlockSpec(memory_space=pl.ANY)] + mod_specs,
        out_specs=pl.BlockSpec((rows, D), step_map),
        out_shape=jax.ShapeDtypeStruct(xn.shape, F32),
        scratch_shapes=[pltpu.VMEM((2, STEP_TILES, SORT_ROWS, HALF), U32), pltpu.SemaphoreType.DMA((2,))],
        compiler_params=_params(("arbitrary",)),
        name="moe_combine",
    )(segs, segs, mi, mw, xn, g, ys, *([mods] * STEP_TILES))


def _moe(hf, mi, mw, seg, xn, mods, g, layer, w1, b1, w2, b2, nb, all_tokens):
    t = hf.shape[0]
    n_tiles = t // TILE
    assert n_tiles % STEP_TILES == 0, "dispatch / combine take STEP_TILES token tiles per grid step"
    rows_max = t * TOP_K + n_tiles * N_EXPERTS * (SEG_ALIGN - 1)
    n_blocks = -(-rows_max // EXPERT_BLOCK) + N_EXPERTS
    seg_len = seg[:, :, 0]
    counts = jnp.sum(seg_len, axis=0)
    padded = (counts + EXPERT_BLOCK - 1) // EXPERT_BLOCK * EXPERT_BLOCK
    pad_end = jnp.cumsum(padded)
    pad_start = pad_end - padded
    seg_first = pad_start[None, :] + jnp.cumsum(seg_len, axis=0) - seg_len
    seg_local = jnp.cumsum(seg_len, axis=1) - seg_len
    segs = _piece_lists(seg_len, seg_local, seg_first)
    by_token = lambda a: a.transpose(0, 2, 1).reshape(t, 8)
    blk_row = jnp.arange(n_blocks, dtype=jnp.int32) * EXPERT_BLOCK
    blk_e = jnp.minimum(jnp.sum(pad_end[None, :] <= blk_row[:, None], axis=1), N_EXPERTS - 1).astype(jnp.int32)
    n_used = (pad_end[-1:] // EXPERT_BLOCK).astype(jnp.int32)
    tails = jnp.stack([pad_start + counts, padded - counts]).astype(jnp.int32)
    free = jnp.concatenate([pad_end[-1:], n_blocks - n_used]).astype(jnp.int32)
    xs = _dispatch(segs, tails, free, mi, hf, n_blocks * EXPERT_BLOCK)
    ys = _experts(blk_e, n_used, xs, layer, w1, b1, w2, b2)
    return _combine(segs, by_token(mi), by_token(mw), xn, mods, g, ys, nb, all_tokens)


def _proj1_kernel(x_ref, mod_ref, g_ref, w_ref, q_ref, k_ref, v_ref):
    h = (_rms(x_ref[...], g_ref[...]) * (1.0 + _mod(mod_ref, 1)) + _mod(mod_ref, 0)).astype(BF16)
    qkv = _dot(h, w_ref[...])
    q_ref[...] = (qkv[:, :D] * NA_SCALE).astype(BF16)
    k_ref[...] = qkv[:, D:2 * D].astype(BF16)
    v_ref[...] = qkv[:, 2 * D:].astype(BF16)


def _proj1(x1, mods, g, w, nb):
    n_tiles = nb * SMP_TILES
    tile_map = lambda i: (i, 0)
    return pl.pallas_call(
        _proj1_kernel,
        grid=(n_tiles,),
        in_specs=[pl.BlockSpec((TILE, D), tile_map),
                  pl.BlockSpec((1, 1, 6 * D), _mod_map_all(nb)),
                  pl.BlockSpec((1, D), lambda i: (0, 0)),
                  pl.BlockSpec((D, 3 * D), lambda i: (0, 0))],
        out_specs=[pl.BlockSpec((TILE, D), tile_map)] * 3,
        out_shape=[jax.ShapeDtypeStruct((n_tiles * TILE, D), BF16)] * 3,
        compiler_params=_params(("arbitrary",)),
        name="proj_odd",
    )(x1, mods, g, w)


def _na_row_start(r):
    return jnp.clip(r - WIN_ROWS // 2, 0, GRID_H - WIN_ROWS)


NA_STEP_ROWS = 4


def _na_kernel(q_ref, k_ref, v_ref, *rest):
    bias_refs, o_ref = rest[:NA_STEP_ROWS], rest[NA_STEP_ROWS]
    n_loc = WIN_ROWS * GRID_W
    lane = lax.broadcasted_iota(jnp.int32, (GRID_W, 128), 1)
    pairs = range(NA_HEADS // 2)
    sls = [slice(j * 128, (j + 1) * 128) for j in pairs]
    k0s, qqs = [], []
    for t in range(NA_STEP_ROWS):
        r = pl.program_id(1) * NA_STEP_ROWS + t
        k0s.append(pl.multiple_of(_na_row_start(r) * GRID_W, GRID_W))
        for j in pairs:
            q = q_ref[t * GRID_W:(t + 1) * GRID_W, sls[j]]
            zero = jnp.zeros_like(q)
            qqs.append(jnp.concatenate([jnp.where(lane < NA_DH, q, zero),
                                        jnp.where(lane >= NA_DH, q, zero)], axis=0))
    jobs = [(t, j) for t in range(NA_STEP_ROWS) for j in pairs]
    s_loc = jnp.concatenate([_dot_nt(qqs[n], k_ref[pl.ds(k0s[t], n_loc), sls[j]])
                             for n, (t, j) in enumerate(jobs)], axis=0)
    s_loc = s_loc + jnp.concatenate([b[0] for b in bias_refs], axis=0)
    s_ctx = jnp.concatenate([_dot_nt(qqs[n], k_ref[SEQ:, sls[j]]) for n, (t, j) in enumerate(jobs)], axis=0)
    m = jnp.maximum(jnp.max(s_loc, axis=-1, keepdims=True), jnp.max(s_ctx, axis=-1, keepdims=True))
    p_loc = jnp.exp(s_loc - m)
    p_ctx = jnp.exp(s_ctx - m)
    inv_l = 1.0 / (jnp.sum(p_loc, axis=-1, keepdims=True) + jnp.sum(p_ctx, axis=-1, keepdims=True))
    p_loc = p_loc.astype(BF16)
    p_ctx = p_ctx.astype(BF16)
    for n, (t, j) in enumerate(jobs):
        rows = slice(n * 2 * GRID_W, (n + 1) * 2 * GRID_W)
        pv = (_dot(p_loc[rows], v_ref[pl.ds(k0s[t], n_loc), sls[j]])
              + _dot(p_ctx[rows], v_ref[SEQ:, sls[j]])) * inv_l[rows]
        o_ref[t * GRID_W:(t + 1) * GRID_W, sls[j]] = jnp.where(lane < NA_DH, pv[:GRID_W],
                                                               pv[GRID_W:]).astype(BF16)


def _na_bias(rpb):
    mid = WIN_ROWS // 2
    pat_rows = list(range(mid)) + [mid] + list(range(GRID_H - mid + 1, GRID_H))
    r = np.array(pat_rows)
    rs = np.clip(r - mid, 0, GRID_H - WIN_ROWS)
    row_off = rs[:, None] + np.arange(WIN_ROWS)[None, :] - r[:, None] + WIN_ROWS - 1
    c = np.arange(GRID_W)
    q_start = np.clip(c - WIN_COLS // 2, 0, GRID_W - WIN_COLS)[:, None]
    kc = np.arange(GRID_W)[None, :]
    valid = (kc >= q_start) & (kc < q_start + WIN_COLS)
    col_off = np.clip(kc - c[:, None] + WIN_COLS - 1, 0, 2 * WIN_COLS - 2)
    sel_r = np.eye(2 * WIN_ROWS - 1, dtype=np.float32)[row_off]
    sel_c = np.eye(2 * WIN_COLS - 1, dtype=np.float32)[col_off]
    b = jnp.einsum('pia,hab,ckb->phcik', sel_r, rpb.astype(F32), sel_c, precision=lax.Precision.HIGHEST)
    b = jnp.where(valid[None, None, :, None, :], b, MASK_VALUE)
    return b.reshape(len(pat_rows), NA_HEADS * GRID_W, WIN_ROWS * GRID_W)


def _na_pattern(r):
    mid = WIN_ROWS // 2
    return jnp.where(r < mid, r, jnp.where(r <= GRID_H - mid, mid, r - (GRID_H - 2 * mid)))


def _na(q, k, v, bias, nb):
    q_rows = NA_STEP_ROWS * GRID_W
    steps = GRID_H // NA_STEP_ROWS
    blocks_per_smp = SMP_ROWS // q_rows
    bias_specs = [pl.BlockSpec((1,) + bias.shape[1:],
                               lambda b, s, t=t: (_na_pattern(s * NA_STEP_ROWS + t), 0, 0))
                  for t in range(NA_STEP_ROWS)]
    return pl.pallas_call(
        _na_kernel,
        grid=(nb, steps),
        in_specs=[pl.BlockSpec((q_rows, D), lambda b, s: (b * blocks_per_smp + s, 0)),
                  pl.BlockSpec((SMP_ROWS, D), lambda b, s: (b, 0)),
                  pl.BlockSpec((SMP_ROWS, D), lambda b, s: (b, 0))] + bias_specs,
        out_specs=pl.BlockSpec((q_rows, D), lambda b, s: (b * steps + s, 0)),
        out_shape=jax.ShapeDtypeStruct((nb * SEQ, D), BF16),
        compiler_params=_params(("arbitrary", "arbitrary")),
        name="na_attn",
    )(q, k, v, *([bias] * NA_STEP_ROWS))


def kernel(x, c, ctx, c_ctx, ada_w, ada_b, mix_pre_g, mix_post_g, ffn_pre_g, ffn_post_g, even_w_in,
           diff_lambda, diff_subln_g, mla_q_norm_g, mla_w_qb, mla_kv_norm_g, mla_w_kvb, even_w_out,
           na_w_qkv, na_rpb, na_w_out, router_w, router_b, moe_w1, moe_b1, moe_w2, moe_b2):
    nb = x.shape[0]
    assert x.shape[1:] == (SEQ, D) and ctx.shape[1:] == (CTX, D)
    x2 = x.reshape(nb * SEQ, D)
    c2 = ctx.reshape(nb * CTX, D)
    row = lambda a: a.reshape(1, -1)

    mods = _ada(jnp.concatenate([c, c_ctx[None, :]], axis=0), ada_w, ada_b)
    mods0 = mods[0].reshape(nb + 1, 1, 6 * D)
    mods1 = mods[1].reshape(nb + 1, 1, 6 * D)

    lam_init = 0.8 - 0.6 * math.exp(-0.3 * 0)
    w_in = even_w_in[0]
    wa = jnp.pad(w_in, ((0, 0), (0, 2048 - w_in.shape[1]))).astype(BF16)
    wqb = mla_w_qb[0].reshape(MLA_Q_LORA, MLA_HEADS, MLA_NOPE + MLA_ROPE)
    wqb = jnp.pad(wqb, ((0, 0), (0, 0), (0, MLA_QK_PAD - MLA_NOPE - MLA_ROPE)))
    wqb = wqb.reshape(MLA_Q_LORA, MLA_HEADS * MLA_QK_PAD).astype(BF16)
    qd, kd, vd, qm, km, vm = _proj0(x2, c2, mods0, row(mix_pre_g[0]), wa, row(mla_q_norm_g[0]), wqb,
                                    row(mla_kv_norm_g[0]), mla_w_kvb[0].astype(BF16), nb)
    o0 = _attn0(qd, kd, vd, qm, km, vm, diff_lambda[0], row(diff_subln_g[0]), lam_init, nb)
    xn, hf, mi, mw, seg = _oproj(o0, (x2, c2), mods0, even_w_out[0].astype(BF16), row(mix_post_g[0]),
                                 row(ffn_pre_g[0]), router_w[0], row(router_b[0]), nb, True)
    x1 = _moe(hf, mi, mw, seg, xn, mods0, row(ffn_post_g[0]), 0, moe_w1, moe_b1, moe_w2, moe_b2, nb, True)

    q, k, v = _proj1(x1, mods1, row(mix_pre_g[1]), na_w_qkv[0].astype(BF16), nb)
    o1 = _na(q, k, v, _na_bias(na_rpb[0]), nb)
    xn, hf, mi, mw, seg = _oproj(o1, (x1,), mods1, na_w_out[0].astype(BF16), row(mix_post_g[1]),
                                 row(ffn_pre_g[1]), router_w[1], row(router_b[1]), nb, False)
    out = _moe(hf, mi, mw, seg, xn, mods1, row(ffn_post_g[1]), 1, moe_w1, moe_b1, moe_w2, moe_b2, nb, False)
    return out.reshape(nb, SEQ, D)
```
